```python
import math
import jax
import jax.numpy as jnp
from jax import lax
import numpy as np

D_MODEL = 1024
BATCH = 4
SEQ = 8192
DEPTH = 1
DEC_BATCH = 32
DEC_SEQ = 8
PAST_LEN = 16384
PAGE_SIZE = 128

H_HG = 8
HG_DK = 64
HG_DV = 64
HG_KWIDTH = H_HG * HG_DK
HG_WIDTH = H_HG * HG_DV
HG_CHUNK = 64
H_MB = 8
MB_HD = 64
MB_WIDTH = H_MB * MB_HD
MB_BLOCK = 256
MB_TOPK = 3
MB_QCHUNK = 32
MIX_WIDTH = HG_WIDTH + MB_WIDTH
IN_COLS = 2 * HG_KWIDTH + 2 * HG_WIDTH + 3 * MB_WIDTH
NUM_BUCKETS = 32
MAX_DISTANCE = 8192
N_MEM = 256
H_MEM = 4
MEM_HD = 128
MEM_WIDTH = H_MEM * MEM_HD
D_FF = 2816
CONV_W = 3
EPS = 1e-6

kernel_name = 'hymba_hgrn2_moba_convffn_step'

F32 = jnp.float32


def rmsnorm(x, g):
    x32 = x.astype(F32)
    y = x32 * lax.rsqrt(jnp.mean(x32 * x32, axis=-1, keepdims=True) + EPS)
    return (y * g.astype(F32)).astype(x.dtype)


def heads(a, n_heads):
    b, t, _ = a.shape
    return a.reshape(b, t, n_heads, -1)


def rel_bucket(dist):
    n = jnp.maximum(dist, 0)
    max_exact = NUM_BUCKETS // 2
    nf = jnp.maximum(n, 1).astype(F32)
    large = max_exact + (jnp.log(nf / max_exact) / math.log(MAX_DISTANCE / max_exact)
                         * (NUM_BUCKETS - max_exact)).astype(jnp.int32)
    large = jnp.minimum(large, NUM_BUCKETS - 1)
    return jnp.where(n < max_exact, n, large)


def mixer_inputs(xn, w_in_l, lb_logits, layer, q_norm_g, k_norm_g):
    proj = jnp.einsum('btd,dc->btc', xn, w_in_l)
    c0 = 2 * HG_KWIDTH + 2 * HG_WIDTH
    cuts = [HG_KWIDTH, 2 * HG_KWIDTH, 2 * HG_KWIDTH + HG_WIDTH, c0, c0 + MB_WIDTH, c0 + 2 * MB_WIDTH]
    q_hg, f_hg, i_hg, g_hg, q_mb, k_mb, v_mb = jnp.split(proj, cuts, axis=-1)
    lb = jnp.cumsum(jax.nn.softmax(lb_logits.astype(F32), axis=0), axis=0)[layer]
    f_logit = f_hg.astype(F32)
    log_f = jnp.log(lb + (1.0 - lb) * jax.nn.sigmoid(f_logit))
    k_hg = (1.0 - lb) * jax.nn.sigmoid(-f_logit)
    hg = (heads(q_hg.astype(F32), H_HG), heads(k_hg, H_HG),
          heads(i_hg.astype(F32), H_HG), heads(log_f, H_HG))
    mb = (rmsnorm(heads(q_mb, H_MB), q_norm_g), rmsnorm(heads(k_mb, H_MB), k_norm_g), heads(v_mb, H_MB))
    return hg, g_hg, mb


def hgrn_chunk(s0, q, k, v, log_f):
    c = q.shape[2]
    a = jnp.cumsum(log_f, axis=2)
    causal = jnp.arange(c)[:, None] >= jnp.arange(c)[None, :]
    diff = a[:, :, :, None, :] - a[:, :, None, :, :]
    decay = jnp.exp(jnp.where(causal[None, None, :, :, None], diff, -jnp.inf))
    attn = jnp.einsum('bhtc,bhtsc,bhsc->bhts', q, decay, k)
    o = jnp.einsum('bhtc,bhcv->bhtv', q * jnp.exp(a), s0) + jnp.einsum('bhts,bhsv->bhtv', attn, v)
    a_end = a[:, :, -1:, :]
    s_new = jnp.exp(a_end[:, :, 0, :])[..., None] * s0 + jnp.einsum('bhsc,bhsv->bhcv', k * jnp.exp(a_end - a), v)
    return s_new, o


def hgrn_prompt(q, k, v, log_f):
    b, t = q.shape[:2]
    nc = t // HG_CHUNK

    def to_chunks(x):
        return x.reshape(b, nc, HG_CHUNK, H_HG, -1).transpose(1, 0, 3, 2, 4)

    s0 = jnp.zeros((b, H_HG, HG_DK, HG_DV), F32)
    s_fin, o = lax.scan(lambda s, xs: hgrn_chunk(s, *xs), s0,
                        (to_chunks(q), to_chunks(k), to_chunks(v), to_chunks(log_f)))
    return s_fin, o.transpose(1, 0, 3, 2, 4).reshape(b, t, H_HG, HG_DV)


def hgrn_step(s0, q, k, v, log_f):
    tr = lambda x: x.transpose(0, 2, 1, 3)
    s_new, o = hgrn_chunk(s0.astype(F32), tr(q), tr(k), tr(v), tr(log_f))
    return s_new, tr(o)


def moba_core(q, q_pos, q_block, kb, vb, kmean, k_own, v_own, own_pos, rel_bias):
    b, h, nq, _ = q.shape
    scale = MB_HD ** -0.5
    qf = q.astype(F32)
    rb = rel_bias.astype(F32).T
    s_own = jnp.einsum('bhqd,bhkd->bhqk', qf, k_own.astype(F32)) * scale \
        + rb[:, rel_bucket(q_pos[:, None] - own_pos[None, :])][None]
    s_own = jnp.where((own_pos[None, :] <= q_pos[:, None])[None, None], s_own, -jnp.inf)
    n_blocks = kmean.shape[2]
    n_top = min(MB_TOPK, n_blocks)
    if n_top == 0:
        p = jax.nn.softmax(s_own, axis=-1)
        return jnp.einsum('bhqk,bhkd->bhqd', p, v_own.astype(F32)).astype(q.dtype)
    gate = jnp.einsum('bhqd,bhnd->bhqn', qf, kmean)
    gate = jnp.where(jnp.arange(n_blocks) < q_block, gate, -jnp.inf)
    _, idx = lax.top_k(gate, n_top)
    valid = idx < q_block
    bi = jnp.arange(b)[:, None, None, None]
    hi = jnp.arange(h)[None, :, None, None]
    k_sel = kb[bi, hi, idx].astype(F32)
    v_sel = vb[bi, hi, idx].astype(F32)
    sel_pos = idx[..., None] * MB_BLOCK + jnp.arange(MB_BLOCK)
    bias_sel = rb[hi[..., None], rel_bucket(q_pos[:, None, None] - sel_pos)]
    s_sel = jnp.einsum('bhqd,bhqnkd->bhqnk', qf, k_sel) * scale + bias_sel
    s_sel = jnp.where(valid[..., None], s_sel, -jnp.inf)
    n_sel = n_top * MB_BLOCK
    p = jax.nn.softmax(jnp.concatenate([s_sel.reshape(b, h, nq, n_sel), s_own], axis=-1), axis=-1)
    o = jnp.einsum('bhqnk,bhqnkd->bhqd', p[..., :n_sel].reshape(b, h, nq, n_top, MB_BLOCK), v_sel) \
        + jnp.einsum('bhqk,bhkd->bhqd', p[..., n_sel:], v_own.astype(F32))
    return o.astype(q.dtype)


def moba_prompt(q, k, v, rel_bias):
    b, t = q.shape[:2]
    q, k, v = (x.transpose(0, 2, 1, 3) for x in (q, k, v))
    n_full = t // MB_BLOCK
    n_blk = -(-t // MB_BLOCK)
    pad = n_blk * MB_BLOCK - t
    kp = jnp.pad(k, ((0, 0), (0, 0), (0, pad), (0, 0))).reshape(b, H_MB, n_blk, MB_BLOCK, MB_HD)
    vp = jnp.pad(v, ((0, 0), (0, 0), (0, pad), (0, 0))).reshape(b, H_MB, n_blk, MB_BLOCK, MB_HD)
    kmean = kp[:, :, :n_full].astype(F32).mean(axis=3)

    def chunk(ci):
        start = ci * MB_QCHUNK
        qc = lax.dynamic_slice_in_dim(q, start, MB_QCHUNK, axis=2)
        q_pos = start + jnp.arange(MB_QCHUNK)
        blk = start // MB_BLOCK
        k_own = lax.dynamic_index_in_dim(kp, blk, axis=2, keepdims=False)
        v_own = lax.dynamic_index_in_dim(vp, blk, axis=2, keepdims=False)
        own_pos = blk * MB_BLOCK + jnp.arange(MB_BLOCK)
        return moba_core(qc, q_pos, blk, kp, vp, kmean, k_own, v_own, own_pos, rel_bias)

    out = lax.map(chunk, jnp.arange(t // MB_QCHUNK))
    return out.transpose(1, 0, 3, 2, 4).reshape(b, t, H_MB, MB_HD)


def moba_sample(q, k, v, k_pool, v_pool, page_table, rel_bias):
    db, ds = q.shape[:2]
    gathered_k = k_pool[page_table]
    past_len = gathered_k.shape[1] * gathered_k.shape[2]
    kpast = gathered_k.reshape(db, past_len, H_MB, MB_HD).transpose(0, 2, 1, 3)
    vpast = v_pool[page_table].reshape(db, past_len, H_MB, MB_HD).transpose(0, 2, 1, 3)
    n_full = past_len // MB_BLOCK
    own_start = n_full * MB_BLOCK
    kb = kpast[:, :, :own_start].reshape(db, H_MB, n_full, MB_BLOCK, MB_HD)
    vb = vpast[:, :, :own_start].reshape(db, H_MB, n_full, MB_BLOCK, MB_HD)
    kmean = kb.astype(F32).mean(axis=3)
    tr = lambda x: x.transpose(0, 2, 1, 3)
    k_own = jnp.concatenate([kpast[:, :, own_start:], tr(k).astype(kpast.dtype)], axis=2)
    v_own = jnp.concatenate([vpast[:, :, own_start:], tr(v).astype(vpast.dtype)], axis=2)
    own_pos = own_start + jnp.arange(past_len - own_start + ds)
    q_pos = past_len + jnp.arange(ds)
    o = moba_core(tr(q), q_pos, n_full, kb, vb, kmean, k_own, v_own, own_pos, rel_bias)
    return tr(o)


def mixer_out(o_hg, g, o_mb, out_norm_g, w_out_l):
    b, t = g.shape[:2]
    hg = rmsnorm(o_hg, out_norm_g).reshape(b, t, HG_WIDTH) * jax.nn.silu(g.astype(F32))
    cat = jnp.concatenate([hg.astype(g.dtype), o_mb.reshape(b, t, MB_WIDTH).astype(g.dtype)], axis=-1)
    return jnp.einsum('btc,cd->btd', cat, w_out_l)


def mem_kv(mem, g, w_kv, k_norm_g):
    kv = jnp.einsum('bnd,dc->bnc', rmsnorm(mem, g), w_kv)
    k, v = jnp.split(kv, 2, axis=-1)
    return rmsnorm(heads(k, H_MEM), k_norm_g), heads(v, H_MEM)


def mem_attend(h, g, w_q, q_norm_g, mk, mv, w_o):
    b, t = h.shape[:2]
    q = rmsnorm(heads(jnp.einsum('btd,dc->btc', rmsnorm(h, g), w_q), H_MEM), q_norm_g)
    s = jnp.einsum('bthd,bnhd->bhtn', q.astype(F32), mk.astype(F32)) * MEM_HD ** -0.5
    p = jax.nn.softmax(s, axis=-1)
    o = jnp.einsum('bhtn,bnhd->bthd', p, mv.astype(F32)).astype(h.dtype)
    return jnp.einsum('btc,cd->btd', o.reshape(b, t, MEM_WIDTH), w_o)


def conv_ffn(h, g, w_up, conv_w, conv_b, w_down, prev):
    u, v = jnp.split(jnp.einsum('btd,df->btf', rmsnorm(h, g), w_up), 2, axis=-1)
    t = u.shape[1]
    ext = jnp.concatenate([prev.astype(u.dtype), u], axis=1)
    c = conv_b + ext[:, 0:t] * conv_w[0]
    for j in range(1, CONV_W):
        c = c + ext[:, j:j + t] * conv_w[j]
    y = jnp.einsum('btf,fd->btd', jax.nn.gelu(c, approximate=False) * v, w_down)
    return y, ext[:, t:]


def setup_inputs(seed: int = 0) -> dict:
    key = jax.random.key(seed)
    ks = iter(jax.random.split(key, 40))
    nrm = lambda shape, scale=1.0: jax.random.normal(next(ks), shape, F32) * scale
    gain = lambda shape: 1.0 + 0.02 * nrm(shape)
    n_pages = PAST_LEN // PAGE_SIZE
    n_used = DEC_BATCH * n_pages
    n_phys = n_used + max(1, n_used // 4)
    x_prompt = nrm((BATCH, SEQ, D_MODEL))
    x_sample = nrm((DEC_BATCH, DEC_SEQ, D_MODEL))
    cache_k = nrm((DEPTH, n_phys, PAGE_SIZE, H_MB, MB_HD))
    cache_v = nrm((DEPTH, n_phys, PAGE_SIZE, H_MB, MB_HD))
    page_table = jax.random.permutation(next(ks), n_phys)[:n_used].reshape(DEC_BATCH, n_pages).astype(jnp.int32)
    state_hgrn = nrm((DEPTH, DEC_BATCH, H_HG, HG_DK, HG_DV), 0.5)
    state_conv = nrm((DEPTH, DEC_BATCH, CONV_W - 1, D_FF))
    cache_mem_k = nrm((DEPTH, DEC_BATCH, N_MEM, H_MEM, MEM_HD))
    cache_mem_v = nrm((DEPTH, DEC_BATCH, N_MEM, H_MEM, MEM_HD))
    mem_prompt = nrm((BATCH, N_MEM, D_MODEL))
    return {
        'x_prompt': x_prompt,
        'x_sample': x_sample,
        'cache_k': cache_k,
        'cache_v': cache_v,
        'page_table': page_table,
        'state_hgrn': state_hgrn,
        'state_conv': state_conv,
        'cache_mem_k': cache_mem_k,
        'cache_mem_v': cache_mem_v,
        'mem_prompt': mem_prompt,
        'norm_mix': gain((DEPTH, D_MODEL)),
        'w_in': nrm((DEPTH, D_MODEL, IN_COLS), D_MODEL ** -0.5),
        'hg_lb_logits': nrm((DEPTH + 1, HG_KWIDTH), 0.5),
        'hg_out_norm': gain((DEPTH, HG_DV)),
        'mb_q_norm': gain((DEPTH, MB_HD)),
        'mb_k_norm': gain((DEPTH, MB_HD)),
        'rel_bias': nrm((NUM_BUCKETS, H_MB), 0.5),
        'w_out': nrm((DEPTH, MIX_WIDTH, D_MODEL), MIX_WIDTH ** -0.5),
        'norm_mem': gain((DEPTH, D_MODEL)),
        'norm_mem_src': gain((DEPTH, D_MODEL)),
        'w_mem_q': nrm((DEPTH, D_MODEL, MEM_WIDTH), D_MODEL ** -0.5),
        'w_mem_kv': nrm((DEPTH, D_MODEL, 2 * MEM_WIDTH), D_MODEL ** -0.5),
        'mem_q_norm': gain((DEPTH, MEM_HD)),
        'mem_k_norm': gain((DEPTH, MEM_HD)),
        'w_mem_o': nrm((DEPTH, MEM_WIDTH, D_MODEL), MEM_WIDTH ** -0.5),
        'norm_ffn': gain((DEPTH, D_MODEL)),
        'w_up': nrm((DEPTH, D_MODEL, 2 * D_FF), D_MODEL ** -0.5),
        'conv_w': nrm((DEPTH, CONV_W, D_FF), CONV_W ** -0.5),
        'conv_b': nrm((DEPTH, D_FF), 0.02),
        'w_down': nrm((DEPTH, D_FF, D_MODEL), D_FF ** -0.5),
    }


def reference(x_prompt, x_sample, cache_k, cache_v, page_table, state_hgrn, state_conv,
              cache_mem_k, cache_mem_v, mem_prompt, norm_mix, w_in, hg_lb_logits, hg_out_norm,
              mb_q_norm, mb_k_norm, rel_bias, w_out, norm_mem, norm_mem_src, w_mem_q, w_mem_kv,
              mem_q_norm, mem_k_norm, w_mem_o, norm_ffn, w_up, conv_w, conv_b, w_down):
    hp, hs = x_prompt, x_sample
    kp_l, vp_l, ks_l, vs_l = [], [], [], []
    sp_l, ss_l, cp_l, cs_l, mkp_l, mvp_l = [], [], [], [], [], []
    for l in range(DEPTH):
        xn_p = rmsnorm(hp, norm_mix[l])
        xn_s = rmsnorm(hs, norm_mix[l])
        hg_p, g_p, (qm_p, km_p, vm_p) = mixer_inputs(xn_p, w_in[l], hg_lb_logits, l, mb_q_norm[l], mb_k_norm[l])
        hg_s, g_s, (qm_s, km_s, vm_s) = mixer_inputs(xn_s, w_in[l], hg_lb_logits, l, mb_q_norm[l], mb_k_norm[l])
        s_p, o_hg_p = hgrn_prompt(*hg_p)
        s_s, o_hg_s = hgrn_step(state_hgrn[l], *hg_s)
        o_mb_p = moba_prompt(qm_p, km_p, vm_p, rel_bias)
        o_mb_s = moba_sample(qm_s, km_s, vm_s, cache_k[l], cache_v[l], page_table, rel_bias)
        hp = hp + mixer_out(o_hg_p, g_p, o_mb_p, hg_out_norm[l], w_out[l])
        hs = hs + mixer_out(o_hg_s, g_s, o_mb_s, hg_out_norm[l], w_out[l])
        mk_p, mv_p = mem_kv(mem_prompt, norm_mem_src[l], w_mem_kv[l], mem_k_norm[l])
        hp = hp + mem_attend(hp, norm_mem[l], w_mem_q[l], mem_q_norm[l], mk_p, mv_p, w_mem_o[l])
        hs = hs + mem_attend(hs, norm_mem[l], w_mem_q[l], mem_q_norm[l], cache_mem_k[l], cache_mem_v[l], w_mem_o[l])
        prev_p = jnp.zeros((hp.shape[0], CONV_W - 1, D_FF), hp.dtype)
        f_p, c_p = conv_ffn(hp, norm_ffn[l], w_up[l], conv_w[l], conv_b[l], w_down[l], prev_p)
        f_s, c_s = conv_ffn(hs, norm_ffn[l], w_up[l], conv_w[l], conv_b[l], w_down[l], state_conv[l])
        hp = hp + f_p
        hs = hs + f_s
        kp_l.append(km_p)
        vp_l.append(vm_p)
        ks_l.append(km_s)
        vs_l.append(vm_s)
        sp_l.append(s_p)
        ss_l.append(s_s)
        cp_l.append(c_p)
        cs_l.append(c_s)
        mkp_l.append(mk_p)
        mvp_l.append(mv_p)
    y_prompt = hp
    y_sample = hs
    k_prompt = jnp.stack(kp_l)
    v_prompt = jnp.stack(vp_l)
    k_sample = jnp.stack(ks_l)
    v_sample = jnp.stack(vs_l)
    hgrn_prompt_state = jnp.stack(sp_l)
    hgrn_sample_state = jnp.stack(ss_l)
    conv_prompt = jnp.stack(cp_l)
    conv_sample = jnp.stack(cs_l)
    mem_k_prompt = jnp.stack(mkp_l)
    mem_v_prompt = jnp.stack(mvp_l)
    return (y_prompt, y_sample, k_prompt, v_prompt, k_sample, v_sample, hgrn_prompt_state, hgrn_sample_state, conv_prompt, conv_sample, mem_k_prompt, mem_v_prompt)
```

```python
import functools
import math

import numpy as np
import jax
import jax.numpy as jnp
from jax import lax
from jax.experimental import pallas as pl
from jax.experimental.pallas import tpu as pltpu

F32 = jnp.float32
BF16 = jnp.bfloat16
EPS = 1e-6

H_HG = 8
H_MB = 8
HEAD_D = 64
WIDTH = 512
MB_BLOCK = 256
MB_TOPK = 3
NUM_BUCKETS = 32
MAX_DISTANCE = 8192
H_MEM = 4
MEM_HD = 128
CONV_W = 3

LANES = 128
SUBLANES = 8
VMEM_LIMIT = 56 * 1024 * 1024

HG_SUB = 8
HG_GROUP = 128


def _cparams(*sem):
    return pltpu.CompilerParams(dimension_semantics=sem, vmem_limit_bytes=VMEM_LIMIT)


def _const_spec(shape):
    nd = len(shape)
    return pl.BlockSpec(shape, lambda *_: (0,) * nd, pipeline_mode=pl.Buffered(1))


def _group_ones(width, group):
    i = np.arange(width) // group
    return (i[:, None] == i[None, :]).astype(np.float32)


def _split_dot(x, ones_bf16):
    hi = x.astype(BF16)
    lo = (x - hi.astype(F32)).astype(BF16)
    return (jnp.dot(hi, ones_bf16, preferred_element_type=F32)
            + jnp.dot(lo, ones_bf16, preferred_element_type=F32))


def _seg_rmsnorm(x, gain, ones_bf16, seg):
    ms = _split_dot(x * x, ones_bf16) * (1.0 / seg)
    return x * lax.rsqrt(ms + EPS) * gain


def _rmsnorm_rows(x, gain):
    ms = jnp.mean(x * x, axis=-1, keepdims=True)
    return x * lax.rsqrt(ms + EPS) * gain


def _inproj_kernel(x_ref, gmix_ref, w_ref, wvt_ref, lbl_ref, qn_ref, kn_ref, ones_ref,
                   qhg_ref, khg_ref, ihg_ref, logf_ref, g_ref, qmb_ref, kmb_ref, vmb_ref, ksum_ref, vt_ref):
    xb = _rmsnorm_rows(x_ref[...], gmix_ref[...]).astype(BF16)
    vt_ref[...] = lax.dot_general(wvt_ref[...], xb, (((1,), (1,)), ((), ())), preferred_element_type=F32)

    def proj(i):
        return jnp.dot(xb, w_ref[:, i * WIDTH:(i + 1) * WIDTH], preferred_element_type=F32)

    qhg_ref[...] = proj(0)
    f_logit = proj(1)
    logits = lbl_ref[...]
    e = jnp.exp(logits - jnp.max(logits, axis=0, keepdims=True))
    lb = e[0:1] / jnp.sum(e, axis=0, keepdims=True)
    logf_ref[...] = jnp.log(lb + (1.0 - lb) * jax.nn.sigmoid(f_logit))
    khg_ref[...] = (1.0 - lb) * jax.nn.sigmoid(-f_logit)
    ihg_ref[...] = proj(2)
    g_ref[...] = proj(3)
    ones = ones_ref[...]
    qmb_ref[...] = _seg_rmsnorm(proj(4), qn_ref[...], ones, HEAD_D)
    k = _seg_rmsnorm(proj(5), kn_ref[...], ones, HEAD_D)
    kmb_ref[...] = k
    ksum_ref[0] = jnp.sum(k, axis=0, keepdims=True)
    vmb_ref[...] = proj(6)


def _inproj(x, gmix, w_bf16, lb_logits, qn, kn, tm):
    n, d = x.shape
    assert n % tm == 0
    nt = n // tm
    tok = pl.BlockSpec((tm, WIDTH), lambda i: (i, 0))
    outs = ([jax.ShapeDtypeStruct((n, WIDTH), F32)] * 8 + [jax.ShapeDtypeStruct((nt, 1, WIDTH), F32)]
            + [jax.ShapeDtypeStruct((WIDTH, n), F32)])
    ones = jnp.asarray(_group_ones(WIDTH, HEAD_D), BF16)
    wvt = w_bf16[:, 6 * WIDTH:].T
    return pl.pallas_call(
        _inproj_kernel,
        grid=(nt,),
        in_specs=[pl.BlockSpec((tm, d), lambda i: (i, 0)), _const_spec((1, d)), _const_spec(w_bf16.shape),
                  _const_spec(wvt.shape), _const_spec(lb_logits.shape), _const_spec((1, WIDTH)),
                  _const_spec((1, WIDTH)), _const_spec((WIDTH, WIDTH))],
        out_specs=[tok] * 8 + [pl.BlockSpec((1, 1, WIDTH), lambda i: (i, 0, 0)),
                               pl.BlockSpec((WIDTH, tm), lambda i: (0, i))],
        out_shape=outs,
        compiler_params=_cparams("parallel"),
        name="inproj",
    )(x, gmix, w_bf16, wvt, lb_logits, qn, kn, ones)


def _hgrn_kernel(q_ref, k_ref, v_ref, lf_ref, g_ref, s0_ref, gn_ref, ones_ref, gmask_ref,
                 o_ref, sfin_ref, st_scr, a_scr, o_scr, *, tc):
    c = pl.program_id(1)
    ng = WIDTH // HG_GROUP
    hpg = HG_GROUP // HEAD_D
    gmask = gmask_ref[...]

    @pl.when(c == 0)
    def _():
        s0 = s0_ref[0]
        for gi in range(ng):
            blk = s0[:, gi * HG_GROUP:(gi + 1) * HG_GROUP]
            st_scr[gi] = jnp.concatenate([blk] * hpg, axis=0) * gmask

    a = lf_ref[...]
    row = lax.broadcasted_iota(jnp.int32, a.shape, 0) % HG_SUB
    sh = 1
    while sh < HG_SUB:
        a = a + jnp.where(row >= sh, pltpu.roll(a, sh, 0), 0.0)
        sh *= 2
    a_scr[...] = a

    ones = ones_ref[...]
    trow = lax.broadcasted_iota(jnp.int32, (HG_SUB, WIDTH), 0)

    def step(n, carry):
        off = pl.multiple_of(n * HG_SUB, HG_SUB)
        q = q_ref[pl.ds(off, HG_SUB), :]
        k = k_ref[pl.ds(off, HG_SUB), :]
        v = v_ref[pl.ds(off, HG_SUB), :]
        al = a_scr[pl.ds(off, HG_SUB), :]
        a_end = al[HG_SUB - 1:HG_SUB, :]
        qe = (q * jnp.exp(al)).astype(BF16)
        kd = (k * jnp.exp(a_end - al)).astype(BF16)
        dec = jnp.exp(a_end)
        vb = v.astype(BF16)
        parts = []
        for s in range(HG_SUB):
            a_s = a_scr[pl.ds(off + s, 1), :]
            k_s = k_ref[pl.ds(off + s, 1), :]
            e_s = q * k_s * jnp.exp(al - a_s)
            parts.append(jnp.where(trow >= s, e_s, 0.0))
        ecat = jnp.concatenate(parts, axis=0).astype(BF16)
        outs = []
        for gi in range(ng):
            sl = slice(gi * HG_GROUP, (gi + 1) * HG_GROUP)
            st = st_scr[gi]
            o_g = lax.dot_general(qe[:, sl], st.astype(BF16), (((1,), (1,)), ((), ())),
                                  preferred_element_type=F32)
            ag = jnp.dot(ecat[:, sl], ones, preferred_element_type=F32)
            for s in range(HG_SUB):
                o_g = o_g + ag[s * HG_SUB:(s + 1) * HG_SUB, :] * v[s:s + 1, sl]
            outs.append(o_g)
            upd = lax.dot_general(vb[:, sl], kd[:, sl], (((0,), (0,)), ((), ())),
                                  preferred_element_type=F32)
            st_scr[gi] = st * dec[:, sl] + upd * gmask
        o_scr[pl.ds(off, HG_SUB), :] = jnp.concatenate(outs, axis=1)
        return carry

    lax.fori_loop(0, tc // HG_SUB, step, 0)

    o = o_scr[...]
    ms = jnp.concatenate(
        [_split_dot(o[:, gi * HG_GROUP:(gi + 1) * HG_GROUP] ** 2, ones) for gi in range(ng)], axis=1)
    g = g_ref[...]
    o_ref[...] = o * lax.rsqrt(ms * (1.0 / HEAD_D) + EPS) * gn_ref[...] * (g * jax.nn.sigmoid(g))

    @pl.when(c == pl.num_programs(1) - 1)
    def _():
        cols = []
        for gi in range(ng):
            st = st_scr[gi]
            acc = st[0:HEAD_D, :]
            for hh in range(1, hpg):
                acc = acc + st[hh * HEAD_D:(hh + 1) * HEAD_D, :]
            cols.append(acc)
        sfin_ref[0] = jnp.concatenate(cols, axis=1)


def _hgrn(q, k, v, logf, g, s0t, gn, batch, tc):
    n = q.shape[0]
    t = n // batch
    assert t % tc == 0 and tc % HG_SUB == 0
    nc = t // tc
    tok = pl.BlockSpec((tc, WIDTH), lambda b, c: (b * nc + c, 0))
    st_spec = pl.BlockSpec((1, HEAD_D, WIDTH), lambda b, c: (b, 0, 0))
    gm = _group_ones(HG_GROUP, HEAD_D)
    kern = functools.partial(_hgrn_kernel, tc=tc)
    return pl.pallas_call(
        kern,
        grid=(batch, nc),
        in_specs=[tok] * 5 + [st_spec, _const_spec((1, WIDTH)), _const_spec((HG_GROUP, HG_GROUP)),
                              _const_spec((HG_GROUP, HG_GROUP))],
        out_specs=[tok, st_spec],
        out_shape=[jax.ShapeDtypeStruct((n, WIDTH), F32), jax.ShapeDtypeStruct((batch, HEAD_D, WIDTH), F32)],
        scratch_shapes=[pltpu.VMEM((WIDTH // HG_GROUP, HG_GROUP, HG_GROUP), F32),
                        pltpu.VMEM((tc, WIDTH), F32), pltpu.VMEM((tc, WIDTH), F32)],
        compiler_params=_cparams("parallel", "arbitrary"),
        name="hgrn",
    )(q, k, v, logf, g, s0t, gn, jnp.asarray(gm, BF16), jnp.asarray(gm, F32))


NEG = -1e30


def _bucket_table(max_dist):
    max_exact = NUM_BUCKETS // 2
    d = np.arange(max_dist)
    nf = np.maximum(d, 1).astype(np.float32)
    large = max_exact + (np.log(nf / max_exact) / math.log(MAX_DISTANCE / max_exact)
                         * (NUM_BUCKETS - max_exact)).astype(np.int32)
    bucket = np.where(d < max_exact, d, np.minimum(large, NUM_BUCKETS - 1))
    assert np.all(np.diff(bucket) >= 0) and bucket[-1] == NUM_BUCKETS - 1
    first = [int(np.argmax(bucket >= kk)) for kk in range(NUM_BUCKETS)]
    return bucket, first


_BUCKET, _BUCKET_FIRST = _bucket_table(2 * MAX_DISTANCE)


def _bias_of_distance(d, rb_ref, h):
    acc = jnp.full(d.shape, rb_ref[0, h], F32)
    for kk in range(1, NUM_BUCKETS):
        acc = jnp.where(d >= _BUCKET_FIRST[kk], rb_ref[kk, h], acc)
    return acc


def _far_block_table(nb):
    big = 4 * MB_BLOCK
    tab = np.zeros((nb, 5), np.int32)
    for delta in range(nb):
        lo, hi = max(delta * MB_BLOCK - (MB_BLOCK - 1), 0), delta * MB_BLOCK + (MB_BLOCK - 1)
        b0 = int(_BUCKET[lo])
        ks = [kk for kk in range(b0 + 1, NUM_BUCKETS) if _BUCKET_FIRST[kk] <= hi]
        assert delta < 2 or len(ks) <= 2
        cs = [_BUCKET_FIRST[kk] - delta * MB_BLOCK for kk in ks[:2]] + [big, big]
        tab[delta] = [b0, min(b0 + 1, NUM_BUCKETS - 1), min(b0 + 2, NUM_BUCKETS - 1), cs[0], cs[1]]
    return tab


def _top_blocks(gate, n_valid, n_top):
    nb = gate.shape[0]
    j = lax.broadcasted_iota(jnp.int32, gate.shape, 0)
    g = jnp.where(j < n_valid, gate, -jnp.inf)
    sel = jnp.zeros(gate.shape, F32)
    idxs, oks = [], []
    for _ in range(n_top):
        mx = jnp.max(g, axis=0, keepdims=True)
        idx = jnp.min(jnp.where(g == mx, j, nb), axis=0, keepdims=True)
        ok = jnp.where(mx > -jnp.inf, 1.0, 0.0)
        pick = jnp.where(j == idx, ok, 0.0) > 0.0
        sel = jnp.where(pick, 1.0, sel)
        g = jnp.where(pick, -jnp.inf, g)
        idxs.append(idx)
        oks.append(ok)
    return idxs, oks, sel


def _pair_masks(shape):
    lane = lax.broadcasted_iota(jnp.int32, shape, len(shape) - 1)
    return [lane < HEAD_D, lane >= HEAD_D]


def _moba_prompt_kernel(qi_ref, kj_ref, tab_ref, q_ref, k_ref, vt_ref, ksum_ref, rb_ref, o_ref,
                        qm_scr, sel_scr, m_scr, l_scr, acc_scr, near_scr):
    p = pl.program_id(1)
    qi = qi_ref[p]
    kj = kj_ref[p]
    delta = qi - kj
    blk = MB_BLOCK
    dts = (lax.broadcasted_iota(jnp.int32, (blk, blk), 1)
           - lax.broadcasted_iota(jnp.int32, (blk, blk), 0))

    @pl.when(p == 0)
    def _():
        for h in range(H_MB):
            near_scr[h] = jnp.where(dts >= 0, _bias_of_distance(dts, rb_ref, h), NEG)
            near_scr[H_MB + h] = _bias_of_distance(dts + blk, rb_ref, h)

    @pl.when(kj == 0)
    def _():
        q = q_ref[...]
        kmean = ksum_ref[0] * (1.0 / blk)
        m_scr[...] = jnp.full(m_scr.shape, NEG, F32)
        l_scr[...] = jnp.zeros(l_scr.shape, F32)
        acc_scr[...] = jnp.zeros(acc_scr.shape, F32)
        masks = _pair_masks((blk, LANES))
        for h in range(H_MB):
            sl = slice((h // 2) * LANES, (h // 2 + 1) * LANES)
            qh = jnp.where(masks[h % 2], q[:, sl], 0.0)
            qm_scr[h] = (qh * (HEAD_D ** -0.5)).astype(BF16)
            gate = lax.dot_general(kmean[:, sl], qh, (((1,), (1,)), ((), ())),
                                   precision=lax.Precision.HIGHEST, preferred_element_type=F32)
            _, _, sel = _top_blocks(gate, qi, MB_TOPK)
            sel_scr[h] = jnp.where(sel > 0.0, 0.0, NEG)

    kb = k_ref[...].astype(BF16)
    c1 = tab_ref[delta, 3]
    c2 = tab_ref[delta, 4]
    m1 = dts >= c1
    m2 = dts >= c2
    for h in range(H_MB):
        sl = slice((h // 2) * LANES, (h // 2 + 1) * LANES)
        s = lax.dot_general(kb[:, sl], qm_scr[h], (((1,), (1,)), ((), ())),
                            preferred_element_type=F32)

        def near(h=h):
            return near_scr[jnp.minimum(delta, 1) * H_MB + h]

        def far(h=h):
            v0 = rb_ref[tab_ref[delta, 0], h]
            v1 = rb_ref[tab_ref[delta, 1], h]
            v2 = rb_ref[tab_ref[delta, 2], h]
            return jnp.where(m2, v2, jnp.where(m1, v1, v0))

        bias = lax.cond(delta < 2, near, far)
        selrow = jnp.where(delta == 0, 0.0, sel_scr[h, pl.ds(kj, 1), :])
        s = s + bias + selrow
        m_old = m_scr[h]
        m_new = jnp.maximum(m_old, jnp.max(s, axis=0, keepdims=True))
        alpha = jnp.exp(m_old - m_new)
        pexp = jnp.exp(s - m_new)
        l_scr[h] = alpha * l_scr[h] + jnp.sum(pexp, axis=0, keepdims=True)
        hs = slice(h * HEAD_D, (h + 1) * HEAD_D)
        pv = jnp.dot(vt_ref[hs, :].astype(BF16), pexp.astype(BF16), preferred_element_type=F32)
        acc_scr[hs, :] = alpha * acc_scr[hs, :] + pv
        m_scr[h] = m_new

    @pl.when(delta == 0)
    def _():
        for h in range(H_MB):
            hs = slice(h * HEAD_D, (h + 1) * HEAD_D)
            acc_scr[hs, :] = acc_scr[hs, :] / l_scr[h]
        o_ref[...] = acc_scr[...].T


def _moba_prompt(q, k, vt, ksum, rel_bias, batch):
    n = q.shape[0]
    t = n // batch
    assert t % MB_BLOCK == 0
    nb = t // MB_BLOCK
    qi = np.array([i for i in range(nb) for _ in range(i + 1)], np.int32)
    kj = np.array([j for i in range(nb) for j in range(i + 1)], np.int32)
    tab = _far_block_table(nb)
    blk = MB_BLOCK
    grid_spec = pltpu.PrefetchScalarGridSpec(
        num_scalar_prefetch=3,
        grid=(batch, len(qi)),
        in_specs=[
            pl.BlockSpec((blk, WIDTH), lambda b, p, qi, kj, tab: (b * nb + qi[p], 0)),
            pl.BlockSpec((blk, WIDTH), lambda b, p, qi, kj, tab: (b * nb + kj[p], 0)),
            pl.BlockSpec((WIDTH, blk), lambda b, p, qi, kj, tab: (0, b * nb + kj[p])),
            pl.BlockSpec((1, nb, WIDTH), lambda b, p, qi, kj, tab: (b, 0, 0)),
            pl.BlockSpec(memory_space=pltpu.SMEM),
        ],
        out_specs=pl.BlockSpec((blk, WIDTH), lambda b, p, qi, kj, tab: (b * nb + qi[p], 0)),
        scratch_shapes=[pltpu.VMEM((H_MB, blk, LANES), BF16), pltpu.VMEM((H_MB, nb, blk), F32),
                        pltpu.VMEM((H_MB, 1, blk), F32), pltpu.VMEM((H_MB, 1, blk), F32),
                        pltpu.VMEM((WIDTH, blk), F32), pltpu.VMEM((2 * H_MB, blk, blk), F32)],
    )
    return pl.pallas_call(
        _moba_prompt_kernel,
        grid_spec=grid_spec,
        out_shape=jax.ShapeDtypeStruct((n, WIDTH), F32),
        compiler_params=_cparams("parallel", "arbitrary"),
        name="moba_prompt",
    )(jnp.asarray(qi), jnp.asarray(kj), jnp.asarray(tab), q, k, vt, ksum, rel_bias)


SAMPLE_PAGES_PER_STEP = 16


def _page_ksum_kernel(pt_ref, *refs, ppb):
    pages, out_ref = refs[:-1], refs[-1]
    rows = []
    for i in range(0, len(pages), ppb):
        acc = jnp.sum(pages[i][0], axis=0, keepdims=True)
        for e in range(1, ppb):
            acc = acc + jnp.sum(pages[i + e][0], axis=0, keepdims=True)
        rows.append(acc)
    out_ref[0] = jnp.concatenate(rows, axis=0)


def _page_ksum(pool, page_table):
    db, n_pages = page_table.shape
    page = pool.shape[1]
    ppb = MB_BLOCK // page
    pps = SAMPLE_PAGES_PER_STEP
    assert MB_BLOCK % page == 0 and n_pages % pps == 0 and (pps // ppb) % SUBLANES == 0
    nblk = n_pages // ppb
    specs = [pl.BlockSpec((1, page, WIDTH), (lambda b, j, pt, i=i: (pt[b, j * pps + i], 0, 0))) for i in range(pps)]
    grid_spec = pltpu.PrefetchScalarGridSpec(
        num_scalar_prefetch=1, grid=(db, n_pages // pps), in_specs=specs,
        out_specs=pl.BlockSpec((1, pps // ppb, WIDTH), lambda b, j, pt: (b, j, 0)))
    return pl.pallas_call(
        functools.partial(_page_ksum_kernel, ppb=ppb), grid_spec=grid_spec,
        out_shape=jax.ShapeDtypeStruct((db, nblk, WIDTH), F32),
        compiler_params=_cparams("parallel", "arbitrary"), name="page_ksum",
    )(page_table, *([pool] * pps))


def _sample_select_kernel(q_ref, ksum_ref, idx_ref):
    q = q_ref[...]
    kmean = ksum_ref[0] * (1.0 / MB_BLOCK)
    nblk = kmean.shape[0]
    masks = _pair_masks((q.shape[0], LANES))
    rows = []
    for h in range(H_MB):
        sl = slice((h // 2) * LANES, (h // 2 + 1) * LANES)
        qh = jnp.where(masks[h % 2], q[:, sl], 0.0)
        gate = lax.dot_general(kmean[:, sl], qh, (((1,), (1,)), ((), ())),
                               precision=lax.Precision.HIGHEST, preferred_element_type=F32)
        idxs, _, _ = _top_blocks(gate, nblk, MB_TOPK)
        rows.extend(idxs)
    idx_ref[0] = jnp.concatenate(rows, axis=0)


def _sample_select(q, ksum, ds):
    db, nblk, _ = ksum.shape
    assert nblk >= MB_TOPK
    return pl.pallas_call(
        _sample_select_kernel, grid=(db,),
        in_specs=[pl.BlockSpec((ds, WIDTH), lambda b: (b, 0)), pl.BlockSpec((1, nblk, WIDTH), lambda b: (b, 0, 0))],
        out_specs=pl.BlockSpec((1, H_MB * MB_TOPK, ds), lambda b: (b, 0, 0)),
        out_shape=jax.ShapeDtypeStruct((db, H_MB * MB_TOPK, ds), jnp.int32),
        compiler_params=_cparams("parallel"), name="sample_select",
    )(q, ksum)


def _sample_attend_kernel(pt_ref, sel_ref, q_ref, kn_ref, vn_ref, rb_ref, *refs, ds, page, past_len):
    ppb = MB_BLOCK // page
    npg = MB_TOPK * ppb
    kpages, vpages, o_ref = refs[:npg], refs[npg:2 * npg], refs[2 * npg]
    b, hp, par, qq = pl.program_id(0), pl.program_id(1), pl.program_id(2), pl.program_id(3)
    h = 2 * hp + par
    lane = lax.broadcasted_iota(jnp.int32, (1, LANES), 1)
    hmask = (lane // HEAD_D) == par
    qrow = jnp.where(hmask, q_ref[pl.ds(qq, 1), :], 0.0) * (HEAD_D ** -0.5)
    q8 = jnp.broadcast_to(qrow, (SUBLANES, LANES))
    hi = lax.Precision.HIGHEST
    q_pos = past_len + qq

    def score(keys):
        return lax.dot_general(q8, keys, (((1,), (1,)), ((), ())), precision=hi,
                               preferred_element_type=F32)[0:1]

    scores, vals = [], []
    for r in range(MB_TOPK):
        blk = sel_ref[((b * ds + qq) * H_MB + h) * MB_TOPK + r]
        for e in range(ppb):
            kpos = blk * MB_BLOCK + e * page + lax.broadcasted_iota(jnp.int32, (1, page), 1)
            scores.append(score(kpages[r * ppb + e][0]) + _bias_of_distance(q_pos - kpos, rb_ref, h))
            vals.append(vpages[r * ppb + e][0])
    own = lax.broadcasted_iota(jnp.int32, (1, ds), 1)
    s_own = score(kn_ref[...]) + _bias_of_distance(qq - own, rb_ref, h)
    scores.append(jnp.where(own <= qq, s_own, NEG))
    vals.append(vn_ref[...])
    m = scores[0].max(axis=1, keepdims=True)
    for s in scores[1:]:
        m = jnp.maximum(m, s.max(axis=1, keepdims=True))
    den = jnp.zeros((1, 1), F32)
    o = jnp.zeros((1, LANES), F32)
    for s, v in zip(scores, vals):
        pexp = jnp.exp(s - m)
        den = den + pexp.sum(axis=1, keepdims=True)
        p8 = jnp.broadcast_to(pexp, (SUBLANES, pexp.shape[1]))
        o = o + jnp.dot(p8, v, precision=hi, preferred_element_type=F32)[0:1]
    o = o / den

    @pl.when((par == 0) & (qq == 0))
    def _():
        o_ref[...] = jnp.zeros(o_ref.shape, F32)

    row = lax.broadcasted_iota(jnp.int32, (ds, LANES), 0)
    hmask_rows = (lax.broadcasted_iota(jnp.int32, (ds, LANES), 1) // HEAD_D) == par
    o_ref[...] += jnp.where(hmask_rows, jnp.where(row == qq, o, 0.0), 0.0)


def _sample_attend(q, k_new, v_new, pool_k, pool_v, page_table, sel_flat, rel_bias, ds):
    db, n_pages = page_table.shape
    page = pool_k.shape[1]
    ppb = MB_BLOCK // page
    past_len = n_pages * page
    assert past_len % MB_BLOCK == 0

    def page_spec(r, e):
        def imap(b, hp, par, qq, pt, sel):
            blk = sel[((b * ds + qq) * H_MB + 2 * hp + par) * MB_TOPK + r]
            return (pt[b, blk * ppb + e], 0, hp)
        return pl.BlockSpec((1, page, LANES), imap)

    pages = [page_spec(r, e) for r in range(MB_TOPK) for e in range(ppb)]
    tok = pl.BlockSpec((ds, LANES), lambda b, hp, par, qq, pt, sel: (b, hp))
    grid_spec = pltpu.PrefetchScalarGridSpec(
        num_scalar_prefetch=2, grid=(db, H_MB // 2, 2, ds),
        in_specs=[tok, tok, tok, pl.BlockSpec(memory_space=pltpu.SMEM)] + pages + pages,
        out_specs=tok)
    kern = functools.partial(_sample_attend_kernel, ds=ds, page=page, past_len=past_len)
    npg = len(pages)
    return pl.pallas_call(
        kern, grid_spec=grid_spec, out_shape=jax.ShapeDtypeStruct(q.shape, F32),
        compiler_params=_cparams("parallel", "arbitrary", "arbitrary", "arbitrary"), name="sample_attend",
    )(page_table, sel_flat, q, k_new, v_new, rel_bias, *([pool_k] * npg), *([pool_v] * npg))


def _memkv_kernel(m_ref, g_ref, w_ref, kn_ref, ones_ref, k_ref, v_ref):
    xb = _rmsnorm_rows(m_ref[...], g_ref[...]).astype(BF16)
    wk = w_ref.shape[1] // 2
    k = jnp.dot(xb, w_ref[:, :wk], preferred_element_type=F32)
    k_ref[...] = _seg_rmsnorm(k, kn_ref[...], ones_ref[...], MEM_HD)
    v_ref[...] = jnp.dot(xb, w_ref[:, wk:], preferred_element_type=F32)


def _memkv(mem, g, w_bf16, kn, tm):
    n, d = mem.shape
    wk = w_bf16.shape[1] // 2
    out = pl.BlockSpec((tm, wk), lambda i: (i, 0))
    return pl.pallas_call(
        _memkv_kernel, grid=(n // tm,),
        in_specs=[pl.BlockSpec((tm, d), lambda i: (i, 0)), _const_spec((1, d)), _const_spec(w_bf16.shape),
                  _const_spec((1, wk)), _const_spec((wk, wk))],
        out_specs=[out, out], out_shape=[jax.ShapeDtypeStruct((n, wk), F32)] * 2,
        compiler_params=_cparams("parallel"), name="mem_kv",
    )(mem, g, w_bf16, kn, jnp.asarray(_group_ones(wk, MEM_HD), BF16))


def _mixmem_kernel(x_ref, hg_ref, omb_ref, wout_ref, gmem_ref, wq_ref, qn_ref, ones_ref, mk_ref, mv_ref, wo_ref,
                   h_ref):
    mix = (jnp.dot(hg_ref[...].astype(BF16), wout_ref[:WIDTH, :], preferred_element_type=F32)
           + jnp.dot(omb_ref[...].astype(BF16), wout_ref[WIDTH:, :], preferred_element_type=F32))
    h1 = x_ref[...] + mix
    hn = _rmsnorm_rows(h1, gmem_ref[...]).astype(BF16)
    q = _seg_rmsnorm(jnp.dot(hn, wq_ref[...], preferred_element_type=F32), qn_ref[...], ones_ref[...], MEM_HD)
    qb = q.astype(BF16)
    mk = mk_ref[0].astype(BF16)
    mv = mv_ref[0].astype(BF16)
    outs = []
    for h in range(H_MEM):
        sl = slice(h * MEM_HD, (h + 1) * MEM_HD)
        s = lax.dot_general(qb[:, sl], mk[:, sl], (((1,), (1,)), ((), ())),
                            preferred_element_type=F32) * (MEM_HD ** -0.5)
        e = jnp.exp(s - jnp.max(s, axis=-1, keepdims=True))
        p = e / jnp.sum(e, axis=-1, keepdims=True)
        outs.append(jnp.dot(p.astype(BF16), mv[:, sl], preferred_element_type=F32))
    o = jnp.concatenate(outs, axis=1).astype(BF16)
    h_ref[...] = h1 + jnp.dot(o, wo_ref[...], preferred_element_type=F32)


def _mixmem(x, hg, omb, wout, gmem, wq, qn, mk, mv, wo, batch, tm):
    n, d = x.shape
    t = n // batch
    assert t % tm == 0
    nt = t // tm
    n_mem, wm = mk.shape[1:]
    tok = lambda w: pl.BlockSpec((tm, w), lambda b, i: (b * nt + i, 0))
    mem = pl.BlockSpec((1, n_mem, wm), lambda b, i: (b, 0, 0))
    return pl.pallas_call(
        _mixmem_kernel, grid=(batch, nt),
        in_specs=[tok(d), tok(WIDTH), tok(WIDTH), _const_spec(wout.shape), _const_spec((1, d)),
                  _const_spec(wq.shape), _const_spec((1, wm)), _const_spec((wm, wm)), mem, mem,
                  _const_spec(wo.shape)],
        out_specs=tok(d), out_shape=jax.ShapeDtypeStruct((n, d), F32),
        compiler_params=_cparams("parallel", "parallel"), name="mix_mem",
    )(x, hg, omb, wout, gmem, wq, qn, jnp.asarray(_group_ones(wm, MEM_HD), BF16), mk, mv, wo)


FFN_CHUNKS = 2


def _ffn_kernel(h_ref, g_ref, wup_ref, cw_ref, cb_ref, wdn_ref, prev_ref, y_ref, cst_ref, carry_scr, *, dff):
    c = pl.program_id(1)

    @pl.when(c == 0)
    def _():
        carry_scr[...] = prev_ref[0]

    h = h_ref[...]
    hn = _rmsnorm_rows(h, g_ref[...]).astype(BF16)
    tm = h.shape[0]
    fc = dff // FFN_CHUNKS
    row = lax.broadcasted_iota(jnp.int32, (tm, fc), 0)
    acc = h
    for ci in range(FFN_CHUNKS):
        sl = slice(ci * fc, (ci + 1) * fc)
        u = jnp.dot(hn, wup_ref[:, sl], preferred_element_type=F32)
        v = jnp.dot(hn, wup_ref[:, dff + ci * fc:dff + (ci + 1) * fc], preferred_element_type=F32)
        prev = carry_scr[:, sl]
        u1 = jnp.where(row == 0, prev[1:2], pltpu.roll(u, 1, 0))
        u2 = jnp.where(row == 0, prev[0:1], jnp.where(row == 1, prev[1:2], pltpu.roll(u, 2, 0)))
        cw = cw_ref[:, sl]
        conv = cb_ref[:, sl] + u2 * cw[0:1] + u1 * cw[1:2] + u * cw[2:3]
        act = 0.5 * conv * (1.0 + lax.erf(conv * (2.0 ** -0.5))) * v
        acc = acc + jnp.dot(act.astype(BF16), wdn_ref[sl, :], preferred_element_type=F32)
        last = u[tm - (CONV_W - 1):tm]
        carry_scr[:, sl] = last
        cst_ref[0, :, sl] = last
    y_ref[...] = acc


def _ffn(h, g, wup, cw, cb, wdn, prev, batch, tm):
    n, d = h.shape
    t = n // batch
    assert t % tm == 0 and tm >= CONV_W - 1
    nt = t // tm
    dff = wdn.shape[0]
    assert dff % (FFN_CHUNKS * LANES) == 0
    tok = pl.BlockSpec((tm, d), lambda b, i: (b * nt + i, 0))
    st = pl.BlockSpec((1, CONV_W - 1, dff), lambda b, i: (b, 0, 0))
    return pl.pallas_call(
        functools.partial(_ffn_kernel, dff=dff), grid=(batch, nt),
        in_specs=[tok, _const_spec((1, d)), _const_spec(wup.shape), _const_spec(cw.shape), _const_spec((1, dff)),
                  _const_spec(wdn.shape), st],
        out_specs=[tok, st],
        out_shape=[jax.ShapeDtypeStruct((n, d), F32), jax.ShapeDtypeStruct((batch, CONV_W - 1, dff), F32)],
        scratch_shapes=[pltpu.VMEM((CONV_W - 1, dff), F32)],
        compiler_params=_cparams("parallel", "arbitrary"), name="conv_ffn",
    )(h, g, wup, cw, cb, wdn, prev)


PROMPT_TILE = 256


def _state_to_rows(s):
    b = s.shape[0]
    return s.transpose(0, 3, 1, 2).reshape(b, HEAD_D, WIDTH)


def _rows_to_state(st):
    b = st.shape[0]
    return st.reshape(b, HEAD_D, H_HG, HEAD_D).transpose(0, 2, 3, 1)


def kernel(x_prompt, x_sample, cache_k, cache_v, page_table, state_hgrn, state_conv, cache_mem_k, cache_mem_v, mem_prompt, norm_mix, w_in, hg_lb_logits, hg_out_norm, mb_q_norm, mb_k_norm, rel_bias, w_out, norm_mem, norm_mem_src, w_mem_q, w_mem_kv, mem_q_norm, mem_k_norm, w_mem_o, norm_ffn, w_up, conv_w, conv_b, w_down):
    assert norm_mix.shape[0] == 1, "one layer"
    l = 0
    b, t, dm = x_prompt.shape
    db, ds, _ = x_sample.shape
    n_mem = mem_prompt.shape[1]
    dff = w_down.shape[1]
    row = lambda a: a[None]
    w_in_b, w_out_b = w_in[l].astype(BF16), w_out[l].astype(BF16)
    w_q_b, w_kv_b, w_o_b = w_mem_q[l].astype(BF16), w_mem_kv[l].astype(BF16), w_mem_o[l].astype(BF16)
    w_up_b, w_dn_b = w_up[l].astype(BF16), w_down[l].astype(BF16)
    qn, kn = row(jnp.tile(mb_q_norm[l], H_MB)), row(jnp.tile(mb_k_norm[l], H_MB))
    gn = row(jnp.tile(hg_out_norm[l], H_HG))
    mqn, mkn = row(jnp.tile(mem_q_norm[l], H_MEM)), row(jnp.tile(mem_k_norm[l], H_MEM))

    def layer(x, batch, tile, s0t, moba, mk, mv, prev):
        seq = x.shape[0] // batch
        qhg, khg, ihg, logf, g, qmb, kmb, vmb, ksum, vt = _inproj(
            x, row(norm_mix[l]), w_in_b, hg_lb_logits, qn, kn, min(PROMPT_TILE, x.shape[0]))
        hg, st = _hgrn(qhg, khg, ihg, logf, g, s0t, gn, batch, tile)
        omb = moba(qmb, kmb, vmb, ksum, vt)
        h = _mixmem(x, hg, omb, w_out_b, row(norm_mem[l]), w_q_b, mqn, mk, mv, w_o_b, batch, tile)
        y, cst = _ffn(h, row(norm_ffn[l]), w_up_b, conv_w[l], row(conv_b[l]), w_dn_b, prev, batch, tile)
        return y, kmb, vmb, _rows_to_state(st), cst

    mk_p, mv_p = _memkv(mem_prompt.reshape(b * n_mem, dm), row(norm_mem_src[l]), w_kv_b, mkn, n_mem)
    moba_p = lambda q, k, v, ksum, vt: _moba_prompt(q, k, vt, ksum.reshape(b, t // MB_BLOCK, WIDTH), rel_bias, b)
    y_p, k_p, v_p, s_p, c_p = layer(
        x_prompt.reshape(b * t, dm), b, PROMPT_TILE, jnp.zeros((b, HEAD_D, WIDTH), F32), moba_p,
        mk_p.reshape(b, n_mem, -1), mv_p.reshape(b, n_mem, -1), jnp.zeros((b, CONV_W - 1, dff), F32))

    pool_k = cache_k[l].reshape(cache_k.shape[1], cache_k.shape[2], WIDTH)
    pool_v = cache_v[l].reshape(cache_v.shape[1], cache_v.shape[2], WIDTH)

    def moba_s(q, k, v, ksum, vt):
        sel = _sample_select(q, _page_ksum(pool_k, page_table), ds)
        sel_flat = sel.reshape(db, H_MB, MB_TOPK, ds).transpose(0, 3, 1, 2).reshape(-1)
        return _sample_attend(q, k, v, pool_k, pool_v, page_table, sel_flat, rel_bias, ds)

    y_s, k_s, v_s, s_s, c_s = layer(
        x_sample.reshape(db * ds, dm), db, ds, _state_to_rows(state_hgrn[l]), moba_s,
        cache_mem_k[l].reshape(db, n_mem, -1), cache_mem_v[l].reshape(db, n_mem, -1), state_conv[l])

    hd = lambda a, bb, tt: a.reshape(1, bb, tt, H_MB, HEAD_D)
    return (y_p.reshape(b, t, dm), y_s.reshape(db, ds, dm),
            hd(k_p, b, t), hd(v_p, b, t), hd(k_s, db, ds), hd(v_s, db, ds),
            s_p[None], s_s[None], c_p[None], c_s[None],
            mk_p.reshape(1, b, n_mem, H_MEM, MEM_HD), mv_p.reshape(1, b, n_mem, H_MEM, MEM_HD))
```

```python
import functools
import math

import numpy as np
import jax
import jax.numpy as jnp
from jax import lax
from jax.experimental import pallas as pl
from jax.experimental.pallas import tpu as pltpu

F32 = jnp.float32
BF16 = jnp.bfloat16
EPS = 1e-6

H_HG = 8
H_MB = 8
HEAD_D = 64
WIDTH = 512
MB_BLOCK = 256
MB_TOPK = 3
NUM_BUCKETS = 32
MAX_DISTANCE = 8192
H_MEM = 4
MEM_HD = 128
CONV_W = 3

LANES = 128
SUBLANES = 8
VMEM_LIMIT = 56 * 1024 * 1024

HG_SUB = 8
HG_GROUP = 128
HG_SEQS = 4
HG_TILE = 128


def _cparams(*sem):
    return pltpu.CompilerParams(dimension_semantics=sem, vmem_limit_bytes=VMEM_LIMIT)


def _const_spec(shape):
    nd = len(shape)
    return pl.BlockSpec(shape, lambda *_: (0,) * nd, pipeline_mode=pl.Buffered(1))


def _group_ones(width, group):
    i = np.arange(width) // group
    return (i[:, None] == i[None, :]).astype(np.float32)


def _split_dot(x, ones_bf16):
    hi = x.astype(BF16)
    lo = (x - hi.astype(F32)).astype(BF16)
    return (jnp.dot(hi, ones_bf16, preferred_element_type=F32)
            + jnp.dot(lo, ones_bf16, preferred_element_type=F32))


def _seg_rmsnorm(x, gain, ones_bf16, seg):
    ms = _split_dot(x * x, ones_bf16) * (1.0 / seg)
    return x * lax.rsqrt(ms + EPS) * gain


def _rmsnorm_rows(x, gain):
    ms = jnp.mean(x * x, axis=-1, keepdims=True)
    return x * lax.rsqrt(ms + EPS) * gain


def _inproj_kernel(x_ref, gmix_ref, w_ref, wvt_ref, lbl_ref, qn_ref, kn_ref, ones_ref,
                   qhg_ref, khg_ref, ihg_ref, logf_ref, g_ref, qmb_ref, kmb_ref, ksum_ref, kt_ref, vt_ref):
    xb = _rmsnorm_rows(x_ref[...], gmix_ref[...]).astype(BF16)

    def proj(i):
        return jnp.dot(xb, w_ref[:, i * WIDTH:(i + 1) * WIDTH], preferred_element_type=F32)

    qhg_ref[...] = proj(0)
    f_logit = proj(1)
    logits = lbl_ref[...]
    e = jnp.exp(logits - jnp.max(logits, axis=0, keepdims=True))
    lb = e[0:1] / jnp.sum(e, axis=0, keepdims=True)
    logf_ref[...] = jnp.log(lb + (1.0 - lb) * jax.nn.sigmoid(f_logit))
    khg_ref[...] = (1.0 - lb) * jax.nn.sigmoid(-f_logit)
    ihg_ref[...] = proj(2)
    g_ref[...] = proj(3)
    ones = ones_ref[...]
    qmb_ref[...] = _seg_rmsnorm(proj(4), qn_ref[...], ones, HEAD_D)
    k = _seg_rmsnorm(proj(5), kn_ref[...], ones, HEAD_D)
    kmb_ref[...] = k.astype(BF16)
    ksum_ref[0] = jnp.sum(k, axis=0, keepdims=True)
    kt_ref[0] = k.T
    vt_ref[0] = lax.dot_general(wvt_ref[...], xb, (((1,), (1,)), ((), ())), preferred_element_type=F32)


def _inproj(x, gmix, w_bf16, lb_logits, qn, kn, batch, tm):
    n, d = x.shape
    t = n // batch
    assert t % tm == 0
    tpb = t // tm
    tok = pl.BlockSpec((tm, WIDTH), lambda i: (i, 0))
    tr = pl.BlockSpec((1, WIDTH, tm), lambda i: (i // tpb, 0, i % tpb))
    outs = ([jax.ShapeDtypeStruct((n, WIDTH), F32)] * 6 + [jax.ShapeDtypeStruct((n, WIDTH), BF16)]
            + [jax.ShapeDtypeStruct((n // tm, 1, WIDTH), F32)]
            + [jax.ShapeDtypeStruct((batch, WIDTH, t), F32)] * 2)
    ones = jnp.asarray(_group_ones(WIDTH, HEAD_D), BF16)
    w_main, wvt = w_bf16[:, :6 * WIDTH], w_bf16[:, 6 * WIDTH:].T
    return pl.pallas_call(
        _inproj_kernel,
        grid=(n // tm,),
        in_specs=[pl.BlockSpec((tm, d), lambda i: (i, 0)), _const_spec((1, d)), _const_spec(w_main.shape),
                  _const_spec(wvt.shape), _const_spec(lb_logits.shape), _const_spec((1, WIDTH)),
                  _const_spec((1, WIDTH)), _const_spec((WIDTH, WIDTH))],
        out_specs=[tok] * 7 + [pl.BlockSpec((1, 1, WIDTH), lambda i: (i, 0, 0)), tr, tr],
        out_shape=outs,
        compiler_params=_cparams("parallel"),
        name="inproj",
    )(x, gmix, w_main, wvt, lb_logits, qn, kn, ones)


def _hgrn_kernel(q_ref, k_ref, v_ref, lf_ref, g_ref, s0_ref, gn_ref, ones_ref, gmask_ref,
                 o_ref, sfin_ref, st_scr, a_scr, o_scr, *, tc, nseq):
    c = pl.program_id(1)
    ng = WIDTH // HG_GROUP
    hpg = HG_GROUP // HEAD_D
    gmask = gmask_ref[...]

    @pl.when(c == 0)
    def _():
        for bi in range(nseq):
            s0 = s0_ref[bi]
            for gi in range(ng):
                blk = s0[:, gi * HG_GROUP:(gi + 1) * HG_GROUP]
                st_scr[bi * ng + gi] = jnp.concatenate([blk] * hpg, axis=0) * gmask

    row = lax.broadcasted_iota(jnp.int32, (tc, WIDTH), 0) % HG_SUB
    for bi in range(nseq):
        a = lf_ref[bi]
        sh = 1
        while sh < HG_SUB:
            a = a + jnp.where(row >= sh, pltpu.roll(a, sh, 0), 0.0)
            sh *= 2
        a_scr[bi] = a

    ones = ones_ref[...]
    trow = lax.broadcasted_iota(jnp.int32, (HG_SUB, WIDTH), 0)

    def step(n, carry):
        off = pl.multiple_of(n * HG_SUB, HG_SUB)
        for bi in range(nseq):
            q = q_ref[bi, pl.ds(off, HG_SUB), :]
            k = k_ref[bi, pl.ds(off, HG_SUB), :]
            v = v_ref[bi, pl.ds(off, HG_SUB), :]
            al = a_scr[bi, pl.ds(off, HG_SUB), :]
            a_end = al[HG_SUB - 1:HG_SUB, :]
            qe = (q * jnp.exp(al)).astype(BF16)
            kd = (k * jnp.exp(a_end - al)).astype(BF16)
            dec = jnp.exp(a_end)
            vb = v.astype(BF16)
            parts = []
            for s in range(HG_SUB):
                a_s = a_scr[bi, pl.ds(off + s, 1), :]
                k_s = k_ref[bi, pl.ds(off + s, 1), :]
                e_s = q * k_s * jnp.exp(al - a_s)
                parts.append(jnp.where(trow >= s, e_s, 0.0))
            ecat = jnp.concatenate(parts, axis=0).astype(BF16)
            outs = []
            for gi in range(ng):
                sl = slice(gi * HG_GROUP, (gi + 1) * HG_GROUP)
                st = st_scr[bi * ng + gi]
                o_g = lax.dot_general(qe[:, sl], st.astype(BF16), (((1,), (1,)), ((), ())),
                                      preferred_element_type=F32)
                ag = jnp.dot(ecat[:, sl], ones, preferred_element_type=F32)
                for s in range(HG_SUB):
                    o_g = o_g + ag[s * HG_SUB:(s + 1) * HG_SUB, :] * v[s:s + 1, sl]
                outs.append(o_g)
                upd = lax.dot_general(vb[:, sl], kd[:, sl], (((0,), (0,)), ((), ())),
                                      preferred_element_type=F32)
                st_scr[bi * ng + gi] = st * dec[:, sl] + upd * gmask
            o_scr[bi, pl.ds(off, HG_SUB), :] = jnp.concatenate(outs, axis=1)
        return carry

    lax.fori_loop(0, tc // HG_SUB, step, 0)

    for bi in range(nseq):
        o = o_scr[bi]
        ms = jnp.concatenate(
            [_split_dot(o[:, gi * HG_GROUP:(gi + 1) * HG_GROUP] ** 2, ones) for gi in range(ng)], axis=1)
        g = g_ref[bi]
        o_ref[bi] = o * lax.rsqrt(ms * (1.0 / HEAD_D) + EPS) * gn_ref[...] * (g * jax.nn.sigmoid(g))

    @pl.when(c == pl.num_programs(1) - 1)
    def _():
        for bi in range(nseq):
            cols = []
            for gi in range(ng):
                st = st_scr[bi * ng + gi]
                acc = st[0:HEAD_D, :]
                for hh in range(1, hpg):
                    acc = acc + st[hh * HEAD_D:(hh + 1) * HEAD_D, :]
                cols.append(acc)
            sfin_ref[bi] = jnp.concatenate(cols, axis=1)


def _hgrn(q, k, v, logf, g, s0t, gn, batch, tc):
    n = q.shape[0]
    t = n // batch
    nseq = min(HG_SEQS, batch)
    assert t % tc == 0 and tc % HG_SUB == 0 and batch % nseq == 0
    ng = WIDTH // HG_GROUP
    seq = lambda a: a.reshape(batch, t, WIDTH)
    tok = pl.BlockSpec((nseq, tc, WIDTH), lambda b, c: (b, c, 0))
    st_spec = pl.BlockSpec((nseq, HEAD_D, WIDTH), lambda b, c: (b, 0, 0))
    gm = _group_ones(HG_GROUP, HEAD_D)
    kern = functools.partial(_hgrn_kernel, tc=tc, nseq=nseq)
    o, st = pl.pallas_call(
        kern,
        grid=(batch // nseq, t // tc),
        in_specs=[tok] * 5 + [st_spec, _const_spec((1, WIDTH)), _const_spec((HG_GROUP, HG_GROUP)),
                              _const_spec((HG_GROUP, HG_GROUP))],
        out_specs=[tok, st_spec],
        out_shape=[jax.ShapeDtypeStruct((batch, t, WIDTH), F32), jax.ShapeDtypeStruct((batch, HEAD_D, WIDTH), F32)],
        scratch_shapes=[pltpu.VMEM((nseq * ng, HG_GROUP, HG_GROUP), F32),
                        pltpu.VMEM((nseq, tc, WIDTH), F32), pltpu.VMEM((nseq, tc, WIDTH), F32)],
        compiler_params=_cparams("parallel", "arbitrary"),
        name="hgrn",
    )(seq(q), seq(k), seq(v), seq(logf), seq(g), s0t, gn, jnp.asarray(gm, BF16), jnp.asarray(gm, F32))
    return o.reshape(n, WIDTH), st


NEG = -1e30


def _bucket_table(max_dist):
    max_exact = NUM_BUCKETS // 2
    d = np.arange(max_dist)
    nf = np.maximum(d, 1).astype(np.float32)
    large = max_exact + (np.log(nf / max_exact) / math.log(MAX_DISTANCE / max_exact)
                         * (NUM_BUCKETS - max_exact)).astype(np.int32)
    bucket = np.where(d < max_exact, d, np.minimum(large, NUM_BUCKETS - 1))
    assert np.all(np.diff(bucket) >= 0) and bucket[-1] == NUM_BUCKETS - 1
    first = [int(np.argmax(bucket >= kk)) for kk in range(NUM_BUCKETS)]
    return bucket, first


_BUCKET, _BUCKET_FIRST = _bucket_table(2 * MAX_DISTANCE)


def _bias_of_distance(d, rb_ref, h):
    acc = jnp.full(d.shape, rb_ref[0, h], F32)
    for kk in range(1, NUM_BUCKETS):
        acc = jnp.where(d >= _BUCKET_FIRST[kk], rb_ref[kk, h], acc)
    return acc


def _far_block_table(nb):
    big = 4 * MB_BLOCK
    tab = np.zeros((nb, 5), np.int32)
    for delta in range(nb):
        lo, hi = max(delta * MB_BLOCK - (MB_BLOCK - 1), 0), delta * MB_BLOCK + (MB_BLOCK - 1)
        b0 = int(_BUCKET[lo])
        ks = [kk for kk in range(b0 + 1, NUM_BUCKETS) if _BUCKET_FIRST[kk] <= hi]
        assert delta < 2 or len(ks) <= 2
        cs = [_BUCKET_FIRST[kk] - delta * MB_BLOCK for kk in ks[:2]] + [big, big]
        tab[delta] = [b0, min(b0 + 1, NUM_BUCKETS - 1), min(b0 + 2, NUM_BUCKETS - 1), cs[0], cs[1]]
    return tab


def _top_blocks(gate, n_valid, n_top, axis=0):
    nb = gate.shape[axis]
    j = lax.broadcasted_iota(jnp.int32, gate.shape, axis)
    g = jnp.where(j < n_valid, gate, -jnp.inf)
    sel = jnp.zeros(gate.shape, F32)
    idxs, oks = [], []
    for _ in range(n_top):
        mx = jnp.max(g, axis=axis, keepdims=True)
        idx = jnp.min(jnp.where(g == mx, j, nb), axis=axis, keepdims=True)
        ok = jnp.where(mx > -jnp.inf, 1.0, 0.0)
        pick = jnp.where(j == idx, ok, 0.0) > 0.0
        sel = jnp.where(pick, 1.0, sel)
        g = jnp.where(pick, -jnp.inf, g)
        idxs.append(idx)
        oks.append(ok)
    return idxs, oks, sel


def _pair_masks(shape):
    lane = lax.broadcasted_iota(jnp.int32, shape, len(shape) - 1)
    return [lane < HEAD_D, lane >= HEAD_D]


def _moba_prompt_kernel(qi_ref, kj_ref, tab_ref, q_ref, k_ref, vt_ref, ksum_ref, rb_ref, o_ref,
                        qm_scr, sel_scr, m_scr, l_scr, acc_scr, near_scr):
    p = pl.program_id(1)
    qi = qi_ref[p]
    kj = kj_ref[p]
    delta = qi - kj
    blk = MB_BLOCK
    dts = (lax.broadcasted_iota(jnp.int32, (blk, blk), 1)
           - lax.broadcasted_iota(jnp.int32, (blk, blk), 0))

    @pl.when(p == 0)
    def _():
        for h in range(H_MB):
            near_scr[h] = jnp.where(dts >= 0, _bias_of_distance(dts, rb_ref, h), NEG)
            near_scr[H_MB + h] = _bias_of_distance(dts + blk, rb_ref, h)

    @pl.when(kj == 0)
    def _():
        q = q_ref[...]
        kmean = ksum_ref[0] * (1.0 / blk)
        m_scr[...] = jnp.full(m_scr.shape, NEG, F32)
        l_scr[...] = jnp.zeros(l_scr.shape, F32)
        acc_scr[...] = jnp.zeros(acc_scr.shape, F32)
        masks = _pair_masks((blk, LANES))
        for h in range(H_MB):
            sl = slice((h // 2) * LANES, (h // 2 + 1) * LANES)
            qh = jnp.where(masks[h % 2], q[:, sl], 0.0)
            qm_scr[h] = (qh * (HEAD_D ** -0.5)).astype(BF16)
            gate = lax.dot_general(kmean[:, sl], qh, (((1,), (1,)), ((), ())),
                                   precision=lax.Precision.HIGHEST, preferred_element_type=F32)
            _, _, sel = _top_blocks(gate, qi, MB_TOPK)
            sel_scr[h] = jnp.where(sel > 0.0, 0.0, NEG)

    def attend(bias_of):
        for h in range(H_MB):
            sl = slice((h // 2) * LANES, (h // 2 + 1) * LANES)
            hs = slice(h * HEAD_D, (h + 1) * HEAD_D)
            vth = vt_ref[0, hs, :].astype(BF16)
            s_all = lax.dot_general(k_ref[:, sl], qm_scr[h], (((1,), (1,)), ((), ())),
                                    preferred_element_type=F32)
            for qh in range(blk // LANES):
                ql = slice(qh * LANES, (qh + 1) * LANES)
                s = s_all[:, ql] + bias_of(h, qh)
                m_old = m_scr[h, :, ql]
                m_new = jnp.maximum(m_old, jnp.max(s, axis=0, keepdims=True))
                alpha = jnp.exp(m_old - m_new)
                pexp = jnp.exp(s - m_new)
                l_scr[h, :, ql] = alpha * l_scr[h, :, ql] + jnp.sum(pexp, axis=0, keepdims=True)
                pv = jnp.dot(vth, pexp.astype(BF16), preferred_element_type=F32)
                acc_scr[hs, ql] = alpha * acc_scr[hs, ql] + pv
                m_scr[h, :, ql] = m_new

    def selrow(h, qh):
        return sel_scr[h, pl.ds(kj, 1), :][:, qh * LANES:(qh + 1) * LANES]

    has_steps = tab_ref[delta, 3] < 2 * blk

    @pl.when(delta == 0)
    def _():
        attend(lambda h, qh: near_scr[h, :, qh * LANES:(qh + 1) * LANES])

    @pl.when(delta == 1)
    def _():
        attend(lambda h, qh: near_scr[H_MB + h, :, qh * LANES:(qh + 1) * LANES] + selrow(h, qh))

    @pl.when((delta >= 2) & jnp.logical_not(has_steps))
    def _():
        attend(lambda h, qh: rb_ref[tab_ref[delta, 0], h] + selrow(h, qh))

    @pl.when((delta >= 2) & has_steps)
    def _():
        dts0 = dts[:, :LANES]

        def bias_of(h, qh):
            sr = selrow(h, qh)
            r0 = rb_ref[tab_ref[delta, 0], h] + sr
            r1 = rb_ref[tab_ref[delta, 1], h] + sr
            r2 = rb_ref[tab_ref[delta, 2], h] + sr
            m1 = dts0 >= tab_ref[delta, 3] - qh * LANES
            m2 = dts0 >= tab_ref[delta, 4] - qh * LANES
            return jnp.where(m2, r2, jnp.where(m1, r1, r0))

        attend(bias_of)

    @pl.when(delta == 0)
    def _():
        for h in range(H_MB):
            hs = slice(h * HEAD_D, (h + 1) * HEAD_D)
            acc_scr[hs, :] = acc_scr[hs, :] / l_scr[h]
        o_ref[...] = acc_scr[...].T


def _moba_prompt(q, k, vt, ksum, rel_bias, batch):
    n = q.shape[0]
    t = n // batch
    assert t % MB_BLOCK == 0
    nb = t // MB_BLOCK
    qi = np.array([i for i in range(nb) for _ in range(i + 1)], np.int32)
    kj = np.array([j for i in range(nb) for j in range(i + 1)], np.int32)
    tab = _far_block_table(nb)
    blk = MB_BLOCK
    grid_spec = pltpu.PrefetchScalarGridSpec(
        num_scalar_prefetch=3,
        grid=(batch, len(qi)),
        in_specs=[
            pl.BlockSpec((blk, WIDTH), lambda b, p, qi, kj, tab: (b * nb + qi[p], 0)),
            pl.BlockSpec((blk, WIDTH), lambda b, p, qi, kj, tab: (b * nb + kj[p], 0)),
            pl.BlockSpec((1, WIDTH, blk), lambda b, p, qi, kj, tab: (b, 0, kj[p])),
            pl.BlockSpec((1, nb, WIDTH), lambda b, p, qi, kj, tab: (b, 0, 0)),
            pl.BlockSpec(memory_space=pltpu.SMEM),
        ],
        out_specs=pl.BlockSpec((blk, WIDTH), lambda b, p, qi, kj, tab: (b * nb + qi[p], 0)),
        scratch_shapes=[pltpu.VMEM((H_MB, blk, LANES), BF16), pltpu.VMEM((H_MB, nb, blk), F32),
                        pltpu.VMEM((H_MB, 1, blk), F32), pltpu.VMEM((H_MB, 1, blk), F32),
                        pltpu.VMEM((WIDTH, blk), F32), pltpu.VMEM((2 * H_MB, blk, blk), F32)],
    )
    return pl.pallas_call(
        _moba_prompt_kernel,
        grid_spec=grid_spec,
        out_shape=jax.ShapeDtypeStruct((n, WIDTH), F32),
        compiler_params=_cparams("parallel", "arbitrary"),
        name="moba_prompt",
    )(jnp.asarray(qi), jnp.asarray(kj), jnp.asarray(tab), q, k, vt, ksum, rel_bias)


SAMPLE_PAGES_PER_STEP = 16


def _page_ksum_kernel(pt_ref, *refs, ppb):
    pages, out_ref = refs[:-1], refs[-1]
    j = pl.program_id(1)
    nblk = out_ref.shape[2]
    lane = lax.broadcasted_iota(jnp.int32, (WIDTH, nblk), 1)

    @pl.when(j == 0)
    def _():
        out_ref[...] = jnp.zeros(out_ref.shape, F32)

    acc = out_ref[0]
    for i in range(0, len(pages), ppb):
        blk = pages[i][0].reshape(WIDTH, pages[i].shape[-1])
        for e in range(1, ppb):
            blk = blk + pages[i + e][0].reshape(blk.shape)
        col = jnp.sum(blk, axis=1, keepdims=True)
        acc = jnp.where(lane == j * (len(pages) // ppb) + i // ppb, col, acc)
    out_ref[0] = acc


def _page_ksum(pool_t, page_table):
    db, n_pages = page_table.shape
    page = pool_t.shape[-1]
    ppb = MB_BLOCK // page
    pps = SAMPLE_PAGES_PER_STEP
    assert MB_BLOCK % page == 0 and n_pages % pps == 0 and pps % ppb == 0
    nblk = n_pages // ppb
    specs = [pl.BlockSpec((1,) + pool_t.shape[1:], (lambda b, j, pt, i=i: (pt[b, j * pps + i], 0, 0, 0)))
             for i in range(pps)]
    grid_spec = pltpu.PrefetchScalarGridSpec(
        num_scalar_prefetch=1, grid=(db, n_pages // pps), in_specs=specs,
        out_specs=pl.BlockSpec((1, WIDTH, nblk), lambda b, j, pt: (b, 0, 0)))
    return pl.pallas_call(
        functools.partial(_page_ksum_kernel, ppb=ppb), grid_spec=grid_spec,
        out_shape=jax.ShapeDtypeStruct((db, WIDTH, nblk), F32),
        compiler_params=_cparams("parallel", "arbitrary"), name="page_ksum",
    )(page_table, *([pool_t] * pps))


def _sample_select_kernel(q_ref, ksum_ref, idx_ref):
    q = q_ref[...]
    kmean_t = ksum_ref[0] * (1.0 / MB_BLOCK)
    nblk = kmean_t.shape[1]
    cols = []
    for h in range(H_MB):
        hs = slice(h * HEAD_D, (h + 1) * HEAD_D)
        gate = jnp.dot(q[:, hs], kmean_t[hs, :], precision=lax.Precision.HIGHEST,
                       preferred_element_type=F32)
        idxs, _, _ = _top_blocks(gate, nblk, MB_TOPK, axis=1)
        cols.extend(idxs)
    idx_ref[0] = jnp.concatenate(cols, axis=1)


def _sample_select(q, ksum_t, ds):
    db, _, nblk = ksum_t.shape
    assert nblk >= MB_TOPK
    return pl.pallas_call(
        _sample_select_kernel, grid=(db,),
        in_specs=[pl.BlockSpec((ds, WIDTH), lambda b: (b, 0)), pl.BlockSpec((1, WIDTH, nblk), lambda b: (b, 0, 0))],
        out_specs=pl.BlockSpec((1, ds, H_MB * MB_TOPK), lambda b: (b, 0, 0)),
        out_shape=jax.ShapeDtypeStruct((db, ds, H_MB * MB_TOPK), jnp.int32),
        compiler_params=_cparams("parallel"), name="sample_select",
    )(q, ksum_t)


def _sample_attend_kernel(pt_ref, sel_ref, q_ref, kn_ref, vn_ref, rb_ref, kpool_ref, vpool_ref, o_ref,
                          kbuf, vbuf, sems, *, ds, page, past_len):
    ppb = MB_BLOCK // page
    ntile = MB_TOPK * ppb
    step = pl.program_id(0)
    nstep = pl.num_programs(0)

    def copies(st, slot):
        b, h = st // H_MB, st % H_MB
        out = []
        for q in range(ds):
            for r in range(MB_TOPK):
                blk = sel_ref[((b * ds + q) * H_MB + h) * MB_TOPK + r]
                for e in range(ppb):
                    pg = pt_ref[b, blk * ppb + e]
                    dst = pl.ds((r * ppb + e) * page, page)
                    out.append(pltpu.make_async_copy(kpool_ref.at[pg, h], kbuf.at[slot, q, :, dst], sems.at[0, slot]))
                    out.append(pltpu.make_async_copy(vpool_ref.at[pg, h], vbuf.at[slot, q, :, dst], sems.at[1, slot]))
        return out

    @pl.when(step == 0)
    def _():
        for cp in copies(step, 0):
            cp.start()

    @pl.when(step + 1 < nstep)
    def _():
        for cp in copies(step + 1, (step + 1) % 2):
            cp.start()

    slot = step % 2
    for qq in range(ds):
        for tile in range(ntile):
            dst = pl.ds(tile * page, page)
            pltpu.make_async_copy(kpool_ref.at[0, 0], kbuf.at[slot, qq, :, dst], sems.at[0, slot]).wait()
            pltpu.make_async_copy(vpool_ref.at[0, 0], vbuf.at[slot, qq, :, dst], sems.at[1, slot]).wait()

    b, h = step // H_MB, step % H_MB
    hi = lax.Precision.HIGHEST
    q = q_ref[0] * (HEAD_D ** -0.5)
    rowi = lax.broadcasted_iota(jnp.int32, (ds, ntile * page), 0)
    s_sel = jnp.zeros((ds, ntile * page), F32)
    kpos = jnp.zeros((ds, ntile * page), jnp.int32)
    lane = lax.broadcasted_iota(jnp.int32, (ds, page), 1)
    for qq in range(ds):
        res = jnp.dot(q, kbuf[slot, qq], precision=hi, preferred_element_type=F32)
        s_sel = jnp.where(rowi == qq, res, s_sel)
        pieces = []
        for r in range(MB_TOPK):
            blk = sel_ref[((b * ds + qq) * H_MB + h) * MB_TOPK + r]
            for e in range(ppb):
                pieces.append(blk * MB_BLOCK + e * page + lane)
        kpos = jnp.where(rowi == qq, jnp.concatenate(pieces, axis=1), kpos)
    q_pos = past_len + lax.broadcasted_iota(jnp.int32, (ds, 1), 0)
    s_sel = s_sel + _bias_of_distance(q_pos - kpos, rb_ref, h)
    own = lax.broadcasted_iota(jnp.int32, (ds, ds), 1)
    qrow = lax.broadcasted_iota(jnp.int32, (ds, ds), 0)
    s_own = jnp.dot(q, kn_ref[0], precision=hi, preferred_element_type=F32) \
        + _bias_of_distance(qrow - own, rb_ref, h)
    s_own = jnp.where(own <= qrow, s_own, NEG)
    m = jnp.maximum(jnp.max(s_sel, axis=1, keepdims=True), jnp.max(s_own, axis=1, keepdims=True))
    p_sel = jnp.exp(s_sel - m)
    p_own = jnp.exp(s_own - m)
    den = jnp.sum(p_sel, axis=1, keepdims=True) + jnp.sum(p_own, axis=1, keepdims=True)
    nt_dims = (((1,), (1,)), ((), ()))
    o = lax.dot_general(p_own, vn_ref[0], nt_dims, precision=hi, preferred_element_type=F32)
    rowo = lax.broadcasted_iota(jnp.int32, (ds, HEAD_D), 0)
    for qq in range(ds):
        res = lax.dot_general(p_sel, vbuf[slot, qq], nt_dims, precision=hi, preferred_element_type=F32)
        o = o + jnp.where(rowo == qq, res, 0.0)
    o_ref[0] = o / den


def _sample_attend(q, kt_new, vt_new, pool_kt, pool_vt, page_table, sel_flat, rel_bias, ds):
    db, n_pages = page_table.shape
    page = pool_kt.shape[-1]
    ppb = MB_BLOCK // page
    past_len = n_pages * page
    assert past_len % MB_BLOCK == 0
    ntile = MB_TOPK * ppb
    qh = q.reshape(db, ds, H_MB, HEAD_D).transpose(0, 2, 1, 3).reshape(db * H_MB, ds, HEAD_D)
    per_head = lambda shape: pl.BlockSpec((1,) + shape, lambda s, pt, sel: (s, 0, 0))
    grid_spec = pltpu.PrefetchScalarGridSpec(
        num_scalar_prefetch=2, grid=(db * H_MB,),
        in_specs=[per_head((ds, HEAD_D)), per_head((HEAD_D, ds)), per_head((HEAD_D, ds)),
                  pl.BlockSpec(memory_space=pltpu.SMEM), pl.BlockSpec(memory_space=pl.ANY),
                  pl.BlockSpec(memory_space=pl.ANY)],
        out_specs=per_head((ds, HEAD_D)),
        scratch_shapes=[pltpu.VMEM((2, ds, HEAD_D, ntile * page), F32), pltpu.VMEM((2, ds, HEAD_D, ntile * page), F32),
                        pltpu.SemaphoreType.DMA((2, 2))])
    kern = functools.partial(_sample_attend_kernel, ds=ds, page=page, past_len=past_len)
    return pl.pallas_call(
        kern, grid_spec=grid_spec, out_shape=jax.ShapeDtypeStruct((db * H_MB, ds, HEAD_D), F32),
        compiler_params=_cparams("arbitrary"), name="sample_attend",
    )(page_table, sel_flat, qh, kt_new, vt_new, rel_bias, pool_kt, pool_vt)


def _memkv_kernel(m_ref, g_ref, w_ref, kn_ref, ones_ref, k_ref, v_ref):
    xb = _rmsnorm_rows(m_ref[...], g_ref[...]).astype(BF16)
    wk = w_ref.shape[1] // 2
    k = jnp.dot(xb, w_ref[:, :wk], preferred_element_type=F32)
    k_ref[...] = _seg_rmsnorm(k, kn_ref[...], ones_ref[...], MEM_HD)
    v_ref[...] = jnp.dot(xb, w_ref[:, wk:], preferred_element_type=F32)


def _memkv(mem, g, w_bf16, kn, tm):
    n, d = mem.shape
    wk = w_bf16.shape[1] // 2
    out = pl.BlockSpec((tm, wk), lambda i: (i, 0))
    return pl.pallas_call(
        _memkv_kernel, grid=(n // tm,),
        in_specs=[pl.BlockSpec((tm, d), lambda i: (i, 0)), _const_spec((1, d)), _const_spec(w_bf16.shape),
                  _const_spec((1, wk)), _const_spec((wk, wk))],
        out_specs=[out, out], out_shape=[jax.ShapeDtypeStruct((n, wk), F32)] * 2,
        compiler_params=_cparams("parallel"), name="mem_kv",
    )(mem, g, w_bf16, kn, jnp.asarray(_group_ones(wk, MEM_HD), BF16))


def _mixmem_kernel(x_ref, hg_ref, omb_ref, wout_ref, gmem_ref, wq_ref, qn_ref, ones_ref, mk_ref, mv_ref, wo_ref,
                   h_ref):
    mix = (jnp.dot(hg_ref[...].astype(BF16), wout_ref[:WIDTH, :], preferred_element_type=F32)
           + jnp.dot(omb_ref[...].astype(BF16), wout_ref[WIDTH:, :], preferred_element_type=F32))
    h1 = x_ref[...] + mix
    hn = _rmsnorm_rows(h1, gmem_ref[...]).astype(BF16)
    q = _seg_rmsnorm(jnp.dot(hn, wq_ref[...], preferred_element_type=F32), qn_ref[...], ones_ref[...], MEM_HD)
    qb = q.astype(BF16)
    mk = mk_ref[0].astype(BF16)
    mv = mv_ref[0].astype(BF16)
    outs = []
    for h in range(H_MEM):
        sl = slice(h * MEM_HD, (h + 1) * MEM_HD)
        s = lax.dot_general(qb[:, sl], mk[:, sl], (((1,), (1,)), ((), ())),
                            preferred_element_type=F32) * (MEM_HD ** -0.5)
        e = jnp.exp(s - jnp.max(s, axis=-1, keepdims=True))
        p = e / jnp.sum(e, axis=-1, keepdims=True)
        outs.append(jnp.dot(p.astype(BF16), mv[:, sl], preferred_element_type=F32))
    o = jnp.concatenate(outs, axis=1).astype(BF16)
    h_ref[...] = h1 + jnp.dot(o, wo_ref[...], preferred_element_type=F32)


def _mixmem(x, hg, omb, wout, gmem, wq, qn, mk, mv, wo, batch, tm):
    n, d = x.shape
    t = n // batch
    assert t % tm == 0
    nt = t // tm
    n_mem, wm = mk.shape[1:]
    tok = lambda w: pl.BlockSpec((tm, w), lambda b, i: (b * nt + i, 0))
    mem = pl.BlockSpec((1, n_mem, wm), lambda b, i: (b, 0, 0))
    return pl.pallas_call(
        _mixmem_kernel, grid=(batch, nt),
        in_specs=[tok(d), tok(WIDTH), tok(WIDTH), _const_spec(wout.shape), _const_spec((1, d)),
                  _const_spec(wq.shape), _const_spec((1, wm)), _const_spec((wm, wm)), mem, mem,
                  _const_spec(wo.shape)],
        out_specs=tok(d), out_shape=jax.ShapeDtypeStruct((n, d), F32),
        compiler_params=_cparams("parallel", "parallel"), name="mix_mem",
    )(x, hg, omb, wout, gmem, wq, qn, jnp.asarray(_group_ones(wm, MEM_HD), BF16), mk, mv, wo)


FFN_CHUNKS = 2


def _ffn_kernel(h_ref, g_ref, wup_ref, cw_ref, cb_ref, wdn_ref, prev_ref, y_ref, cst_ref, carry_scr, *, dff):
    c = pl.program_id(1)

    @pl.when(c == 0)
    def _():
        carry_scr[...] = prev_ref[0]

    h = h_ref[...]
    hn = _rmsnorm_rows(h, g_ref[...]).astype(BF16)
    tm = h.shape[0]
    fc = dff // FFN_CHUNKS
    row = lax.broadcasted_iota(jnp.int32, (tm, fc), 0)
    acc = h
    for ci in range(FFN_CHUNKS):
        sl = slice(ci * fc, (ci + 1) * fc)
        u = jnp.dot(hn, wup_ref[:, sl], preferred_element_type=F32)
        v = jnp.dot(hn, wup_ref[:, dff + ci * fc:dff + (ci + 1) * fc], preferred_element_type=F32)
        prev = carry_scr[:, sl]
        u1 = jnp.where(row == 0, prev[1:2], pltpu.roll(u, 1, 0))
        u2 = jnp.where(row == 0, prev[0:1], jnp.where(row == 1, prev[1:2], pltpu.roll(u, 2, 0)))
        cw = cw_ref[:, sl]
        conv = cb_ref[:, sl] + u2 * cw[0:1] + u1 * cw[1:2] + u * cw[2:3]
        act = 0.5 * conv * (1.0 + lax.erf(conv * (2.0 ** -0.5))) * v
        acc = acc + jnp.dot(act.astype(BF16), wdn_ref[sl, :], preferred_element_type=F32)
        last = u[tm - (CONV_W - 1):tm]
        carry_scr[:, sl] = last
        cst_ref[0, :, sl] = last
    y_ref[...] = acc


def _ffn(h, g, wup, cw, cb, wdn, prev, batch, tm):
    n, d = h.shape
    t = n // batch
    assert t % tm == 0 and tm >= CONV_W - 1
    nt = t // tm
    dff = wdn.shape[0]
    assert dff % (FFN_CHUNKS * LANES) == 0
    tok = pl.BlockSpec((tm, d), lambda b, i: (b * nt + i, 0))
    st = pl.BlockSpec((1, CONV_W - 1, dff), lambda b, i: (b, 0, 0))
    return pl.pallas_call(
        functools.partial(_ffn_kernel, dff=dff), grid=(batch, nt),
        in_specs=[tok, _const_spec((1, d)), _const_spec(wup.shape), _const_spec(cw.shape), _const_spec((1, dff)),
                  _const_spec(wdn.shape), st],
        out_specs=[tok, st],
        out_shape=[jax.ShapeDtypeStruct((n, d), F32), jax.ShapeDtypeStruct((batch, CONV_W - 1, dff), F32)],
        scratch_shapes=[pltpu.VMEM((CONV_W - 1, dff), F32)],
        compiler_params=_cparams("parallel", "arbitrary"), name="conv_ffn",
    )(h, g, wup, cw, cb, wdn, prev)


PROMPT_TILE = 256


def _state_to_rows(s):
    b = s.shape[0]
    return s.transpose(0, 3, 1, 2).reshape(b, HEAD_D, WIDTH)


def _rows_to_state(st):
    b = st.shape[0]
    return st.reshape(b, HEAD_D, H_HG, HEAD_D).transpose(0, 2, 3, 1)


def kernel(x_prompt, x_sample, cache_k, cache_v, page_table, state_hgrn, state_conv, cache_mem_k, cache_mem_v, mem_prompt, norm_mix, w_in, hg_lb_logits, hg_out_norm, mb_q_norm, mb_k_norm, rel_bias, w_out, norm_mem, norm_mem_src, w_mem_q, w_mem_kv, mem_q_norm, mem_k_norm, w_mem_o, norm_ffn, w_up, conv_w, conv_b, w_down):
    assert norm_mix.shape[0] == 1, "one layer"
    l = 0
    b, t, dm = x_prompt.shape
    db, ds, _ = x_sample.shape
    n_mem = mem_prompt.shape[1]
    dff = w_down.shape[1]
    row = lambda a: a[None]
    w_in_b, w_out_b = w_in[l].astype(BF16), w_out[l].astype(BF16)
    w_q_b, w_kv_b, w_o_b = w_mem_q[l].astype(BF16), w_mem_kv[l].astype(BF16), w_mem_o[l].astype(BF16)
    w_up_b, w_dn_b = w_up[l].astype(BF16), w_down[l].astype(BF16)
    qn, kn = row(jnp.tile(mb_q_norm[l], H_MB)), row(jnp.tile(mb_k_norm[l], H_MB))
    gn = row(jnp.tile(hg_out_norm[l], H_HG))
    mqn, mkn = row(jnp.tile(mem_q_norm[l], H_MEM)), row(jnp.tile(mem_k_norm[l], H_MEM))

    def layer(x, batch, tile, s0t, moba, mk, mv, prev):
        n = x.shape[0]
        tm = min(PROMPT_TILE, n)
        slabs = batch if (n // batch) % tm == 0 else 1
        qhg, khg, ihg, logf, g, qmb, kmb, ksum, kt, vt = _inproj(
            x, row(norm_mix[l]), w_in_b, hg_lb_logits, qn, kn, slabs, tm)
        hg, st = _hgrn(qhg, khg, ihg, logf, g, s0t, gn, batch, min(tile, HG_TILE))
        omb = moba(qmb, kmb, ksum, kt, vt)
        h = _mixmem(x, hg, omb, w_out_b, row(norm_mem[l]), w_q_b, mqn, mk, mv, w_o_b, batch, tile)
        y, cst = _ffn(h, row(norm_ffn[l]), w_up_b, conv_w[l], row(conv_b[l]), w_dn_b, prev, batch, tile)
        return y, kt, vt, _rows_to_state(st), cst

    mk_p, mv_p = _memkv(mem_prompt.reshape(b * n_mem, dm), row(norm_mem_src[l]), w_kv_b, mkn, n_mem)
    moba_p = lambda q, k, ksum, kt, vt: _moba_prompt(q, k, vt, ksum.reshape(b, t // MB_BLOCK, WIDTH), rel_bias, b)
    y_p, kt_p, vt_p, s_p, c_p = layer(
        x_prompt.reshape(b * t, dm), b, PROMPT_TILE, jnp.zeros((b, HEAD_D, WIDTH), F32), moba_p,
        mk_p.reshape(b, n_mem, -1), mv_p.reshape(b, n_mem, -1), jnp.zeros((b, CONV_W - 1, dff), F32))

    pool_kt = cache_k[l].transpose(0, 2, 3, 1)
    pool_vt = cache_v[l].transpose(0, 2, 3, 1)
    per_seq = lambda a: a.reshape(WIDTH, db, ds).transpose(1, 0, 2)

    def moba_s(q, k, ksum, kt, vt):
        sel = _sample_select(q, _page_ksum(pool_kt, page_table), ds)
        o = _sample_attend(q, per_seq(kt).reshape(db * H_MB, HEAD_D, ds), per_seq(vt).reshape(db * H_MB, HEAD_D, ds),
                           pool_kt, pool_vt, page_table, sel.reshape(-1), rel_bias, ds)
        return o.reshape(db, H_MB, ds, HEAD_D).transpose(0, 2, 1, 3).reshape(db * ds, WIDTH)

    y_s, kt_s, vt_s, s_s, c_s = layer(
        x_sample.reshape(db * ds, dm), db, ds, _state_to_rows(state_hgrn[l]), moba_s,
        cache_mem_k[l].reshape(db, n_mem, -1), cache_mem_v[l].reshape(db, n_mem, -1), state_conv[l])

    hd = lambda a, bb, tt: a.reshape(bb, H_MB, HEAD_D, tt).transpose(0, 3, 1, 2)[None]
    return (y_p.reshape(b, t, dm), y_s.reshape(db, ds, dm),
            hd(kt_p, b, t), hd(vt_p, b, t), hd(per_seq(kt_s), db, ds), hd(per_seq(vt_s), db, ds),
            s_p[None], s_s[None], c_p[None], c_s[None],
            mk_p.reshape(1, b, n_mem, H_MEM, MEM_HD), mv_p.reshape(1, b, n_mem, H_MEM, MEM_HD))
```

```python
import functools
import math

import numpy as np
import jax
import jax.numpy as jnp
from jax import lax
from jax.experimental import pallas as pl
from jax.experimental.pallas import tpu as pltpu

F32 = jnp.float32
BF16 = jnp.bfloat16
EPS = 1e-6

H_HG = 8
H_MB = 8
HEAD_D = 64
WIDTH = 512
MB_BLOCK = 256
MB_TOPK = 3
NUM_BUCKETS = 32
MAX_DISTANCE = 8192
H_MEM = 4
MEM_HD = 128
CONV_W = 3

LANES = 128
SUBLANES = 8
VMEM_LIMIT = 56 * 1024 * 1024

HG_SUB = 8
HG_GROUP = 128
HG_SEQS = 4
HG_TILE = 128


def _cparams(*sem):
    return pltpu.CompilerParams(dimension_semantics=sem, vmem_limit_bytes=VMEM_LIMIT)


def _const_spec(shape):
    nd = len(shape)
    return pl.BlockSpec(shape, lambda *_: (0,) * nd, pipeline_mode=pl.Buffered(1))


def _group_ones(width, group):
    i = np.arange(width) // group
    return (i[:, None] == i[None, :]).astype(np.float32)


def _split_dot(x, ones_bf16):
    hi = x.astype(BF16)
    lo = (x - hi.astype(F32)).astype(BF16)
    return (jnp.dot(hi, ones_bf16, preferred_element_type=F32)
            + jnp.dot(lo, ones_bf16, preferred_element_type=F32))


def _seg_rmsnorm(x, gain, ones_bf16, seg):
    ms = _split_dot(x * x, ones_bf16) * (1.0 / seg)
    return x * lax.rsqrt(ms + EPS) * gain


def _rmsnorm_rows(x, gain):
    ms = jnp.mean(x * x, axis=-1, keepdims=True)
    return x * lax.rsqrt(ms + EPS) * gain


def _inproj_kernel(x_ref, gmix_ref, w_ref, wvt_ref, lbl_ref, qn_ref, kn_ref, ones_ref,
                   qhg_ref, khg_ref, ihg_ref, logf_ref, g_ref, qmb_ref, kmb_ref, ksum_ref, kt_ref, vt_ref, vtb_ref):
    xb = _rmsnorm_rows(x_ref[...], gmix_ref[...]).astype(BF16)

    def proj(i):
        return jnp.dot(xb, w_ref[:, i * WIDTH:(i + 1) * WIDTH], preferred_element_type=F32)

    qhg_ref[...] = proj(0)
    f_logit = proj(1)
    logits = lbl_ref[...]
    e = jnp.exp(logits - jnp.max(logits, axis=0, keepdims=True))
    lb = e[0:1] / jnp.sum(e, axis=0, keepdims=True)
    logf_ref[...] = jnp.log(lb + (1.0 - lb) * jax.nn.sigmoid(f_logit))
    khg_ref[...] = (1.0 - lb) * jax.nn.sigmoid(-f_logit)
    ihg_ref[...] = proj(2)
    g_ref[...] = proj(3)
    ones = ones_ref[...]
    qmb_ref[...] = _seg_rmsnorm(proj(4), qn_ref[...], ones, HEAD_D)
    k = _seg_rmsnorm(proj(5), kn_ref[...], ones, HEAD_D)
    kmb_ref[...] = k.astype(BF16)
    for j in range(ksum_ref.shape[0]):
        rows = k.shape[0] // ksum_ref.shape[0]
        ksum_ref[j] = jnp.sum(k[j * rows:(j + 1) * rows], axis=0, keepdims=True)
    kt_ref[0] = k.T
    vt = lax.dot_general(wvt_ref[...], xb, (((1,), (1,)), ((), ())), preferred_element_type=F32)
    vt_ref[0] = vt
    vtb_ref[0] = vt.astype(BF16)


def _inproj(x, gmix, w_bf16, lb_logits, qn, kn, batch, tm):
    n, d = x.shape
    t = n // batch
    assert t % tm == 0
    tpb = t // tm
    grp = min(tm, MB_BLOCK)
    assert tm % grp == 0
    tok = pl.BlockSpec((tm, WIDTH), lambda i: (i, 0))
    tr = pl.BlockSpec((1, WIDTH, tm), lambda i: (i // tpb, 0, i % tpb))
    outs = ([jax.ShapeDtypeStruct((n, WIDTH), F32)] * 6 + [jax.ShapeDtypeStruct((n, WIDTH), BF16)]
            + [jax.ShapeDtypeStruct((n // grp, 1, WIDTH), F32)]
            + [jax.ShapeDtypeStruct((batch, WIDTH, t), F32)] * 2 + [jax.ShapeDtypeStruct((batch, WIDTH, t), BF16)])
    ones = jnp.asarray(_group_ones(WIDTH, HEAD_D), BF16)
    w_main, wvt = w_bf16[:, :6 * WIDTH], w_bf16[:, 6 * WIDTH:].T
    return pl.pallas_call(
        _inproj_kernel,
        grid=(n // tm,),
        in_specs=[pl.BlockSpec((tm, d), lambda i: (i, 0)), _const_spec((1, d)), _const_spec(w_main.shape),
                  _const_spec(wvt.shape), _const_spec(lb_logits.shape), _const_spec((1, WIDTH)),
                  _const_spec((1, WIDTH)), _const_spec((WIDTH, WIDTH))],
        out_specs=[tok] * 7 + [pl.BlockSpec((tm // grp, 1, WIDTH), lambda i: (i, 0, 0)), tr, tr, tr],
        out_shape=outs,
        compiler_params=_cparams("parallel"),
        name="inproj",
    )(x, gmix, w_main, wvt, lb_logits, qn, kn, ones)


def _hgrn_kernel(q_ref, k_ref, v_ref, lf_ref, g_ref, s0_ref, gn_ref, ones_ref, gmask_ref,
                 o_ref, sfin_ref, st_scr, a_scr, o_scr, *, tc, nseq):
    c = pl.program_id(1)
    ng = WIDTH // HG_GROUP
    hpg = HG_GROUP // HEAD_D
    gmask = gmask_ref[...]

    @pl.when(c == 0)
    def _():
        for bi in range(nseq):
            s0 = s0_ref[bi]
            for gi in range(ng):
                blk = s0[:, gi * HG_GROUP:(gi + 1) * HG_GROUP]
                st_scr[bi * ng + gi] = jnp.concatenate([blk] * hpg, axis=0) * gmask

    row = lax.broadcasted_iota(jnp.int32, (tc, WIDTH), 0) % HG_SUB
    for bi in range(nseq):
        a = lf_ref[bi]
        sh = 1
        while sh < HG_SUB:
            a = a + jnp.where(row >= sh, pltpu.roll(a, sh, 0), 0.0)
            sh *= 2
        a_scr[bi] = a

    ones = ones_ref[...]
    trow = lax.broadcasted_iota(jnp.int32, (HG_SUB, WIDTH), 0)

    def step(n, carry):
        off = pl.multiple_of(n * HG_SUB, HG_SUB)
        for bi in range(nseq):
            q = q_ref[bi, pl.ds(off, HG_SUB), :]
            k = k_ref[bi, pl.ds(off, HG_SUB), :]
            v = v_ref[bi, pl.ds(off, HG_SUB), :]
            al = a_scr[bi, pl.ds(off, HG_SUB), :]
            a_end = al[HG_SUB - 1:HG_SUB, :]
            qe = (q * jnp.exp(al)).astype(BF16)
            kd = (k * jnp.exp(a_end - al)).astype(BF16)
            dec = jnp.exp(a_end)
            vb = v.astype(BF16)
            parts = []
            for s in range(HG_SUB):
                a_s = a_scr[bi, pl.ds(off + s, 1), :]
                k_s = k_ref[bi, pl.ds(off + s, 1), :]
                e_s = q * k_s * jnp.exp(al - a_s)
                parts.append(jnp.where(trow >= s, e_s, 0.0))
            ecat = jnp.concatenate(parts, axis=0).astype(BF16)
            outs = []
            for gi in range(ng):
                sl = slice(gi * HG_GROUP, (gi + 1) * HG_GROUP)
                st = st_scr[bi * ng + gi]
                o_g = lax.dot_general(qe[:, sl], st.astype(BF16), (((1,), (1,)), ((), ())),
                                      preferred_element_type=F32)
                ag = jnp.dot(ecat[:, sl], ones, preferred_element_type=F32)
                for s in range(HG_SUB):
                    o_g = o_g + ag[s * HG_SUB:(s + 1) * HG_SUB, :] * v[s:s + 1, sl]
                outs.append(o_g)
                upd = lax.dot_general(vb[:, sl], kd[:, sl], (((0,), (0,)), ((), ())),
                                      preferred_element_type=F32)
                st_scr[bi * ng + gi] = st * dec[:, sl] + upd * gmask
            o_scr[bi, pl.ds(off, HG_SUB), :] = jnp.concatenate(outs, axis=1)
        return carry

    lax.fori_loop(0, tc // HG_SUB, step, 0)

    for bi in range(nseq):
        o = o_scr[bi]
        ms = jnp.concatenate(
            [_split_dot(o[:, gi * HG_GROUP:(gi + 1) * HG_GROUP] ** 2, ones) for gi in range(ng)], axis=1)
        g = g_ref[bi]
        o_ref[bi] = o * lax.rsqrt(ms * (1.0 / HEAD_D) + EPS) * gn_ref[...] * (g * jax.nn.sigmoid(g))

    @pl.when(c == pl.num_programs(1) - 1)
    def _():
        for bi in range(nseq):
            cols = []
            for gi in range(ng):
                st = st_scr[bi * ng + gi]
                acc = st[0:HEAD_D, :]
                for hh in range(1, hpg):
                    acc = acc + st[hh * HEAD_D:(hh + 1) * HEAD_D, :]
                cols.append(acc)
            sfin_ref[bi] = jnp.concatenate(cols, axis=1)


def _hgrn(q, k, v, logf, g, s0t, gn, batch, tc):
    n = q.shape[0]
    t = n // batch
    nseq = min(HG_SEQS, batch)
    assert t % tc == 0 and tc % HG_SUB == 0 and batch % nseq == 0
    ng = WIDTH // HG_GROUP
    seq = lambda a: a.reshape(batch, t, WIDTH)
    tok = pl.BlockSpec((nseq, tc, WIDTH), lambda b, c: (b, c, 0))
    st_spec = pl.BlockSpec((nseq, HEAD_D, WIDTH), lambda b, c: (b, 0, 0))
    gm = _group_ones(HG_GROUP, HEAD_D)
    kern = functools.partial(_hgrn_kernel, tc=tc, nseq=nseq)
    o, st = pl.pallas_call(
        kern,
        grid=(batch // nseq, t // tc),
        in_specs=[tok] * 5 + [st_spec, _const_spec((1, WIDTH)), _const_spec((HG_GROUP, HG_GROUP)),
                              _const_spec((HG_GROUP, HG_GROUP))],
        out_specs=[tok, st_spec],
        out_shape=[jax.ShapeDtypeStruct((batch, t, WIDTH), F32), jax.ShapeDtypeStruct((batch, HEAD_D, WIDTH), F32)],
        scratch_shapes=[pltpu.VMEM((nseq * ng, HG_GROUP, HG_GROUP), F32),
                        pltpu.VMEM((nseq, tc, WIDTH), F32), pltpu.VMEM((nseq, tc, WIDTH), F32)],
        compiler_params=_cparams("parallel", "arbitrary"),
        name="hgrn",
    )(seq(q), seq(k), seq(v), seq(logf), seq(g), s0t, gn, jnp.asarray(gm, BF16), jnp.asarray(gm, F32))
    return o.reshape(n, WIDTH), st


NEG = -1e30


def _bucket_table(max_dist):
    max_exact = NUM_BUCKETS // 2
    d = np.arange(max_dist)
    nf = np.maximum(d, 1).astype(np.float32)
    large = max_exact + (np.log(nf / max_exact) / math.log(MAX_DISTANCE / max_exact)
                         * (NUM_BUCKETS - max_exact)).astype(np.int32)
    bucket = np.where(d < max_exact, d, np.minimum(large, NUM_BUCKETS - 1))
    assert np.all(np.diff(bucket) >= 0) and bucket[-1] == NUM_BUCKETS - 1
    first = [int(np.argmax(bucket >= kk)) for kk in range(NUM_BUCKETS)]
    return bucket, first


_BUCKET, _BUCKET_FIRST = _bucket_table(2 * MAX_DISTANCE)


def _bias_of_distance(d, rb_ref, h):
    acc = jnp.full(d.shape, rb_ref[0, h], F32)
    for kk in range(1, NUM_BUCKETS):
        acc = jnp.where(d >= _BUCKET_FIRST[kk], rb_ref[kk, h], acc)
    return acc


def _far_block_table(nb):
    big = 4 * MB_BLOCK
    tab = np.zeros((nb, 5), np.int32)
    for delta in range(nb):
        lo, hi = max(delta * MB_BLOCK - (MB_BLOCK - 1), 0), delta * MB_BLOCK + (MB_BLOCK - 1)
        b0 = int(_BUCKET[lo])
        ks = [kk for kk in range(b0 + 1, NUM_BUCKETS) if _BUCKET_FIRST[kk] <= hi]
        assert delta < 2 or len(ks) <= 2
        cs = [_BUCKET_FIRST[kk] - delta * MB_BLOCK for kk in ks[:2]] + [big, big]
        tab[delta] = [b0, min(b0 + 1, NUM_BUCKETS - 1), min(b0 + 2, NUM_BUCKETS - 1), cs[0], cs[1]]
    return tab


def _top_blocks(gate, n_valid, n_top, axis=0):
    nb = gate.shape[axis]
    j = lax.broadcasted_iota(jnp.int32, gate.shape, axis)
    g = jnp.where(j < n_valid, gate, -jnp.inf)
    sel = jnp.zeros(gate.shape, F32)
    idxs, oks = [], []
    for _ in range(n_top):
        mx = jnp.max(g, axis=axis, keepdims=True)
        idx = jnp.min(jnp.where(g == mx, j, nb), axis=axis, keepdims=True)
        ok = jnp.where(mx > -jnp.inf, 1.0, 0.0)
        pick = jnp.where(j == idx, ok, 0.0) > 0.0
        sel = jnp.where(pick, 1.0, sel)
        g = jnp.where(pick, -jnp.inf, g)
        idxs.append(idx)
        oks.append(ok)
    return idxs, oks, sel


def _pair_masks(shape):
    lane = lax.broadcasted_iota(jnp.int32, shape, len(shape) - 1)
    return [lane < HEAD_D, lane >= HEAD_D]


def _moba_prompt_kernel(qi_ref, kj_ref, tab_ref, q_ref, k_ref, vt_ref, ksum_ref, rb_ref, o_ref,
                        qm_scr, sel_scr, m_scr, l_scr, acc_scr, near_scr, alpha_scr, p_scr):
    p = pl.program_id(1)
    qi = qi_ref[p]
    kj = kj_ref[p]
    delta = qi - kj
    blk = MB_BLOCK
    dts = (lax.broadcasted_iota(jnp.int32, (blk, blk), 1)
           - lax.broadcasted_iota(jnp.int32, (blk, blk), 0))

    @pl.when(p == 0)
    def _():
        for h in range(H_MB):
            near_scr[h] = jnp.where(dts >= 0, _bias_of_distance(dts, rb_ref, h), NEG)
            near_scr[H_MB + h] = _bias_of_distance(dts + blk, rb_ref, h)

    @pl.when(kj == 0)
    def _():
        q = q_ref[...]
        kmean = ksum_ref[0] * (1.0 / blk)
        m_scr[...] = jnp.full(m_scr.shape, NEG, F32)
        l_scr[...] = jnp.zeros(l_scr.shape, F32)
        acc_scr[...] = jnp.zeros(acc_scr.shape, F32)
        masks = _pair_masks((blk, LANES))
        for h in range(H_MB):
            sl = slice((h // 2) * LANES, (h // 2 + 1) * LANES)
            qh = jnp.where(masks[h % 2], q[:, sl], 0.0)
            qm_scr[h] = (qh * (HEAD_D ** -0.5)).astype(BF16)
            gate = lax.dot_general(kmean[:, sl], qh, (((1,), (1,)), ((), ())),
                                   precision=lax.Precision.HIGHEST, preferred_element_type=F32)
            _, _, sel = _top_blocks(gate, qi, MB_TOPK)
            sel_scr[h] = jnp.where(sel > 0.0, 0.0, NEG)

    def attend(bias_of):
        for h in range(H_MB):
            sl = slice((h // 2) * LANES, (h // 2 + 1) * LANES)
            s_all = lax.dot_general(k_ref[:, sl], qm_scr[h], (((1,), (1,)), ((), ())),
                                    preferred_element_type=F32)
            for qh in range(blk // LANES):
                ql = slice(qh * LANES, (qh + 1) * LANES)
                s = s_all[:, ql] + bias_of(h, qh)
                m_old = m_scr[h, :, ql]
                m_new = jnp.maximum(m_old, jnp.max(s, axis=0, keepdims=True))
                alpha = jnp.exp(m_old - m_new)
                pexp = jnp.exp(s - m_new)
                l_scr[h, :, ql] = alpha * l_scr[h, :, ql] + jnp.sum(pexp, axis=0, keepdims=True)
                m_scr[h, :, ql] = m_new
                alpha_scr[h, :, ql] = alpha
                p_scr[h, :, ql] = pexp.astype(BF16)
        for h in range(H_MB):
            hs = slice(h * HEAD_D, (h + 1) * HEAD_D)
            pv = jnp.dot(vt_ref[0, hs, :], p_scr[h], preferred_element_type=F32)
            acc_scr[hs, :] = alpha_scr[h] * acc_scr[hs, :] + pv

    def selrow(h, qh):
        return sel_scr[h, pl.ds(kj, 1), :][:, qh * LANES:(qh + 1) * LANES]

    has_steps = tab_ref[delta, 3] < 2 * blk

    @pl.when(delta == 0)
    def _():
        attend(lambda h, qh: near_scr[h, :, qh * LANES:(qh + 1) * LANES])

    @pl.when(delta == 1)
    def _():
        attend(lambda h, qh: near_scr[H_MB + h, :, qh * LANES:(qh + 1) * LANES] + selrow(h, qh))

    @pl.when((delta >= 2) & jnp.logical_not(has_steps))
    def _():
        attend(lambda h, qh: rb_ref[tab_ref[delta, 0], h] + selrow(h, qh))

    @pl.when((delta >= 2) & has_steps)
    def _():
        dts0 = dts[:, :LANES]

        def bias_of(h, qh):
            sr = selrow(h, qh)
            r0 = rb_ref[tab_ref[delta, 0], h] + sr
            r1 = rb_ref[tab_ref[delta, 1], h] + sr
            r2 = rb_ref[tab_ref[delta, 2], h] + sr
            m1 = dts0 >= tab_ref[delta, 3] - qh * LANES
            m2 = dts0 >= tab_ref[delta, 4] - qh * LANES
            return jnp.where(m2, r2, jnp.where(m1, r1, r0))

        attend(bias_of)

    @pl.when(delta == 0)
    def _():
        for h in range(H_MB):
            hs = slice(h * HEAD_D, (h + 1) * HEAD_D)
            acc_scr[hs, :] = acc_scr[hs, :] / l_scr[h]
        o_ref[...] = acc_scr[...].T


def _moba_prompt(q, k, vt, ksum, rel_bias, batch):
    n = q.shape[0]
    t = n // batch
    assert t % MB_BLOCK == 0
    nb = t // MB_BLOCK
    qi = np.array([i for i in range(nb) for _ in range(i + 1)], np.int32)
    kj = np.array([j for i in range(nb) for j in range(i + 1)], np.int32)
    tab = _far_block_table(nb)
    blk = MB_BLOCK
    grid_spec = pltpu.PrefetchScalarGridSpec(
        num_scalar_prefetch=3,
        grid=(batch, len(qi)),
        in_specs=[
            pl.BlockSpec((blk, WIDTH), lambda b, p, qi, kj, tab: (b * nb + qi[p], 0)),
            pl.BlockSpec((blk, WIDTH), lambda b, p, qi, kj, tab: (b * nb + kj[p], 0)),
            pl.BlockSpec((1, WIDTH, blk), lambda b, p, qi, kj, tab: (b, 0, kj[p])),
            pl.BlockSpec((1, nb, WIDTH), lambda b, p, qi, kj, tab: (b, 0, 0)),
            pl.BlockSpec(memory_space=pltpu.SMEM),
        ],
        out_specs=pl.BlockSpec((blk, WIDTH), lambda b, p, qi, kj, tab: (b * nb + qi[p], 0)),
        scratch_shapes=[pltpu.VMEM((H_MB, blk, LANES), BF16), pltpu.VMEM((H_MB, nb, blk), F32),
                        pltpu.VMEM((H_MB, 1, blk), F32), pltpu.VMEM((H_MB, 1, blk), F32),
                        pltpu.VMEM((WIDTH, blk), F32), pltpu.VMEM((2 * H_MB, blk, blk), F32),
                        pltpu.VMEM((H_MB, 1, blk), F32), pltpu.VMEM((H_MB, blk, blk), BF16)],
    )
    return pl.pallas_call(
        _moba_prompt_kernel,
        grid_spec=grid_spec,
        out_shape=jax.ShapeDtypeStruct((n, WIDTH), F32),
        compiler_params=_cparams("parallel", "arbitrary"),
        name="moba_prompt",
    )(jnp.asarray(qi), jnp.asarray(kj), jnp.asarray(tab), q, k, vt, ksum, rel_bias)


SAMPLE_PAGES_PER_STEP = 16


def _page_ksum_kernel(pt_ref, *refs, ppb):
    pages, out_ref = refs[:-1], refs[-1]
    j = pl.program_id(1)
    nblk = out_ref.shape[2]
    lane = lax.broadcasted_iota(jnp.int32, (WIDTH, nblk), 1)

    @pl.when(j == 0)
    def _():
        out_ref[...] = jnp.zeros(out_ref.shape, F32)

    acc = out_ref[0]
    for i in range(0, len(pages), ppb):
        blk = pages[i][0].reshape(WIDTH, pages[i].shape[-1])
        for e in range(1, ppb):
            blk = blk + pages[i + e][0].reshape(blk.shape)
        col = jnp.sum(blk, axis=1, keepdims=True)
        acc = jnp.where(lane == j * (len(pages) // ppb) + i // ppb, col, acc)
    out_ref[0] = acc


def _page_ksum(pool_t, page_table):
    db, n_pages = page_table.shape
    page = pool_t.shape[-1]
    ppb = MB_BLOCK // page
    pps = SAMPLE_PAGES_PER_STEP
    assert MB_BLOCK % page == 0 and n_pages % pps == 0 and pps % ppb == 0
    nblk = n_pages // ppb
    specs = [pl.BlockSpec((1,) + pool_t.shape[1:], (lambda b, j, pt, i=i: (pt[b, j * pps + i], 0, 0, 0)))
             for i in range(pps)]
    grid_spec = pltpu.PrefetchScalarGridSpec(
        num_scalar_prefetch=1, grid=(db, n_pages // pps), in_specs=specs,
        out_specs=pl.BlockSpec((1, WIDTH, nblk), lambda b, j, pt: (b, 0, 0)))
    return pl.pallas_call(
        functools.partial(_page_ksum_kernel, ppb=ppb), grid_spec=grid_spec,
        out_shape=jax.ShapeDtypeStruct((db, WIDTH, nblk), F32),
        compiler_params=_cparams("parallel", "arbitrary"), name="page_ksum",
    )(page_table, *([pool_t] * pps))


def _sample_select_kernel(q_ref, ksum_ref, idx_ref):
    q = q_ref[...]
    kmean_t = ksum_ref[0] * (1.0 / MB_BLOCK)
    nblk = kmean_t.shape[1]
    cols = []
    for h in range(H_MB):
        hs = slice(h * HEAD_D, (h + 1) * HEAD_D)
        gate = jnp.dot(q[:, hs], kmean_t[hs, :], precision=lax.Precision.HIGHEST,
                       preferred_element_type=F32)
        idxs, _, _ = _top_blocks(gate, nblk, MB_TOPK, axis=1)
        cols.extend(idxs)
    idx_ref[0] = jnp.concatenate(cols, axis=1)


def _sample_select(q, ksum_t, ds):
    db, _, nblk = ksum_t.shape
    assert nblk >= MB_TOPK
    return pl.pallas_call(
        _sample_select_kernel, grid=(db,),
        in_specs=[pl.BlockSpec((ds, WIDTH), lambda b: (b, 0)), pl.BlockSpec((1, WIDTH, nblk), lambda b: (b, 0, 0))],
        out_specs=pl.BlockSpec((1, ds, H_MB * MB_TOPK), lambda b: (b, 0, 0)),
        out_shape=jax.ShapeDtypeStruct((db, ds, H_MB * MB_TOPK), jnp.int32),
        compiler_params=_cparams("parallel"), name="sample_select",
    )(q, ksum_t)


def _sample_attend_kernel(pt_ref, sel_ref, q_ref, kn_ref, vn_ref, rb_ref, kpool_ref, vpool_ref, o_ref,
                          kbuf, vbuf, sems, *, ds, page, past_len):
    ppb = MB_BLOCK // page
    ntile = MB_TOPK * ppb
    step = pl.program_id(0)
    nstep = pl.num_programs(0)

    def copies(st, slot):
        b, h = st // H_MB, st % H_MB
        out = []
        for q in range(ds):
            for r in range(MB_TOPK):
                blk = sel_ref[((b * ds + q) * H_MB + h) * MB_TOPK + r]
                for e in range(ppb):
                    pg = pt_ref[b, blk * ppb + e]
                    dst = pl.ds((r * ppb + e) * page, page)
                    out.append(pltpu.make_async_copy(kpool_ref.at[pg, h], kbuf.at[slot, q, :, dst], sems.at[0, slot]))
                    out.append(pltpu.make_async_copy(vpool_ref.at[pg, h], vbuf.at[slot, q, :, dst], sems.at[1, slot]))
        return out

    @pl.when(step == 0)
    def _():
        for cp in copies(step, 0):
            cp.start()

    @pl.when(step + 1 < nstep)
    def _():
        for cp in copies(step + 1, (step + 1) % 2):
            cp.start()

    slot = step % 2
    for qq in range(ds):
        for tile in range(ntile):
            dst = pl.ds(tile * page, page)
            pltpu.make_async_copy(kpool_ref.at[0, 0], kbuf.at[slot, qq, :, dst], sems.at[0, slot]).wait()
            pltpu.make_async_copy(vpool_ref.at[0, 0], vbuf.at[slot, qq, :, dst], sems.at[1, slot]).wait()

    b, h = step // H_MB, step % H_MB
    q = (q_ref[0] * (HEAD_D ** -0.5)).astype(BF16)
    rowi = lax.broadcasted_iota(jnp.int32, (ds, ntile * page), 0)
    s_sel = jnp.zeros((ds, ntile * page), F32)
    kpos = jnp.zeros((ds, ntile * page), jnp.int32)
    lane = lax.broadcasted_iota(jnp.int32, (ds, page), 1)
    for qq in range(ds):
        res = jnp.dot(q, kbuf[slot, qq].astype(BF16), preferred_element_type=F32)
        s_sel = jnp.where(rowi == qq, res, s_sel)
        pieces = []
        for r in range(MB_TOPK):
            blk = sel_ref[((b * ds + qq) * H_MB + h) * MB_TOPK + r]
            for e in range(ppb):
                pieces.append(blk * MB_BLOCK + e * page + lane)
        kpos = jnp.where(rowi == qq, jnp.concatenate(pieces, axis=1), kpos)
    q_pos = past_len + lax.broadcasted_iota(jnp.int32, (ds, 1), 0)
    s_sel = s_sel + _bias_of_distance(q_pos - kpos, rb_ref, h)
    own = lax.broadcasted_iota(jnp.int32, (ds, ds), 1)
    qrow = lax.broadcasted_iota(jnp.int32, (ds, ds), 0)
    s_own = jnp.dot(q, kn_ref[0].astype(BF16), preferred_element_type=F32) \
        + _bias_of_distance(qrow - own, rb_ref, h)
    s_own = jnp.where(own <= qrow, s_own, NEG)
    m = jnp.maximum(jnp.max(s_sel, axis=1, keepdims=True), jnp.max(s_own, axis=1, keepdims=True))
    p_sel = jnp.exp(s_sel - m)
    p_own = jnp.exp(s_own - m)
    den = jnp.sum(p_sel, axis=1, keepdims=True) + jnp.sum(p_own, axis=1, keepdims=True)
    nt_dims = (((1,), (1,)), ((), ()))
    o = lax.dot_general(p_own.astype(BF16), vn_ref[0].astype(BF16), nt_dims, preferred_element_type=F32)
    rowo = lax.broadcasted_iota(jnp.int32, (ds, HEAD_D), 0)
    p_sel_b = p_sel.astype(BF16)
    for qq in range(ds):
        res = lax.dot_general(p_sel_b, vbuf[slot, qq].astype(BF16), nt_dims, preferred_element_type=F32)
        o = o + jnp.where(rowo == qq, res, 0.0)
    o_ref[0] = o / den


def _sample_attend(q, kt_new, vt_new, pool_kt, pool_vt, page_table, sel_flat, rel_bias, ds):
    db, n_pages = page_table.shape
    page = pool_kt.shape[-1]
    ppb = MB_BLOCK // page
    past_len = n_pages * page
    assert past_len % MB_BLOCK == 0
    ntile = MB_TOPK * ppb
    qh = q.reshape(db, ds, H_MB, HEAD_D).transpose(0, 2, 1, 3).reshape(db * H_MB, ds, HEAD_D)
    per_head = lambda shape: pl.BlockSpec((1,) + shape, lambda s, pt, sel: (s, 0, 0))
    grid_spec = pltpu.PrefetchScalarGridSpec(
        num_scalar_prefetch=2, grid=(db * H_MB,),
        in_specs=[per_head((ds, HEAD_D)), per_head((HEAD_D, ds)), per_head((HEAD_D, ds)),
                  pl.BlockSpec(memory_space=pltpu.SMEM), pl.BlockSpec(memory_space=pl.ANY),
                  pl.BlockSpec(memory_space=pl.ANY)],
        out_specs=per_head((ds, HEAD_D)),
        scratch_shapes=[pltpu.VMEM((2, ds, HEAD_D, ntile * page), F32), pltpu.VMEM((2, ds, HEAD_D, ntile * page), F32),
                        pltpu.SemaphoreType.DMA((2, 2))])
    kern = functools.partial(_sample_attend_kernel, ds=ds, page=page, past_len=past_len)
    return pl.pallas_call(
        kern, grid_spec=grid_spec, out_shape=jax.ShapeDtypeStruct((db * H_MB, ds, HEAD_D), F32),
        compiler_params=_cparams("arbitrary"), name="sample_attend",
    )(page_table, sel_flat, qh, kt_new, vt_new, rel_bias, pool_kt, pool_vt)


def _memkv_kernel(m_ref, g_ref, w_ref, kn_ref, ones_ref, k_ref, v_ref):
    xb = _rmsnorm_rows(m_ref[...], g_ref[...]).astype(BF16)
    wk = w_ref.shape[1] // 2
    k = jnp.dot(xb, w_ref[:, :wk], preferred_element_type=F32)
    k_ref[...] = _seg_rmsnorm(k, kn_ref[...], ones_ref[...], MEM_HD)
    v_ref[...] = jnp.dot(xb, w_ref[:, wk:], preferred_element_type=F32)


def _memkv(mem, g, w_bf16, kn, tm):
    n, d = mem.shape
    wk = w_bf16.shape[1] // 2
    out = pl.BlockSpec((tm, wk), lambda i: (i, 0))
    return pl.pallas_call(
        _memkv_kernel, grid=(n // tm,),
        in_specs=[pl.BlockSpec((tm, d), lambda i: (i, 0)), _const_spec((1, d)), _const_spec(w_bf16.shape),
                  _const_spec((1, wk)), _const_spec((wk, wk))],
        out_specs=[out, out], out_shape=[jax.ShapeDtypeStruct((n, wk), F32)] * 2,
        compiler_params=_cparams("parallel"), name="mem_kv",
    )(mem, g, w_bf16, kn, jnp.asarray(_group_ones(wk, MEM_HD), BF16))


def _mixmem_kernel(x_ref, hg_ref, omb_ref, wout_ref, gmem_ref, wq_ref, qn_ref, ones_ref, mk_ref, mv_ref, wo_ref,
                   h_ref):
    mix = (jnp.dot(hg_ref[...].astype(BF16), wout_ref[:WIDTH, :], preferred_element_type=F32)
           + jnp.dot(omb_ref[...].astype(BF16), wout_ref[WIDTH:, :], preferred_element_type=F32))
    h1 = x_ref[...] + mix
    hn = _rmsnorm_rows(h1, gmem_ref[...]).astype(BF16)
    q = _seg_rmsnorm(jnp.dot(hn, wq_ref[...], preferred_element_type=F32), qn_ref[...], ones_ref[...], MEM_HD)
    qb = q.astype(BF16)
    mk = mk_ref[0].astype(BF16)
    mv = mv_ref[0].astype(BF16)
    outs = []
    for h in range(H_MEM):
        sl = slice(h * MEM_HD, (h + 1) * MEM_HD)
        s = lax.dot_general(qb[:, sl], mk[:, sl], (((1,), (1,)), ((), ())),
                            preferred_element_type=F32) * (MEM_HD ** -0.5)
        e = jnp.exp(s - jnp.max(s, axis=-1, keepdims=True))
        p = e / jnp.sum(e, axis=-1, keepdims=True)
        outs.append(jnp.dot(p.astype(BF16), mv[:, sl], preferred_element_type=F32))
    o = jnp.concatenate(outs, axis=1).astype(BF16)
    h_ref[...] = h1 + jnp.dot(o, wo_ref[...], preferred_element_type=F32)


def _mixmem(x, hg, omb, wout, gmem, wq, qn, mk, mv, wo, batch, tm):
    n, d = x.shape
    t = n // batch
    assert t % tm == 0
    nt = t // tm
    n_mem, wm = mk.shape[1:]
    tok = lambda w: pl.BlockSpec((tm, w), lambda b, i: (b * nt + i, 0))
    mem = pl.BlockSpec((1, n_mem, wm), lambda b, i: (b, 0, 0))
    return pl.pallas_call(
        _mixmem_kernel, grid=(batch, nt),
        in_specs=[tok(d), tok(WIDTH), tok(WIDTH), _const_spec(wout.shape), _const_spec((1, d)),
                  _const_spec(wq.shape), _const_spec((1, wm)), _const_spec((wm, wm)), mem, mem,
                  _const_spec(wo.shape)],
        out_specs=tok(d), out_shape=jax.ShapeDtypeStruct((n, d), F32),
        compiler_params=_cparams("parallel", "parallel"), name="mix_mem",
    )(x, hg, omb, wout, gmem, wq, qn, jnp.asarray(_group_ones(wm, MEM_HD), BF16), mk, mv, wo)


FFN_CHUNKS = 2


def _ffn_kernel(h_ref, g_ref, wup_ref, cw_ref, cb_ref, wdn_ref, prev_ref, y_ref, cst_ref, carry_scr, *, dff):
    c = pl.program_id(1)

    @pl.when(c == 0)
    def _():
        carry_scr[...] = prev_ref[0]

    h = h_ref[...]
    hn = _rmsnorm_rows(h, g_ref[...]).astype(BF16)
    tm = h.shape[0]
    fc = dff // FFN_CHUNKS
    row = lax.broadcasted_iota(jnp.int32, (tm, fc), 0)
    acc = h
    for ci in range(FFN_CHUNKS):
        sl = slice(ci * fc, (ci + 1) * fc)
        u = jnp.dot(hn, wup_ref[:, sl], preferred_element_type=F32)
        v = jnp.dot(hn, wup_ref[:, dff + ci * fc:dff + (ci + 1) * fc], preferred_element_type=F32)
        prev = carry_scr[:, sl]
        u1 = jnp.where(row == 0, prev[1:2], pltpu.roll(u, 1, 0))
        u2 = jnp.where(row == 0, prev[0:1], jnp.where(row == 1, prev[1:2], pltpu.roll(u, 2, 0)))
        cw = cw_ref[:, sl]
        conv = cb_ref[:, sl] + u2 * cw[0:1] + u1 * cw[1:2] + u * cw[2:3]
        act = 0.5 * conv * (1.0 + lax.erf(conv * (2.0 ** -0.5))) * v
        acc = acc + jnp.dot(act.astype(BF16), wdn_ref[sl, :], preferred_element_type=F32)
        last = u[tm - (CONV_W - 1):tm]
        carry_scr[:, sl] = last
        cst_ref[0, :, sl] = last
    y_ref[...] = acc


def _ffn(h, g, wup, cw, cb, wdn, prev, batch, tm):
    n, d = h.shape
    t = n // batch
    assert t % tm == 0 and tm >= CONV_W - 1
    nt = t // tm
    dff = wdn.shape[0]
    assert dff % (FFN_CHUNKS * LANES) == 0
    tok = pl.BlockSpec((tm, d), lambda b, i: (b * nt + i, 0))
    st = pl.BlockSpec((1, CONV_W - 1, dff), lambda b, i: (b, 0, 0))
    return pl.pallas_call(
        functools.partial(_ffn_kernel, dff=dff), grid=(batch, nt),
        in_specs=[tok, _const_spec((1, d)), _const_spec(wup.shape), _const_spec(cw.shape), _const_spec((1, dff)),
                  _const_spec(wdn.shape), st],
        out_specs=[tok, st],
        out_shape=[jax.ShapeDtypeStruct((n, d), F32), jax.ShapeDtypeStruct((batch, CONV_W - 1, dff), F32)],
        scratch_shapes=[pltpu.VMEM((CONV_W - 1, dff), F32)],
        compiler_params=_cparams("parallel", "arbitrary"), name="conv_ffn",
    )(h, g, wup, cw, cb, wdn, prev)


PROMPT_TILE = 512


def _state_to_rows(s):
    b = s.shape[0]
    return s.transpose(0, 3, 1, 2).reshape(b, HEAD_D, WIDTH)


def _rows_to_state(st):
    b = st.shape[0]
    return st.reshape(b, HEAD_D, H_HG, HEAD_D).transpose(0, 2, 3, 1)


def kernel(x_prompt, x_sample, cache_k, cache_v, page_table, state_hgrn, state_conv, cache_mem_k, cache_mem_v, mem_prompt, norm_mix, w_in, hg_lb_logits, hg_out_norm, mb_q_norm, mb_k_norm, rel_bias, w_out, norm_mem, norm_mem_src, w_mem_q, w_mem_kv, mem_q_norm, mem_k_norm, w_mem_o, norm_ffn, w_up, conv_w, conv_b, w_down):
    assert norm_mix.shape[0] == 1, "one layer"
    l = 0
    b, t, dm = x_prompt.shape
    db, ds, _ = x_sample.shape
    n_mem = mem_prompt.shape[1]
    dff = w_down.shape[1]
    row = lambda a: a[None]
    w_in_b, w_out_b = w_in[l].astype(BF16), w_out[l].astype(BF16)
    w_q_b, w_kv_b, w_o_b = w_mem_q[l].astype(BF16), w_mem_kv[l].astype(BF16), w_mem_o[l].astype(BF16)
    w_up_b, w_dn_b = w_up[l].astype(BF16), w_down[l].astype(BF16)
    qn, kn = row(jnp.tile(mb_q_norm[l], H_MB)), row(jnp.tile(mb_k_norm[l], H_MB))
    gn = row(jnp.tile(hg_out_norm[l], H_HG))
    mqn, mkn = row(jnp.tile(mem_q_norm[l], H_MEM)), row(jnp.tile(mem_k_norm[l], H_MEM))

    def layer(x, batch, tile, s0t, moba, mk, mv, prev):
        n = x.shape[0]
        tm = min(PROMPT_TILE, n)
        slabs = batch if (n // batch) % tm == 0 else 1
        qhg, khg, ihg, logf, g, qmb, kmb, ksum, kt, vt, vtb = _inproj(
            x, row(norm_mix[l]), w_in_b, hg_lb_logits, qn, kn, slabs, tm)
        hg, st = _hgrn(qhg, khg, ihg, logf, g, s0t, gn, batch, min(tile, HG_TILE))
        omb = moba(qmb, kmb, ksum, kt, vt, vtb)
        h = _mixmem(x, hg, omb, w_out_b, row(norm_mem[l]), w_q_b, mqn, mk, mv, w_o_b, batch, tile)
        y, cst = _ffn(h, row(norm_ffn[l]), w_up_b, conv_w[l], row(conv_b[l]), w_dn_b, prev, batch, tile)
        return y, kt, vt, _rows_to_state(st), cst

    mk_p, mv_p = _memkv(mem_prompt.reshape(b * n_mem, dm), row(norm_mem_src[l]), w_kv_b, mkn, n_mem)
    moba_p = lambda q, k, ksum, kt, vt, vtb: _moba_prompt(q, k, vtb, ksum.reshape(b, t // MB_BLOCK, WIDTH), rel_bias, b)
    y_p, kt_p, vt_p, s_p, c_p = layer(
        x_prompt.reshape(b * t, dm), b, PROMPT_TILE, jnp.zeros((b, HEAD_D, WIDTH), F32), moba_p,
        mk_p.reshape(b, n_mem, -1), mv_p.reshape(b, n_mem, -1), jnp.zeros((b, CONV_W - 1, dff), F32))

    pool_kt = cache_k[l].transpose(0, 2, 3, 1)
    pool_vt = cache_v[l].transpose(0, 2, 3, 1)
    per_seq = lambda a: a.reshape(WIDTH, db, ds).transpose(1, 0, 2)

    def moba_s(q, k, ksum, kt, vt, vtb):
        sel = _sample_select(q, _page_ksum(pool_kt, page_table), ds)
        o = _sample_attend(q, per_seq(kt).reshape(db * H_MB, HEAD_D, ds), per_seq(vt).reshape(db * H_MB, HEAD_D, ds),
                           pool_kt, pool_vt, page_table, sel.reshape(-1), rel_bias, ds)
        return o.reshape(db, H_MB, ds, HEAD_D).transpose(0, 2, 1, 3).reshape(db * ds, WIDTH)

    y_s, kt_s, vt_s, s_s, c_s = layer(
        x_sample.reshape(db * ds, dm), db, ds, _state_to_rows(state_hgrn[l]), moba_s,
        cache_mem_k[l].reshape(db, n_mem, -1), cache_mem_v[l].reshape(db, n_mem, -1), state_conv[l])

    hd = lambda a, bb, tt: a.reshape(bb, H_MB, HEAD_D, tt).transpose(0, 3, 1, 2)[None]
    return (y_p.reshape(b, t, dm), y_s.reshape(db, ds, dm),
            hd(kt_p, b, t), hd(vt_p, b, t), hd(per_seq(kt_s), db, ds), hd(per_seq(vt_s), db, ds),
            s_p[None], s_s[None], c_p[None], c_s[None],
            mk_p.reshape(1, b, n_mem, H_MEM, MEM_HD), mv_p.reshape(1, b, n_mem, H_MEM, MEM_HD))
```

```python
import functools
import math

import numpy as np
import jax
import jax.numpy as jnp
from jax import lax
from jax.experimental import pallas as pl
from jax.experimental.pallas import tpu as pltpu

F32 = jnp.float32
BF16 = jnp.bfloat16
EPS = 1e-6

H_HG = 8
H_MB = 8
HEAD_D = 64
WIDTH = 512
MB_BLOCK = 256
MB_TOPK = 3
NUM_BUCKETS = 32
MAX_DISTANCE = 8192
H_MEM = 4
MEM_HD = 128
CONV_W = 3

LANES = 128
SUBLANES = 8
VMEM_LIMIT = 56 * 1024 * 1024

HG_SUB = 8
HG_GROUP = 128
HG_SEQS = 4
HG_TILE = 128


def _cparams(*sem):
    return pltpu.CompilerParams(dimension_semantics=sem, vmem_limit_bytes=VMEM_LIMIT)


def _const_spec(shape):
    nd = len(shape)
    return pl.BlockSpec(shape, lambda *_: (0,) * nd, pipeline_mode=pl.Buffered(1))


def _group_ones(width, group):
    i = np.arange(width) // group
    return (i[:, None] == i[None, :]).astype(np.float32)


def _split_dot(x, ones_bf16):
    hi = x.astype(BF16)
    lo = (x - hi.astype(F32)).astype(BF16)
    return (jnp.dot(hi, ones_bf16, preferred_element_type=F32)
            + jnp.dot(lo, ones_bf16, preferred_element_type=F32))


def _seg_rmsnorm(x, gain, ones_bf16, seg):
    ms = _split_dot(x * x, ones_bf16) * (1.0 / seg)
    return x * lax.rsqrt(ms + EPS) * gain


def _rmsnorm_rows(x, gain):
    ms = jnp.mean(x * x, axis=-1, keepdims=True)
    return x * lax.rsqrt(ms + EPS) * gain


def _inproj_kernel(x_ref, gmix_ref, w_ref, wvt_ref, lbl_ref, qn_ref, kn_ref, ones_ref,
                   qhg_ref, khg_ref, ihg_ref, logf_ref, g_ref, qmb_ref, kmb_ref, ksum_ref, kt_ref, vt_ref, vtb_ref):
    xb = _rmsnorm_rows(x_ref[...], gmix_ref[...]).astype(BF16)

    def proj(i):
        return jnp.dot(xb, w_ref[:, i * WIDTH:(i + 1) * WIDTH], preferred_element_type=F32)

    qhg_ref[...] = proj(0)
    f_logit = proj(1)
    logits = lbl_ref[...]
    e = jnp.exp(logits - jnp.max(logits, axis=0, keepdims=True))
    lb = e[0:1] / jnp.sum(e, axis=0, keepdims=True)
    logf_ref[...] = jnp.log(lb + (1.0 - lb) * jax.nn.sigmoid(f_logit))
    khg_ref[...] = (1.0 - lb) * jax.nn.sigmoid(-f_logit)
    ihg_ref[...] = proj(2)
    g_ref[...] = proj(3)
    ones = ones_ref[...]
    qmb_ref[...] = _seg_rmsnorm(proj(4), qn_ref[...], ones, HEAD_D)
    k = _seg_rmsnorm(proj(5), kn_ref[...], ones, HEAD_D)
    kmb_ref[...] = k.astype(BF16)
    for j in range(ksum_ref.shape[0]):
        rows = k.shape[0] // ksum_ref.shape[0]
        ksum_ref[j] = jnp.sum(k[j * rows:(j + 1) * rows], axis=0, keepdims=True)
    kt_ref[0] = k.T
    vt = lax.dot_general(wvt_ref[...], xb, (((1,), (1,)), ((), ())), preferred_element_type=F32)
    vt_ref[0] = vt
    vtb_ref[0] = vt.astype(BF16)


def _inproj(x, gmix, w_bf16, lb_logits, qn, kn, batch, tm):
    n, d = x.shape
    t = n // batch
    assert t % tm == 0
    tpb = t // tm
    grp = min(tm, MB_BLOCK)
    assert tm % grp == 0
    tok = pl.BlockSpec((tm, WIDTH), lambda i: (i, 0))
    tr = pl.BlockSpec((1, WIDTH, tm), lambda i: (i // tpb, 0, i % tpb))
    outs = ([jax.ShapeDtypeStruct((n, WIDTH), F32)] * 6 + [jax.ShapeDtypeStruct((n, WIDTH), BF16)]
            + [jax.ShapeDtypeStruct((n // grp, 1, WIDTH), F32)]
            + [jax.ShapeDtypeStruct((batch, WIDTH, t), F32)] * 2 + [jax.ShapeDtypeStruct((batch, WIDTH, t), BF16)])
    ones = jnp.asarray(_group_ones(WIDTH, HEAD_D), BF16)
    w_main, wvt = w_bf16[:, :6 * WIDTH], w_bf16[:, 6 * WIDTH:].T
    return pl.pallas_call(
        _inproj_kernel,
        grid=(n // tm,),
        in_specs=[pl.BlockSpec((tm, d), lambda i: (i, 0)), _const_spec((1, d)), _const_spec(w_main.shape),
                  _const_spec(wvt.shape), _const_spec(lb_logits.shape), _const_spec((1, WIDTH)),
                  _const_spec((1, WIDTH)), _const_spec((WIDTH, WIDTH))],
        out_specs=[tok] * 7 + [pl.BlockSpec((tm // grp, 1, WIDTH), lambda i: (i, 0, 0)), tr, tr, tr],
        out_shape=outs,
        compiler_params=_cparams("parallel"),
        name="inproj",
    )(x, gmix, w_main, wvt, lb_logits, qn, kn, ones)


def _hgrn_kernel(q_ref, k_ref, v_ref, lf_ref, g_ref, s0_ref, gn_ref, ones_ref, gmask_ref,
                 o_ref, sfin_ref, st_scr, a_scr, o_scr, *, tc, nseq):
    c = pl.program_id(1)
    ng = WIDTH // HG_GROUP
    hpg = HG_GROUP // HEAD_D
    gmask = gmask_ref[...]

    @pl.when(c == 0)
    def _():
        for bi in range(nseq):
            s0 = s0_ref[bi]
            for gi in range(ng):
                blk = s0[:, gi * HG_GROUP:(gi + 1) * HG_GROUP]
                st_scr[bi * ng + gi] = jnp.concatenate([blk] * hpg, axis=0) * gmask

    row = lax.broadcasted_iota(jnp.int32, (tc, WIDTH), 0) % HG_SUB
    for bi in range(nseq):
        a = lf_ref[bi]
        sh = 1
        while sh < HG_SUB:
            a = a + jnp.where(row >= sh, pltpu.roll(a, sh, 0), 0.0)
            sh *= 2
        a_scr[bi] = a

    ones = ones_ref[...]
    trow = lax.broadcasted_iota(jnp.int32, (HG_SUB, WIDTH), 0)

    def step(n, carry):
        off = pl.multiple_of(n * HG_SUB, HG_SUB)
        for bi in range(nseq):
            q = q_ref[bi, pl.ds(off, HG_SUB), :]
            k = k_ref[bi, pl.ds(off, HG_SUB), :]
            v = v_ref[bi, pl.ds(off, HG_SUB), :]
            al = a_scr[bi, pl.ds(off, HG_SUB), :]
            a_end = al[HG_SUB - 1:HG_SUB, :]
            qe = (q * jnp.exp(al)).astype(BF16)
            kd = (k * jnp.exp(a_end - al)).astype(BF16)
            dec = jnp.exp(a_end)
            vb = v.astype(BF16)
            parts = []
            for s in range(HG_SUB):
                a_s = a_scr[bi, pl.ds(off + s, 1), :]
                k_s = k_ref[bi, pl.ds(off + s, 1), :]
                e_s = q * k_s * jnp.exp(al - a_s)
                parts.append(jnp.where(trow >= s, e_s, 0.0))
            ecat = jnp.concatenate(parts, axis=0).astype(BF16)
            outs = []
            for gi in range(ng):
                sl = slice(gi * HG_GROUP, (gi + 1) * HG_GROUP)
                st = st_scr[bi * ng + gi]
                o_g = lax.dot_general(qe[:, sl], st.astype(BF16), (((1,), (1,)), ((), ())),
                                      preferred_element_type=F32)
                ag = jnp.dot(ecat[:, sl], ones, preferred_element_type=F32)
                for s in range(HG_SUB):
                    o_g = o_g + ag[s * HG_SUB:(s + 1) * HG_SUB, :] * v[s:s + 1, sl]
                outs.append(o_g)
                upd = lax.dot_general(vb[:, sl], kd[:, sl], (((0,), (0,)), ((), ())),
                                      preferred_element_type=F32)
                st_scr[bi * ng + gi] = st * dec[:, sl] + upd * gmask
            o_scr[bi, pl.ds(off, HG_SUB), :] = jnp.concatenate(outs, axis=1)
        return carry

    lax.fori_loop(0, tc // HG_SUB, step, 0)

    for bi in range(nseq):
        o = o_scr[bi]
        ms = jnp.concatenate(
            [_split_dot(o[:, gi * HG_GROUP:(gi + 1) * HG_GROUP] ** 2, ones) for gi in range(ng)], axis=1)
        g = g_ref[bi]
        o_ref[bi] = o * lax.rsqrt(ms * (1.0 / HEAD_D) + EPS) * gn_ref[...] * (g * jax.nn.sigmoid(g))

    @pl.when(c == pl.num_programs(1) - 1)
    def _():
        for bi in range(nseq):
            cols = []
            for gi in range(ng):
                st = st_scr[bi * ng + gi]
                acc = st[0:HEAD_D, :]
                for hh in range(1, hpg):
                    acc = acc + st[hh * HEAD_D:(hh + 1) * HEAD_D, :]
                cols.append(acc)
            sfin_ref[bi] = jnp.concatenate(cols, axis=1)


def _hgrn(q, k, v, logf, g, s0t, gn, batch, tc):
    n = q.shape[0]
    t = n // batch
    nseq = min(HG_SEQS, batch)
    assert t % tc == 0 and tc % HG_SUB == 0 and batch % nseq == 0
    ng = WIDTH // HG_GROUP
    seq = lambda a: a.reshape(batch, t, WIDTH)
    tok = pl.BlockSpec((nseq, tc, WIDTH), lambda b, c: (b, c, 0))
    st_spec = pl.BlockSpec((nseq, HEAD_D, WIDTH), lambda b, c: (b, 0, 0))
    gm = _group_ones(HG_GROUP, HEAD_D)
    kern = functools.partial(_hgrn_kernel, tc=tc, nseq=nseq)
    o, st = pl.pallas_call(
        kern,
        grid=(batch // nseq, t // tc),
        in_specs=[tok] * 5 + [st_spec, _const_spec((1, WIDTH)), _const_spec((HG_GROUP, HG_GROUP)),
                              _const_spec((HG_GROUP, HG_GROUP))],
        out_specs=[tok, st_spec],
        out_shape=[jax.ShapeDtypeStruct((batch, t, WIDTH), F32), jax.ShapeDtypeStruct((batch, HEAD_D, WIDTH), F32)],
        scratch_shapes=[pltpu.VMEM((nseq * ng, HG_GROUP, HG_GROUP), F32),
                        pltpu.VMEM((nseq, tc, WIDTH), F32), pltpu.VMEM((nseq, tc, WIDTH), F32)],
        compiler_params=_cparams("parallel", "arbitrary"),
        name="hgrn",
    )(seq(q), seq(k), seq(v), seq(logf), seq(g), s0t, gn, jnp.asarray(gm, BF16), jnp.asarray(gm, F32))
    return o.reshape(n, WIDTH), st


NEG = -1e30


def _bucket_table(max_dist):
    max_exact = NUM_BUCKETS // 2
    d = np.arange(max_dist)
    nf = np.maximum(d, 1).astype(np.float32)
    large = max_exact + (np.log(nf / max_exact) / math.log(MAX_DISTANCE / max_exact)
                         * (NUM_BUCKETS - max_exact)).astype(np.int32)
    bucket = np.where(d < max_exact, d, np.minimum(large, NUM_BUCKETS - 1))
    assert np.all(np.diff(bucket) >= 0) and bucket[-1] == NUM_BUCKETS - 1
    first = [int(np.argmax(bucket >= kk)) for kk in range(NUM_BUCKETS)]
    return bucket, first


_BUCKET, _BUCKET_FIRST = _bucket_table(2 * MAX_DISTANCE)


def _bias_of_distance(d, rb_ref, h):
    acc = jnp.full(d.shape, rb_ref[0, h], F32)
    for kk in range(1, NUM_BUCKETS):
        acc = jnp.where(d >= _BUCKET_FIRST[kk], rb_ref[kk, h], acc)
    return acc


def _far_block_table(nb):
    big = 4 * MB_BLOCK
    tab = np.zeros((nb, 5), np.int32)
    for delta in range(nb):
        lo, hi = max(delta * MB_BLOCK - (MB_BLOCK - 1), 0), delta * MB_BLOCK + (MB_BLOCK - 1)
        b0 = int(_BUCKET[lo])
        ks = [kk for kk in range(b0 + 1, NUM_BUCKETS) if _BUCKET_FIRST[kk] <= hi]
        assert delta < 2 or len(ks) <= 2
        cs = [_BUCKET_FIRST[kk] - delta * MB_BLOCK for kk in ks[:2]] + [big, big]
        tab[delta] = [b0, min(b0 + 1, NUM_BUCKETS - 1), min(b0 + 2, NUM_BUCKETS - 1), cs[0], cs[1]]
    return tab


def _top_blocks(gate, n_valid, n_top, axis=0):
    nb = gate.shape[axis]
    j = lax.broadcasted_iota(jnp.int32, gate.shape, axis)
    g = jnp.where(j < n_valid, gate, -jnp.inf)
    sel = jnp.zeros(gate.shape, F32)
    idxs, oks = [], []
    for _ in range(n_top):
        mx = jnp.max(g, axis=axis, keepdims=True)
        idx = jnp.min(jnp.where(g == mx, j, nb), axis=axis, keepdims=True)
        ok = jnp.where(mx > -jnp.inf, 1.0, 0.0)
        pick = jnp.where(j == idx, ok, 0.0) > 0.0
        sel = jnp.where(pick, 1.0, sel)
        g = jnp.where(pick, -jnp.inf, g)
        idxs.append(idx)
        oks.append(ok)
    return idxs, oks, sel


def _pair_masks(shape):
    lane = lax.broadcasted_iota(jnp.int32, shape, len(shape) - 1)
    return [lane < HEAD_D, lane >= HEAD_D]


def _moba_prompt_kernel(qi_ref, kj_ref, tab_ref, q_ref, k_ref, vt_ref, vtp_ref, ksum_ref, rb_ref, o_ref,
                        qm_scr, sel_scr, m_scr, l_scr, acc_scr, near_scr, alpha_scr, p_scr):
    p = pl.program_id(1)
    qi = qi_ref[p]
    kj = kj_ref[p]
    delta = qi - kj
    blk = MB_BLOCK
    dts = (lax.broadcasted_iota(jnp.int32, (blk, blk), 1)
           - lax.broadcasted_iota(jnp.int32, (blk, blk), 0))

    @pl.when(p == 0)
    def _():
        for h in range(H_MB):
            near_scr[h] = jnp.where(dts >= 0, _bias_of_distance(dts, rb_ref, h), NEG)
            near_scr[H_MB + h] = _bias_of_distance(dts + blk, rb_ref, h)

    @pl.when(kj == 0)
    def _():
        q = q_ref[...]
        kmean = ksum_ref[0] * (1.0 / blk)
        m_scr[...] = jnp.full(m_scr.shape, NEG, F32)
        l_scr[...] = jnp.zeros(l_scr.shape, F32)
        acc_scr[...] = jnp.zeros(acc_scr.shape, F32)
        p_scr[H_MB:] = jnp.zeros((H_MB, blk, blk), BF16)
        alpha_scr[H_MB:] = jnp.ones((H_MB, 1, blk), F32)
        masks = _pair_masks((blk, LANES))
        for h in range(H_MB):
            sl = slice((h // 2) * LANES, (h // 2 + 1) * LANES)
            qh = jnp.where(masks[h % 2], q[:, sl], 0.0)
            qm_scr[h] = (qh * (HEAD_D ** -0.5)).astype(BF16)
            gate = lax.dot_general(kmean[:, sl], qh, (((1,), (1,)), ((), ())),
                                   precision=lax.Precision.HIGHEST, preferred_element_type=F32)
            _, _, sel = _top_blocks(gate, qi, MB_TOPK)
            sel_scr[h] = jnp.where(sel > 0.0, 0.0, NEG)

    def attend(bias_of, cur, own_block=False):
        prv = H_MB - cur
        for h in range(H_MB):
            sl = slice((h // 2) * LANES, (h // 2 + 1) * LANES)
            s_all = lax.dot_general(k_ref[:, sl], qm_scr[h], (((1,), (1,)), ((), ())),
                                    preferred_element_type=F32)
            for qh in range(blk // LANES):
                ql = slice(qh * LANES, (qh + 1) * LANES)
                s = s_all[:, ql] + bias_of(h, qh)
                m_old = m_scr[h, :, ql]
                m_new = jnp.maximum(m_old, jnp.max(s, axis=0, keepdims=True))
                alpha = jnp.exp(m_old - m_new)
                pexp = jnp.exp(s - m_new)
                l_scr[h, :, ql] = alpha * l_scr[h, :, ql] + jnp.sum(pexp, axis=0, keepdims=True)
                m_scr[h, :, ql] = m_new
                alpha_scr[cur + h, :, ql] = alpha
                p_scr[cur + h, :, ql] = pexp.astype(BF16)
        for slot, v_ref in ((prv, vtp_ref),) + (((cur, vt_ref),) if own_block else ()):
            for h in range(H_MB):
                hs = slice(h * HEAD_D, (h + 1) * HEAD_D)
                pv = jnp.dot(v_ref[0, hs, :], p_scr[slot + h], preferred_element_type=F32)
                acc_scr[hs, :] = alpha_scr[slot + h] * acc_scr[hs, :] + pv

    def for_each_slot(cond, body):
        for par in range(2):
            pl.when(cond & (kj % 2 == par))(functools.partial(body, par * H_MB))

    def selrow(h, qh):
        return sel_scr[h, pl.ds(kj, 1), :][:, qh * LANES:(qh + 1) * LANES]

    has_steps = tab_ref[delta, 3] < 2 * blk

    for_each_slot(delta == 0, lambda cur: attend(
        lambda h, qh: near_scr[h, :, qh * LANES:(qh + 1) * LANES], cur, own_block=True))

    for_each_slot(delta == 1, lambda cur: attend(
        lambda h, qh: near_scr[H_MB + h, :, qh * LANES:(qh + 1) * LANES] + selrow(h, qh), cur))

    for_each_slot((delta >= 2) & jnp.logical_not(has_steps), lambda cur: attend(
        lambda h, qh: rb_ref[tab_ref[delta, 0], h] + selrow(h, qh), cur))

    def stepped(cur):
        dts0 = dts[:, :LANES]

        def bias_of(h, qh):
            sr = selrow(h, qh)
            r0 = rb_ref[tab_ref[delta, 0], h] + sr
            r1 = rb_ref[tab_ref[delta, 1], h] + sr
            r2 = rb_ref[tab_ref[delta, 2], h] + sr
            m1 = dts0 >= tab_ref[delta, 3] - qh * LANES
            m2 = dts0 >= tab_ref[delta, 4] - qh * LANES
            return jnp.where(m2, r2, jnp.where(m1, r1, r0))

        attend(bias_of, cur)

    for_each_slot((delta >= 2) & has_steps, stepped)

    @pl.when(delta == 0)
    def _():
        for h in range(H_MB):
            hs = slice(h * HEAD_D, (h + 1) * HEAD_D)
            acc_scr[hs, :] = acc_scr[hs, :] / l_scr[h]
        o_ref[...] = acc_scr[...].T


def _moba_prompt(q, k, vt, ksum, rel_bias, batch):
    n = q.shape[0]
    t = n // batch
    assert t % MB_BLOCK == 0
    nb = t // MB_BLOCK
    qi = np.array([i for i in range(nb) for _ in range(i + 1)], np.int32)
    kj = np.array([j for i in range(nb) for j in range(i + 1)], np.int32)
    tab = _far_block_table(nb)
    blk = MB_BLOCK
    grid_spec = pltpu.PrefetchScalarGridSpec(
        num_scalar_prefetch=3,
        grid=(batch, len(qi)),
        in_specs=[
            pl.BlockSpec((blk, WIDTH), lambda b, p, qi, kj, tab: (b * nb + qi[p], 0)),
            pl.BlockSpec((blk, WIDTH), lambda b, p, qi, kj, tab: (b * nb + kj[p], 0)),
            pl.BlockSpec((1, WIDTH, blk), lambda b, p, qi, kj, tab: (b, 0, kj[p])),
            pl.BlockSpec((1, WIDTH, blk), lambda b, p, qi, kj, tab: (b, 0, jnp.maximum(kj[p] - 1, 0))),
            pl.BlockSpec((1, nb, WIDTH), lambda b, p, qi, kj, tab: (b, 0, 0)),
            pl.BlockSpec(memory_space=pltpu.SMEM),
        ],
        out_specs=pl.BlockSpec((blk, WIDTH), lambda b, p, qi, kj, tab: (b * nb + qi[p], 0)),
        scratch_shapes=[pltpu.VMEM((H_MB, blk, LANES), BF16), pltpu.VMEM((H_MB, nb, blk), F32),
                        pltpu.VMEM((H_MB, 1, blk), F32), pltpu.VMEM((H_MB, 1, blk), F32),
                        pltpu.VMEM((WIDTH, blk), F32), pltpu.VMEM((2 * H_MB, blk, blk), F32),
                        pltpu.VMEM((2 * H_MB, 1, blk), F32), pltpu.VMEM((2 * H_MB, blk, blk), BF16)],
    )
    return pl.pallas_call(
        _moba_prompt_kernel,
        grid_spec=grid_spec,
        out_shape=jax.ShapeDtypeStruct((n, WIDTH), F32),
        compiler_params=_cparams("parallel", "arbitrary"),
        name="moba_prompt",
    )(jnp.asarray(qi), jnp.asarray(kj), jnp.asarray(tab), q, k, vt, vt, ksum, rel_bias)


SAMPLE_PAGES_PER_STEP = 16


def _page_ksum_kernel(pt_ref, *refs, ppb):
    pages, out_ref = refs[:-1], refs[-1]
    j = pl.program_id(1)
    nblk = out_ref.shape[2]
    lane = lax.broadcasted_iota(jnp.int32, (WIDTH, nblk), 1)

    @pl.when(j == 0)
    def _():
        out_ref[...] = jnp.zeros(out_ref.shape, F32)

    acc = out_ref[0]
    for i in range(0, len(pages), ppb):
        blk = pages[i][0].reshape(WIDTH, pages[i].shape[-1])
        for e in range(1, ppb):
            blk = blk + pages[i + e][0].reshape(blk.shape)
        col = jnp.sum(blk, axis=1, keepdims=True)
        acc = jnp.where(lane == j * (len(pages) // ppb) + i // ppb, col, acc)
    out_ref[0] = acc


def _page_ksum(pool_t, page_table):
    db, n_pages = page_table.shape
    page = pool_t.shape[-1]
    ppb = MB_BLOCK // page
    pps = SAMPLE_PAGES_PER_STEP
    assert MB_BLOCK % page == 0 and n_pages % pps == 0 and pps % ppb == 0
    nblk = n_pages // ppb
    specs = [pl.BlockSpec((1,) + pool_t.shape[1:], (lambda b, j, pt, i=i: (pt[b, j * pps + i], 0, 0, 0)))
             for i in range(pps)]
    grid_spec = pltpu.PrefetchScalarGridSpec(
        num_scalar_prefetch=1, grid=(db, n_pages // pps), in_specs=specs,
        out_specs=pl.BlockSpec((1, WIDTH, nblk), lambda b, j, pt: (b, 0, 0)))
    return pl.pallas_call(
        functools.partial(_page_ksum_kernel, ppb=ppb), grid_spec=grid_spec,
        out_shape=jax.ShapeDtypeStruct((db, WIDTH, nblk), F32),
        compiler_params=_cparams("parallel", "arbitrary"), name="page_ksum",
    )(page_table, *([pool_t] * pps))


def _sample_select_kernel(q_ref, ksum_ref, idx_ref):
    q = q_ref[...]
    kmean_t = ksum_ref[0] * (1.0 / MB_BLOCK)
    nblk = kmean_t.shape[1]
    cols = []
    for h in range(H_MB):
        hs = slice(h * HEAD_D, (h + 1) * HEAD_D)
        gate = jnp.dot(q[:, hs], kmean_t[hs, :], precision=lax.Precision.HIGHEST,
                       preferred_element_type=F32)
        idxs, _, _ = _top_blocks(gate, nblk, MB_TOPK, axis=1)
        cols.extend(idxs)
    idx_ref[0] = jnp.concatenate(cols, axis=1)


def _sample_select(q, ksum_t, ds):
    db, _, nblk = ksum_t.shape
    assert nblk >= MB_TOPK
    return pl.pallas_call(
        _sample_select_kernel, grid=(db,),
        in_specs=[pl.BlockSpec((ds, WIDTH), lambda b: (b, 0)), pl.BlockSpec((1, WIDTH, nblk), lambda b: (b, 0, 0))],
        out_specs=pl.BlockSpec((1, ds, H_MB * MB_TOPK), lambda b: (b, 0, 0)),
        out_shape=jax.ShapeDtypeStruct((db, ds, H_MB * MB_TOPK), jnp.int32),
        compiler_params=_cparams("parallel"), name="sample_select",
    )(q, ksum_t)


def _sample_attend_kernel(pt_ref, sel_ref, q_ref, kn_ref, vn_ref, rb_ref, kpool_ref, vpool_ref, o_ref,
                          kbuf, vbuf, sems, *, ds, page, past_len):
    ppb = MB_BLOCK // page
    ntile = MB_TOPK * ppb
    step = pl.program_id(0)
    nstep = pl.num_programs(0)

    def copies(st, slot):
        b, h = st // H_MB, st % H_MB
        out = []
        for q in range(ds):
            for r in range(MB_TOPK):
                blk = sel_ref[((b * ds + q) * H_MB + h) * MB_TOPK + r]
                for e in range(ppb):
                    pg = pt_ref[b, blk * ppb + e]
                    dst = pl.ds((r * ppb + e) * page, page)
                    out.append(pltpu.make_async_copy(kpool_ref.at[pg, h], kbuf.at[slot, q, :, dst], sems.at[0, slot]))
                    out.append(pltpu.make_async_copy(vpool_ref.at[pg, h], vbuf.at[slot, q, :, dst], sems.at[1, slot]))
        return out

    @pl.when(step == 0)
    def _():
        for cp in copies(step, 0):
            cp.start()

    @pl.when(step + 1 < nstep)
    def _():
        for cp in copies(step + 1, (step + 1) % 2):
            cp.start()

    slot = step % 2
    for qq in range(ds):
        for tile in range(ntile):
            dst = pl.ds(tile * page, page)
            pltpu.make_async_copy(kpool_ref.at[0, 0], kbuf.at[slot, qq, :, dst], sems.at[0, slot]).wait()
            pltpu.make_async_copy(vpool_ref.at[0, 0], vbuf.at[slot, qq, :, dst], sems.at[1, slot]).wait()

    b, h = step // H_MB, step % H_MB
    q = (q_ref[0] * (HEAD_D ** -0.5)).astype(BF16)
    rowi = lax.broadcasted_iota(jnp.int32, (ds, ntile * page), 0)
    s_sel = jnp.zeros((ds, ntile * page), F32)
    kpos = jnp.zeros((ds, ntile * page), jnp.int32)
    lane = lax.broadcasted_iota(jnp.int32, (ds, page), 1)
    for qq in range(ds):
        res = jnp.dot(q, kbuf[slot, qq].astype(BF16), preferred_element_type=F32)
        s_sel = jnp.where(rowi == qq, res, s_sel)
        pieces = []
        for r in range(MB_TOPK):
            blk = sel_ref[((b * ds + qq) * H_MB + h) * MB_TOPK + r]
            for e in range(ppb):
                pieces.append(blk * MB_BLOCK + e * page + lane)
        kpos = jnp.where(rowi == qq, jnp.concatenate(pieces, axis=1), kpos)
    q_pos = past_len + lax.broadcasted_iota(jnp.int32, (ds, 1), 0)
    s_sel = s_sel + _bias_of_distance(q_pos - kpos, rb_ref, h)
    own = lax.broadcasted_iota(jnp.int32, (ds, ds), 1)
    qrow = lax.broadcasted_iota(jnp.int32, (ds, ds), 0)
    s_own = jnp.dot(q, kn_ref[0].astype(BF16), preferred_element_type=F32) \
        + _bias_of_distance(qrow - own, rb_ref, h)
    s_own = jnp.where(own <= qrow, s_own, NEG)
    m = jnp.maximum(jnp.max(s_sel, axis=1, keepdims=True), jnp.max(s_own, axis=1, keepdims=True))
    p_sel = jnp.exp(s_sel - m)
    p_own = jnp.exp(s_own - m)
    den = jnp.sum(p_sel, axis=1, keepdims=True) + jnp.sum(p_own, axis=1, keepdims=True)
    nt_dims = (((1,), (1,)), ((), ()))
    o = lax.dot_general(p_own.astype(BF16), vn_ref[0].astype(BF16), nt_dims, preferred_element_type=F32)
    rowo = lax.broadcasted_iota(jnp.int32, (ds, HEAD_D), 0)
    p_sel_b = p_sel.astype(BF16)
    for qq in range(ds):
        res = lax.dot_general(p_sel_b, vbuf[slot, qq].astype(BF16), nt_dims, preferred_element_type=F32)
        o = o + jnp.where(rowo == qq, res, 0.0)
    o_ref[0] = o / den


def _sample_attend(q, kt_new, vt_new, pool_kt, pool_vt, page_table, sel_flat, rel_bias, ds):
    db, n_pages = page_table.shape
    page = pool_kt.shape[-1]
    ppb = MB_BLOCK // page
    past_len = n_pages * page
    assert past_len % MB_BLOCK == 0
    ntile = MB_TOPK * ppb
    qh = q.reshape(db, ds, H_MB, HEAD_D).transpose(0, 2, 1, 3).reshape(db * H_MB, ds, HEAD_D)
    per_head = lambda shape: pl.BlockSpec((1,) + shape, lambda s, pt, sel: (s, 0, 0))
    grid_spec = pltpu.PrefetchScalarGridSpec(
        num_scalar_prefetch=2, grid=(db * H_MB,),
        in_specs=[per_head((ds, HEAD_D)), per_head((HEAD_D, ds)), per_head((HEAD_D, ds)),
                  pl.BlockSpec(memory_space=pltpu.SMEM), pl.BlockSpec(memory_space=pl.ANY),
                  pl.BlockSpec(memory_space=pl.ANY)],
        out_specs=per_head((ds, HEAD_D)),
        scratch_shapes=[pltpu.VMEM((2, ds, HEAD_D, ntile * page), F32), pltpu.VMEM((2, ds, HEAD_D, ntile * page), F32),
                        pltpu.SemaphoreType.DMA((2, 2))])
    kern = functools.partial(_sample_attend_kernel, ds=ds, page=page, past_len=past_len)
    return pl.pallas_call(
        kern, grid_spec=grid_spec, out_shape=jax.ShapeDtypeStruct((db * H_MB, ds, HEAD_D), F32),
        compiler_params=_cparams("arbitrary"), name="sample_attend",
    )(page_table, sel_flat, qh, kt_new, vt_new, rel_bias, pool_kt, pool_vt)


def _memkv_kernel(m_ref, g_ref, w_ref, kn_ref, ones_ref, k_ref, v_ref):
    xb = _rmsnorm_rows(m_ref[...], g_ref[...]).astype(BF16)
    wk = w_ref.shape[1] // 2
    k = jnp.dot(xb, w_ref[:, :wk], preferred_element_type=F32)
    k_ref[...] = _seg_rmsnorm(k, kn_ref[...], ones_ref[...], MEM_HD)
    v_ref[...] = jnp.dot(xb, w_ref[:, wk:], preferred_element_type=F32)


def _memkv(mem, g, w_bf16, kn, tm):
    n, d = mem.shape
    wk = w_bf16.shape[1] // 2
    out = pl.BlockSpec((tm, wk), lambda i: (i, 0))
    return pl.pallas_call(
        _memkv_kernel, grid=(n // tm,),
        in_specs=[pl.BlockSpec((tm, d), lambda i: (i, 0)), _const_spec((1, d)), _const_spec(w_bf16.shape),
                  _const_spec((1, wk)), _const_spec((wk, wk))],
        out_specs=[out, out], out_shape=[jax.ShapeDtypeStruct((n, wk), F32)] * 2,
        compiler_params=_cparams("parallel"), name="mem_kv",
    )(mem, g, w_bf16, kn, jnp.asarray(_group_ones(wk, MEM_HD), BF16))


def _mixmem_kernel(x_ref, hg_ref, omb_ref, wout_ref, gmem_ref, wq_ref, qn_ref, ones_ref, mk_ref, mv_ref, wo_ref,
                   h_ref):
    mix = (jnp.dot(hg_ref[...].astype(BF16), wout_ref[:WIDTH, :], preferred_element_type=F32)
           + jnp.dot(omb_ref[...].astype(BF16), wout_ref[WIDTH:, :], preferred_element_type=F32))
    h1 = x_ref[...] + mix
    hn = _rmsnorm_rows(h1, gmem_ref[...]).astype(BF16)
    q = _seg_rmsnorm(jnp.dot(hn, wq_ref[...], preferred_element_type=F32), qn_ref[...], ones_ref[...], MEM_HD)
    qb = q.astype(BF16)
    mk = mk_ref[0].astype(BF16)
    mv = mv_ref[0].astype(BF16)
    outs = []
    for h in range(H_MEM):
        sl = slice(h * MEM_HD, (h + 1) * MEM_HD)
        s = lax.dot_general(qb[:, sl], mk[:, sl], (((1,), (1,)), ((), ())),
                            preferred_element_type=F32) * (MEM_HD ** -0.5)
        e = jnp.exp(s - jnp.max(s, axis=-1, keepdims=True))
        p = e / jnp.sum(e, axis=-1, keepdims=True)
        outs.append(jnp.dot(p.astype(BF16), mv[:, sl], preferred_element_type=F32))
    o = jnp.concatenate(outs, axis=1).astype(BF16)
    h_ref[...] = h1 + jnp.dot(o, wo_ref[...], preferred_element_type=F32)


def _mixmem(x, hg, omb, wout, gmem, wq, qn, mk, mv, wo, batch, tm):
    n, d = x.shape
    t = n // batch
    assert t % tm == 0
    nt = t // tm
    n_mem, wm = mk.shape[1:]
    tok = lambda w: pl.BlockSpec((tm, w), lambda b, i: (b * nt + i, 0))
    mem = pl.BlockSpec((1, n_mem, wm), lambda b, i: (b, 0, 0))
    return pl.pallas_call(
        _mixmem_kernel, grid=(batch, nt),
        in_specs=[tok(d), tok(WIDTH), tok(WIDTH), _const_spec(wout.shape), _const_spec((1, d)),
                  _const_spec(wq.shape), _const_spec((1, wm)), _const_spec((wm, wm)), mem, mem,
                  _const_spec(wo.shape)],
        out_specs=tok(d), out_shape=jax.ShapeDtypeStruct((n, d), F32),
        compiler_params=_cparams("parallel", "parallel"), name="mix_mem",
    )(x, hg, omb, wout, gmem, wq, qn, jnp.asarray(_group_ones(wm, MEM_HD), BF16), mk, mv, wo)


FFN_CHUNKS = 2


def _ffn_kernel(h_ref, g_ref, wup_ref, cw_ref, cb_ref, wdn_ref, prev_ref, y_ref, cst_ref, carry_scr, *, dff):
    c = pl.program_id(1)

    @pl.when(c == 0)
    def _():
        carry_scr[...] = prev_ref[0]

    h = h_ref[...]
    hn = _rmsnorm_rows(h, g_ref[...]).astype(BF16)
    tm = h.shape[0]
    fc = dff // FFN_CHUNKS
    row = lax.broadcasted_iota(jnp.int32, (tm, fc), 0)
    acc = h
    for ci in range(FFN_CHUNKS):
        sl = slice(ci * fc, (ci + 1) * fc)
        u = jnp.dot(hn, wup_ref[:, sl], preferred_element_type=F32)
        v = jnp.dot(hn, wup_ref[:, dff + ci * fc:dff + (ci + 1) * fc], preferred_element_type=F32)
        prev = carry_scr[:, sl]
        u1 = jnp.where(row == 0, prev[1:2], pltpu.roll(u, 1, 0))
        u2 = jnp.where(row == 0, prev[0:1], jnp.where(row == 1, prev[1:2], pltpu.roll(u, 2, 0)))
        cw = cw_ref[:, sl]
        conv = cb_ref[:, sl] + u2 * cw[0:1] + u1 * cw[1:2] + u * cw[2:3]
        act = 0.5 * conv * (1.0 + lax.erf(conv * (2.0 ** -0.5))) * v
        acc = acc + jnp.dot(act.astype(BF16), wdn_ref[sl, :], preferred_element_type=F32)
        last = u[tm - (CONV_W - 1):tm]
        carry_scr[:, sl] = last
        cst_ref[0, :, sl] = last
    y_ref[...] = acc


def _ffn(h, g, wup, cw, cb, wdn, prev, batch, tm):
    n, d = h.shape
    t = n // batch
    assert t % tm == 0 and tm >= CONV_W - 1
    nt = t // tm
    dff = wdn.shape[0]
    assert dff % (FFN_CHUNKS * LANES) == 0
    tok = pl.BlockSpec((tm, d), lambda b, i: (b * nt + i, 0))
    st = pl.BlockSpec((1, CONV_W - 1, dff), lambda b, i: (b, 0, 0))
    return pl.pallas_call(
        functools.partial(_ffn_kernel, dff=dff), grid=(batch, nt),
        in_specs=[tok, _const_spec((1, d)), _const_spec(wup.shape), _const_spec(cw.shape), _const_spec((1, dff)),
                  _const_spec(wdn.shape), st],
        out_specs=[tok, st],
        out_shape=[jax.ShapeDtypeStruct((n, d), F32), jax.ShapeDtypeStruct((batch, CONV_W - 1, dff), F32)],
        scratch_shapes=[pltpu.VMEM((CONV_W - 1, dff), F32)],
        compiler_params=_cparams("parallel", "arbitrary"), name="conv_ffn",
    )(h, g, wup, cw, cb, wdn, prev)


PROMPT_TILE = 512


def _state_to_rows(s):
    b = s.shape[0]
    return s.transpose(0, 3, 1, 2).reshape(b, HEAD_D, WIDTH)


def _rows_to_state(st):
    b = st.shape[0]
    return st.reshape(b, HEAD_D, H_HG, HEAD_D).transpose(0, 2, 3, 1)


def kernel(x_prompt, x_sample, cache_k, cache_v, page_table, state_hgrn, state_conv, cache_mem_k, cache_mem_v, mem_prompt, norm_mix, w_in, hg_lb_logits, hg_out_norm, mb_q_norm, mb_k_norm, rel_bias, w_out, norm_mem, norm_mem_src, w_mem_q, w_mem_kv, mem_q_norm, mem_k_norm, w_mem_o, norm_ffn, w_up, conv_w, conv_b, w_down):
    assert norm_mix.shape[0] == 1, "one layer"
    l = 0
    b, t, dm = x_prompt.shape
    db, ds, _ = x_sample.shape
    n_mem = mem_prompt.shape[1]
    dff = w_down.shape[1]
    row = lambda a: a[None]
    w_in_b, w_out_b = w_in[l].astype(BF16), w_out[l].astype(BF16)
    w_q_b, w_kv_b, w_o_b = w_mem_q[l].astype(BF16), w_mem_kv[l].astype(BF16), w_mem_o[l].astype(BF16)
    w_up_b, w_dn_b = w_up[l].astype(BF16), w_down[l].astype(BF16)
    qn, kn = row(jnp.tile(mb_q_norm[l], H_MB)), row(jnp.tile(mb_k_norm[l], H_MB))
    gn = row(jnp.tile(hg_out_norm[l], H_HG))
    mqn, mkn = row(jnp.tile(mem_q_norm[l], H_MEM)), row(jnp.tile(mem_k_norm[l], H_MEM))

    def layer(x, batch, tile, s0t, moba, mk, mv, prev):
        n = x.shape[0]
        tm = min(PROMPT_TILE, n)
        slabs = batch if (n // batch) % tm == 0 else 1
        qhg, khg, ihg, logf, g, qmb, kmb, ksum, kt, vt, vtb = _inproj(
            x, row(norm_mix[l]), w_in_b, hg_lb_logits, qn, kn, slabs, tm)
        hg, st = _hgrn(qhg, khg, ihg, logf, g, s0t, gn, batch, min(tile, HG_TILE))
        omb = moba(qmb, kmb, ksum, kt, vt, vtb)
        h = _mixmem(x, hg, omb, w_out_b, row(norm_mem[l]), w_q_b, mqn, mk, mv, w_o_b, batch, tile)
        y, cst = _ffn(h, row(norm_ffn[l]), w_up_b, conv_w[l], row(conv_b[l]), w_dn_b, prev, batch, tile)
        return y, kt, vt, _rows_to_state(st), cst

    mk_p, mv_p = _memkv(mem_prompt.reshape(b * n_mem, dm), row(norm_mem_src[l]), w_kv_b, mkn, n_mem)
    moba_p = lambda q, k, ksum, kt, vt, vtb: _moba_prompt(q, k, vtb, ksum.reshape(b, t // MB_BLOCK, WIDTH), rel_bias, b)
    y_p, kt_p, vt_p, s_p, c_p = layer(
        x_prompt.reshape(b * t, dm), b, PROMPT_TILE, jnp.zeros((b, HEAD_D, WIDTH), F32), moba_p,
        mk_p.reshape(b, n_mem, -1), mv_p.reshape(b, n_mem, -1), jnp.zeros((b, CONV_W - 1, dff), F32))

    pool_kt = cache_k[l].transpose(0, 2, 3, 1)
    pool_vt = cache_v[l].transpose(0, 2, 3, 1)
    per_seq = lambda a: a.reshape(WIDTH, db, ds).transpose(1, 0, 2)

    def moba_s(q, k, ksum, kt, vt, vtb):
        sel = _sample_select(q, _page_ksum(pool_kt, page_table), ds)
        o = _sample_attend(q, per_seq(kt).reshape(db * H_MB, HEAD_D, ds), per_seq(vt).reshape(db * H_MB, HEAD_D, ds),
                           pool_kt, pool_vt, page_table, sel.reshape(-1), rel_bias, ds)
        return o.reshape(db, H_MB, ds, HEAD_D).transpose(0, 2, 1, 3).reshape(db * ds, WIDTH)

    y_s, kt_s, vt_s, s_s, c_s = layer(
        x_sample.reshape(db * ds, dm), db, ds, _state_to_rows(state_hgrn[l]), moba_s,
        cache_mem_k[l].reshape(db, n_mem, -1), cache_mem_v[l].reshape(db, n_mem, -1), state_conv[l])

    hd = lambda a, bb, tt: a.reshape(bb, H_MB, HEAD_D, tt).transpose(0, 3, 1, 2)[None]
    return (y_p.reshape(b, t, dm), y_s.reshape(db, ds, dm),
            hd(kt_p, b, t), hd(vt_p, b, t), hd(per_seq(kt_s), db, ds), hd(per_seq(vt_s), db, ds),
            s_p[None], s_s[None], c_p[None], c_s[None],
            mk_p.reshape(1, b, n_mem, H_MEM, MEM_HD), mv_p.reshape(1, b, n_mem, H_MEM, MEM_HD))
```

```python
import functools
import math

import numpy as np
import jax
import jax.numpy as jnp
from jax import lax
from jax.experimental import pallas as pl
from jax.experimental.pallas import tpu as pltpu

F32 = jnp.float32
BF16 = jnp.bfloat16
EPS = 1e-6

H_HG = 8
H_MB = 8
HEAD_D = 64
WIDTH = 512
MB_BLOCK = 256
MB_TOPK = 3
NUM_BUCKETS = 32
MAX_DISTANCE = 8192
H_MEM = 4
MEM_HD = 128
CONV_W = 3

LANES = 128
SUBLANES = 8
VMEM_LIMIT = 56 * 1024 * 1024

HG_SUB = 8
HG_GROUP = 128
HG_SEQS = 4
HG_TILE = 128


def _cparams(*sem):
    return pltpu.CompilerParams(dimension_semantics=sem, vmem_limit_bytes=VMEM_LIMIT)


def _const_spec(shape):
    nd = len(shape)
    return pl.BlockSpec(shape, lambda *_: (0,) * nd, pipeline_mode=pl.Buffered(1))


def _group_ones(width, group):
    i = np.arange(width) // group
    return (i[:, None] == i[None, :]).astype(np.float32)


def _split_dot(x, ones_bf16):
    hi = x.astype(BF16)
    lo = (x - hi.astype(F32)).astype(BF16)
    return (jnp.dot(hi, ones_bf16, preferred_element_type=F32)
            + jnp.dot(lo, ones_bf16, preferred_element_type=F32))


def _seg_rmsnorm(x, gain, ones_bf16, seg):
    ms = _split_dot(x * x, ones_bf16) * (1.0 / seg)
    return x * lax.rsqrt(ms + EPS) * gain


def _rmsnorm_rows(x, gain):
    ms = jnp.mean(x * x, axis=-1, keepdims=True)
    return x * lax.rsqrt(ms + EPS) * gain


def _inproj_kernel(x_ref, gmix_ref, w_ref, wvt_ref, lbl_ref, qn_ref, kn_ref, ones_ref,
                   qhg_ref, khg_ref, ihg_ref, logf_ref, g_ref, qmb_ref, kmb_ref, ksum_ref, kt_ref, vt_ref, vtb_ref):
    xb = _rmsnorm_rows(x_ref[...], gmix_ref[...]).astype(BF16)

    def proj(i):
        return jnp.dot(xb, w_ref[:, i * WIDTH:(i + 1) * WIDTH], preferred_element_type=F32)

    qhg_ref[...] = proj(0)
    f_logit = proj(1)
    logits = lbl_ref[...]
    e = jnp.exp(logits - jnp.max(logits, axis=0, keepdims=True))
    lb = e[0:1] / jnp.sum(e, axis=0, keepdims=True)
    logf_ref[...] = jnp.log(lb + (1.0 - lb) * jax.nn.sigmoid(f_logit))
    khg_ref[...] = (1.0 - lb) * jax.nn.sigmoid(-f_logit)
    ihg_ref[...] = proj(2)
    g_ref[...] = proj(3)
    ones = ones_ref[...]
    qmb_ref[...] = _seg_rmsnorm(proj(4), qn_ref[...], ones, HEAD_D)
    k = _seg_rmsnorm(proj(5), kn_ref[...], ones, HEAD_D)
    kmb_ref[...] = k.astype(BF16)
    for j in range(ksum_ref.shape[0]):
        rows = k.shape[0] // ksum_ref.shape[0]
        ksum_ref[j] = jnp.sum(k[j * rows:(j + 1) * rows], axis=0, keepdims=True)
    kt_ref[0] = k.T
    vt = lax.dot_general(wvt_ref[...], xb, (((1,), (1,)), ((), ())), preferred_element_type=F32)
    vt_ref[0] = vt
    vtb_ref[0] = vt.astype(BF16)


def _inproj(x, gmix, w_bf16, lb_logits, qn, kn, batch, tm):
    n, d = x.shape
    t = n // batch
    assert t % tm == 0
    tpb = t // tm
    grp = min(tm, MB_BLOCK)
    assert tm % grp == 0
    tok = pl.BlockSpec((tm, WIDTH), lambda i: (i, 0))
    tr = pl.BlockSpec((1, WIDTH, tm), lambda i: (i // tpb, 0, i % tpb))
    outs = ([jax.ShapeDtypeStruct((n, WIDTH), F32)] * 6 + [jax.ShapeDtypeStruct((n, WIDTH), BF16)]
            + [jax.ShapeDtypeStruct((n // grp, 1, WIDTH), F32)]
            + [jax.ShapeDtypeStruct((batch, WIDTH, t), F32)] * 2 + [jax.ShapeDtypeStruct((batch, WIDTH, t), BF16)])
    ones = jnp.asarray(_group_ones(WIDTH, HEAD_D), BF16)
    w_main, wvt = w_bf16[:, :6 * WIDTH], w_bf16[:, 6 * WIDTH:].T
    return pl.pallas_call(
        _inproj_kernel,
        grid=(n // tm,),
        in_specs=[pl.BlockSpec((tm, d), lambda i: (i, 0)), _const_spec((1, d)), _const_spec(w_main.shape),
                  _const_spec(wvt.shape), _const_spec(lb_logits.shape), _const_spec((1, WIDTH)),
                  _const_spec((1, WIDTH)), _const_spec((WIDTH, WIDTH))],
        out_specs=[tok] * 7 + [pl.BlockSpec((tm // grp, 1, WIDTH), lambda i: (i, 0, 0)), tr, tr, tr],
        out_shape=outs,
        compiler_params=_cparams("parallel"),
        name="inproj",
    )(x, gmix, w_main, wvt, lb_logits, qn, kn, ones)


def _hgrn_kernel(q_ref, k_ref, v_ref, lf_ref, g_ref, s0_ref, gn_ref, ones_ref, gmask_ref,
                 o_ref, sfin_ref, st_scr, a_scr, o_scr, *, tc, nseq):
    c = pl.program_id(1)
    ng = WIDTH // HG_GROUP
    hpg = HG_GROUP // HEAD_D
    gmask = gmask_ref[...]

    @pl.when(c == 0)
    def _():
        for bi in range(nseq):
            s0 = s0_ref[bi]
            for gi in range(ng):
                blk = s0[:, gi * HG_GROUP:(gi + 1) * HG_GROUP]
                st_scr[bi * ng + gi] = jnp.concatenate([blk] * hpg, axis=0) * gmask

    row = lax.broadcasted_iota(jnp.int32, (tc, WIDTH), 0) % HG_SUB
    for bi in range(nseq):
        a = lf_ref[bi]
        sh = 1
        while sh < HG_SUB:
            a = a + jnp.where(row >= sh, pltpu.roll(a, sh, 0), 0.0)
            sh *= 2
        a_scr[bi] = a

    ones = ones_ref[...]
    trow = lax.broadcasted_iota(jnp.int32, (HG_SUB, WIDTH), 0)

    def step(n, carry):
        off = pl.multiple_of(n * HG_SUB, HG_SUB)
        for bi in range(nseq):
            q = q_ref[bi, pl.ds(off, HG_SUB), :]
            k = k_ref[bi, pl.ds(off, HG_SUB), :]
            v = v_ref[bi, pl.ds(off, HG_SUB), :]
            al = a_scr[bi, pl.ds(off, HG_SUB), :]
            a_end = al[HG_SUB - 1:HG_SUB, :]
            qe = (q * jnp.exp(al)).astype(BF16)
            kd = (k * jnp.exp(a_end - al)).astype(BF16)
            dec = jnp.exp(a_end)
            vb = v.astype(BF16)
            parts = []
            for s in range(HG_SUB):
                a_s = a_scr[bi, pl.ds(off + s, 1), :]
                k_s = k_ref[bi, pl.ds(off + s, 1), :]
                e_s = q * k_s * jnp.exp(al - a_s)
                parts.append(jnp.where(trow >= s, e_s, 0.0))
            ecat = jnp.concatenate(parts, axis=0).astype(BF16)
            outs = []
            for gi in range(ng):
                sl = slice(gi * HG_GROUP, (gi + 1) * HG_GROUP)
                st = st_scr[bi * ng + gi]
                o_g = lax.dot_general(qe[:, sl], st.astype(BF16), (((1,), (1,)), ((), ())),
                                      preferred_element_type=F32)
                ag = jnp.dot(ecat[:, sl], ones, preferred_element_type=F32)
                for s in range(HG_SUB):
                    o_g = o_g + ag[s * HG_SUB:(s + 1) * HG_SUB, :] * v[s:s + 1, sl]
                outs.append(o_g)
                upd = lax.dot_general(vb[:, sl], kd[:, sl], (((0,), (0,)), ((), ())),
                                      preferred_element_type=F32)
                st_scr[bi * ng + gi] = st * dec[:, sl] + upd * gmask
            o_scr[bi, pl.ds(off, HG_SUB), :] = jnp.concatenate(outs, axis=1)
        return carry

    lax.fori_loop(0, tc // HG_SUB, step, 0)

    for bi in range(nseq):
        o = o_scr[bi]
        ms = jnp.concatenate(
            [_split_dot(o[:, gi * HG_GROUP:(gi + 1) * HG_GROUP] ** 2, ones) for gi in range(ng)], axis=1)
        g = g_ref[bi]
        o_ref[bi] = o * lax.rsqrt(ms * (1.0 / HEAD_D) + EPS) * gn_ref[...] * (g * jax.nn.sigmoid(g))

    @pl.when(c == pl.num_programs(1) - 1)
    def _():
        for bi in range(nseq):
            cols = []
            for gi in range(ng):
                st = st_scr[bi * ng + gi]
                acc = st[0:HEAD_D, :]
                for hh in range(1, hpg):
                    acc = acc + st[hh * HEAD_D:(hh + 1) * HEAD_D, :]
                cols.append(acc)
            sfin_ref[bi] = jnp.concatenate(cols, axis=1)


def _hgrn(q, k, v, logf, g, s0t, gn, batch, tc):
    n = q.shape[0]
    t = n // batch
    nseq = min(HG_SEQS, batch)
    assert t % tc == 0 and tc % HG_SUB == 0 and batch % nseq == 0
    ng = WIDTH // HG_GROUP
    seq = lambda a: a.reshape(batch, t, WIDTH)
    tok = pl.BlockSpec((nseq, tc, WIDTH), lambda b, c: (b, c, 0))
    st_spec = pl.BlockSpec((nseq, HEAD_D, WIDTH), lambda b, c: (b, 0, 0))
    gm = _group_ones(HG_GROUP, HEAD_D)
    kern = functools.partial(_hgrn_kernel, tc=tc, nseq=nseq)
    o, st = pl.pallas_call(
        kern,
        grid=(batch // nseq, t // tc),
        in_specs=[tok] * 5 + [st_spec, _const_spec((1, WIDTH)), _const_spec((HG_GROUP, HG_GROUP)),
                              _const_spec((HG_GROUP, HG_GROUP))],
        out_specs=[tok, st_spec],
        out_shape=[jax.ShapeDtypeStruct((batch, t, WIDTH), F32), jax.ShapeDtypeStruct((batch, HEAD_D, WIDTH), F32)],
        scratch_shapes=[pltpu.VMEM((nseq * ng, HG_GROUP, HG_GROUP), F32),
                        pltpu.VMEM((nseq, tc, WIDTH), F32), pltpu.VMEM((nseq, tc, WIDTH), F32)],
        compiler_params=_cparams("parallel", "arbitrary"),
        name="hgrn",
    )(seq(q), seq(k), seq(v), seq(logf), seq(g), s0t, gn, jnp.asarray(gm, BF16), jnp.asarray(gm, F32))
    return o.reshape(n, WIDTH), st


NEG = -1e30


def _bucket_table(max_dist):
    max_exact = NUM_BUCKETS // 2
    d = np.arange(max_dist)
    nf = np.maximum(d, 1).astype(np.float32)
    large = max_exact + (np.log(nf / max_exact) / math.log(MAX_DISTANCE / max_exact)
                         * (NUM_BUCKETS - max_exact)).astype(np.int32)
    bucket = np.where(d < max_exact, d, np.minimum(large, NUM_BUCKETS - 1))
    assert np.all(np.diff(bucket) >= 0) and bucket[-1] == NUM_BUCKETS - 1
    first = [int(np.argmax(bucket >= kk)) for kk in range(NUM_BUCKETS)]
    return bucket, first


_BUCKET, _BUCKET_FIRST = _bucket_table(2 * MAX_DISTANCE)


def _bias_of_distance(d, rb_ref, h):
    acc = jnp.full(d.shape, rb_ref[0, h], F32)
    for kk in range(1, NUM_BUCKETS):
        acc = jnp.where(d >= _BUCKET_FIRST[kk], rb_ref[kk, h], acc)
    return acc


def _far_block_table(nb):
    big = 4 * MB_BLOCK
    tab = np.zeros((nb, 5), np.int32)
    for delta in range(nb):
        lo, hi = max(delta * MB_BLOCK - (MB_BLOCK - 1), 0), delta * MB_BLOCK + (MB_BLOCK - 1)
        b0 = int(_BUCKET[lo])
        ks = [kk for kk in range(b0 + 1, NUM_BUCKETS) if _BUCKET_FIRST[kk] <= hi]
        assert delta < 2 or len(ks) <= 2
        cs = [_BUCKET_FIRST[kk] - delta * MB_BLOCK for kk in ks[:2]] + [big, big]
        tab[delta] = [b0, min(b0 + 1, NUM_BUCKETS - 1), min(b0 + 2, NUM_BUCKETS - 1), cs[0], cs[1]]
    return tab


def _top_blocks(gate, n_valid, n_top, axis=0):
    nb = gate.shape[axis]
    j = lax.broadcasted_iota(jnp.int32, gate.shape, axis)
    g = jnp.where(j < n_valid, gate, -jnp.inf)
    sel = jnp.zeros(gate.shape, F32)
    idxs, oks = [], []
    for _ in range(n_top):
        mx = jnp.max(g, axis=axis, keepdims=True)
        idx = jnp.min(jnp.where(g == mx, j, nb), axis=axis, keepdims=True)
        ok = jnp.where(mx > -jnp.inf, 1.0, 0.0)
        pick = jnp.where(j == idx, ok, 0.0) > 0.0
        sel = jnp.where(pick, 1.0, sel)
        g = jnp.where(pick, -jnp.inf, g)
        idxs.append(idx)
        oks.append(ok)
    return idxs, oks, sel


def _pair_masks(shape):
    lane = lax.broadcasted_iota(jnp.int32, shape, len(shape) - 1)
    return [lane < HEAD_D, lane >= HEAD_D]


def _moba_prompt_kernel(qi_ref, kj_ref, tab_ref, q_ref, k_ref, vt_ref, vtp_ref, ksum_ref, rb_ref, o_ref,
                        qm_scr, sel_scr, m_scr, l_scr, acc_scr, near_scr, alpha0_scr, p0_scr, alpha1_scr, p1_scr):
    p = pl.program_id(0)
    qi = qi_ref[p]
    kj = kj_ref[p]
    delta = qi - kj
    blk = MB_BLOCK
    nseq = q_ref.shape[0]
    dts = (lax.broadcasted_iota(jnp.int32, (blk, blk), 1)
           - lax.broadcasted_iota(jnp.int32, (blk, blk), 0))
    slots = ((alpha0_scr, p0_scr), (alpha1_scr, p1_scr))

    @pl.when(p == 0)
    def _():
        for h in range(H_MB):
            near_scr[h] = jnp.where(dts >= 0, _bias_of_distance(dts, rb_ref, h), NEG)
            near_scr[H_MB + h] = _bias_of_distance(dts + blk, rb_ref, h)

    def sequence(b, carry):
        hb = b * H_MB

        @pl.when(kj == 0)
        def _():
            q = q_ref[b]
            kmean = ksum_ref[b] * (1.0 / blk)
            masks = _pair_masks((blk, LANES))
            for h in range(H_MB):
                m_scr[hb + h] = jnp.full((1, blk), NEG, F32)
                l_scr[hb + h] = jnp.zeros((1, blk), F32)
                acc_scr[hb + h] = jnp.zeros((HEAD_D, blk), F32)
                p1_scr[hb + h] = jnp.zeros((blk, blk), BF16)
                alpha1_scr[hb + h] = jnp.ones((1, blk), F32)
                sl = slice((h // 2) * LANES, (h // 2 + 1) * LANES)
                qh = jnp.where(masks[h % 2], q[:, sl], 0.0)
                qm_scr[hb + h] = (qh * (HEAD_D ** -0.5)).astype(BF16)
                gate = lax.dot_general(kmean[:, sl], qh, (((1,), (1,)), ((), ())),
                                       precision=lax.Precision.HIGHEST, preferred_element_type=F32)
                _, _, sel = _top_blocks(gate, qi, MB_TOPK)
                sel_scr[hb + h] = jnp.where(sel > 0.0, 0.0, NEG)

        def attend(bias_of, par, own_block=False):
            (a_cur, p_cur), (a_prv, p_prv) = slots[par], slots[1 - par]
            for h in range(H_MB):
                sl = slice((h // 2) * LANES, (h // 2 + 1) * LANES)
                s_all = lax.dot_general(k_ref[b, :, sl], qm_scr[hb + h], (((1,), (1,)), ((), ())),
                                        preferred_element_type=F32)
                for qh in range(blk // LANES):
                    ql = slice(qh * LANES, (qh + 1) * LANES)
                    s = s_all[:, ql] + bias_of(h, qh)
                    m_old = m_scr[hb + h, :, ql]
                    m_new = jnp.maximum(m_old, jnp.max(s, axis=0, keepdims=True))
                    alpha = jnp.exp(m_old - m_new)
                    pexp = jnp.exp(s - m_new)
                    l_scr[hb + h, :, ql] = alpha * l_scr[hb + h, :, ql] + jnp.sum(pexp, axis=0, keepdims=True)
                    m_scr[hb + h, :, ql] = m_new
                    a_cur[hb + h, :, ql] = alpha
                    p_cur[hb + h, :, ql] = pexp.astype(BF16)
            for a_scr, p_scr, v_ref in ((a_prv, p_prv, vtp_ref),) + (((a_cur, p_cur, vt_ref),) if own_block else ()):
                for h in range(H_MB):
                    hs = slice(h * HEAD_D, (h + 1) * HEAD_D)
                    pv = jnp.dot(v_ref[b, hs, :], p_scr[hb + h], preferred_element_type=F32)
                    acc_scr[hb + h] = a_scr[hb + h] * acc_scr[hb + h] + pv

        def for_each_parity(cond, body):
            for par in range(2):
                pl.when(cond & (kj % 2 == par))(functools.partial(body, par))

        def selrow(h, qh):
            return sel_scr[hb + h, pl.ds(kj, 1), :][:, qh * LANES:(qh + 1) * LANES]

        has_steps = tab_ref[delta, 3] < 2 * blk

        for_each_parity(delta == 0, lambda par: attend(
            lambda h, qh: near_scr[h, :, qh * LANES:(qh + 1) * LANES], par, own_block=True))

        for_each_parity(delta == 1, lambda par: attend(
            lambda h, qh: near_scr[H_MB + h, :, qh * LANES:(qh + 1) * LANES] + selrow(h, qh), par))

        for_each_parity((delta >= 2) & jnp.logical_not(has_steps), lambda par: attend(
            lambda h, qh: rb_ref[tab_ref[delta, 0], h] + selrow(h, qh), par))

        def stepped(par):
            dts0 = dts[:, :LANES]

            def bias_of(h, qh):
                sr = selrow(h, qh)
                r0 = rb_ref[tab_ref[delta, 0], h] + sr
                r1 = rb_ref[tab_ref[delta, 1], h] + sr
                r2 = rb_ref[tab_ref[delta, 2], h] + sr
                m1 = dts0 >= tab_ref[delta, 3] - qh * LANES
                m2 = dts0 >= tab_ref[delta, 4] - qh * LANES
                return jnp.where(m2, r2, jnp.where(m1, r1, r0))

            attend(bias_of, par)

        for_each_parity((delta >= 2) & has_steps, stepped)

        @pl.when(delta == 0)
        def _():
            o_ref[b] = jnp.concatenate([acc_scr[hb + h] / l_scr[hb + h] for h in range(H_MB)], axis=0).T

        return carry

    lax.fori_loop(0, nseq, sequence, 0)


def _moba_prompt(q, k, vt, ksum, rel_bias, batch):
    n = q.shape[0]
    t = n // batch
    assert t % MB_BLOCK == 0
    nb = t // MB_BLOCK
    qi = np.array([i for i in range(nb) for _ in range(i + 1)], np.int32)
    kj = np.array([j for i in range(nb) for j in range(i + 1)], np.int32)
    tab = _far_block_table(nb)
    blk = MB_BLOCK
    rows = batch * H_MB
    grid_spec = pltpu.PrefetchScalarGridSpec(
        num_scalar_prefetch=3,
        grid=(len(qi),),
        in_specs=[
            pl.BlockSpec((batch, blk, WIDTH), lambda p, qi, kj, tab: (0, qi[p], 0)),
            pl.BlockSpec((batch, blk, WIDTH), lambda p, qi, kj, tab: (0, kj[p], 0)),
            pl.BlockSpec((batch, WIDTH, blk), lambda p, qi, kj, tab: (0, 0, kj[p])),
            pl.BlockSpec((batch, WIDTH, blk), lambda p, qi, kj, tab: (0, 0, jnp.maximum(kj[p] - 1, 0))),
            pl.BlockSpec((batch, nb, WIDTH), lambda p, qi, kj, tab: (0, 0, 0)),
            pl.BlockSpec(memory_space=pltpu.SMEM),
        ],
        out_specs=pl.BlockSpec((batch, blk, WIDTH), lambda p, qi, kj, tab: (0, qi[p], 0)),
        scratch_shapes=[pltpu.VMEM((rows, blk, LANES), BF16), pltpu.VMEM((rows, nb, blk), F32),
                        pltpu.VMEM((rows, 1, blk), F32), pltpu.VMEM((rows, 1, blk), F32),
                        pltpu.VMEM((rows, HEAD_D, blk), F32), pltpu.VMEM((2 * H_MB, blk, blk), F32),
                        pltpu.VMEM((rows, 1, blk), F32), pltpu.VMEM((rows, blk, blk), BF16),
                        pltpu.VMEM((rows, 1, blk), F32), pltpu.VMEM((rows, blk, blk), BF16)],
    )
    seq = lambda a: a.reshape(batch, t, WIDTH)
    return pl.pallas_call(
        _moba_prompt_kernel,
        grid_spec=grid_spec,
        out_shape=jax.ShapeDtypeStruct((batch, t, WIDTH), F32),
        compiler_params=_cparams("arbitrary"),
        name="moba_prompt",
    )(jnp.asarray(qi), jnp.asarray(kj), jnp.asarray(tab), seq(q), seq(k), vt, vt, ksum, rel_bias).reshape(n, WIDTH)


SAMPLE_PAGES_PER_STEP = 32


def _page_ksum_kernel(pt_ref, *refs, ppb):
    pages, out_ref = refs[:-1], refs[-1]
    j = pl.program_id(1)
    nblk = out_ref.shape[2]
    lane = lax.broadcasted_iota(jnp.int32, (WIDTH, nblk), 1)

    @pl.when(j == 0)
    def _():
        out_ref[...] = jnp.zeros(out_ref.shape, F32)

    acc = out_ref[0]
    for i in range(0, len(pages), ppb):
        blk = pages[i][0].reshape(WIDTH, pages[i].shape[-1])
        for e in range(1, ppb):
            blk = blk + pages[i + e][0].reshape(blk.shape)
        col = jnp.sum(blk, axis=1, keepdims=True)
        acc = jnp.where(lane == j * (len(pages) // ppb) + i // ppb, col, acc)
    out_ref[0] = acc


def _page_ksum(pool_t, page_table):
    db, n_pages = page_table.shape
    page = pool_t.shape[-1]
    ppb = MB_BLOCK // page
    pps = min(SAMPLE_PAGES_PER_STEP, n_pages)
    assert MB_BLOCK % page == 0 and n_pages % pps == 0 and pps % ppb == 0
    nblk = n_pages // ppb
    specs = [pl.BlockSpec((1,) + pool_t.shape[1:], (lambda b, j, pt, i=i: (pt[b, j * pps + i], 0, 0, 0)))
             for i in range(pps)]
    grid_spec = pltpu.PrefetchScalarGridSpec(
        num_scalar_prefetch=1, grid=(db, n_pages // pps), in_specs=specs,
        out_specs=pl.BlockSpec((1, WIDTH, nblk), lambda b, j, pt: (b, 0, 0)))
    return pl.pallas_call(
        functools.partial(_page_ksum_kernel, ppb=ppb), grid_spec=grid_spec,
        out_shape=jax.ShapeDtypeStruct((db, WIDTH, nblk), F32),
        compiler_params=_cparams("parallel", "arbitrary"), name="page_ksum",
    )(page_table, *([pool_t] * pps))


def _sample_select_kernel(q_ref, ksum_ref, idx_ref):
    q = q_ref[...]
    kmean_t = ksum_ref[0] * (1.0 / MB_BLOCK)
    nblk = kmean_t.shape[1]
    cols = []
    for h in range(H_MB):
        hs = slice(h * HEAD_D, (h + 1) * HEAD_D)
        gate = jnp.dot(q[:, hs], kmean_t[hs, :], precision=lax.Precision.HIGHEST,
                       preferred_element_type=F32)
        idxs, _, _ = _top_blocks(gate, nblk, MB_TOPK, axis=1)
        cols.extend(idxs)
    idx_ref[0] = jnp.concatenate(cols, axis=1)


def _sample_select(q, ksum_t, ds):
    db, _, nblk = ksum_t.shape
    assert nblk >= MB_TOPK
    return pl.pallas_call(
        _sample_select_kernel, grid=(db,),
        in_specs=[pl.BlockSpec((ds, WIDTH), lambda b: (b, 0)), pl.BlockSpec((1, WIDTH, nblk), lambda b: (b, 0, 0))],
        out_specs=pl.BlockSpec((1, ds, H_MB * MB_TOPK), lambda b: (b, 0, 0)),
        out_shape=jax.ShapeDtypeStruct((db, ds, H_MB * MB_TOPK), jnp.int32),
        compiler_params=_cparams("parallel"), name="sample_select",
    )(q, ksum_t)


def _sample_attend_kernel(pt_ref, sel_ref, q_ref, kn_ref, vn_ref, rb_ref, kpool_ref, vpool_ref, o_ref,
                          kbuf, vbuf, sems, *, ds, page, past_len):
    ppb = MB_BLOCK // page
    ntile = MB_TOPK * ppb
    step = pl.program_id(0)
    nstep = pl.num_programs(0)

    def copies(st, slot):
        b, h = st // H_MB, st % H_MB
        out = []
        for q in range(ds):
            for r in range(MB_TOPK):
                blk = sel_ref[((b * ds + q) * H_MB + h) * MB_TOPK + r]
                for e in range(ppb):
                    pg = pt_ref[b, blk * ppb + e]
                    dst = pl.ds((r * ppb + e) * page, page)
                    out.append(pltpu.make_async_copy(kpool_ref.at[pg, h], kbuf.at[slot, q, :, dst], sems.at[0, slot]))
                    out.append(pltpu.make_async_copy(vpool_ref.at[pg, h], vbuf.at[slot, q, :, dst], sems.at[1, slot]))
        return out

    @pl.when(step == 0)
    def _():
        for cp in copies(step, 0):
            cp.start()

    @pl.when(step + 1 < nstep)
    def _():
        for cp in copies(step + 1, (step + 1) % 2):
            cp.start()

    slot = step % 2
    for qq in range(ds):
        for tile in range(ntile):
            dst = pl.ds(tile * page, page)
            pltpu.make_async_copy(kpool_ref.at[0, 0], kbuf.at[slot, qq, :, dst], sems.at[0, slot]).wait()
            pltpu.make_async_copy(vpool_ref.at[0, 0], vbuf.at[slot, qq, :, dst], sems.at[1, slot]).wait()

    b, h = step // H_MB, step % H_MB
    q = (q_ref[0] * (HEAD_D ** -0.5)).astype(BF16)
    rowi = lax.broadcasted_iota(jnp.int32, (ds, ntile * page), 0)
    s_sel = jnp.zeros((ds, ntile * page), F32)
    kpos = jnp.zeros((ds, ntile * page), jnp.int32)
    lane = lax.broadcasted_iota(jnp.int32, (ds, page), 1)
    for qq in range(ds):
        res = jnp.dot(q, kbuf[slot, qq].astype(BF16), preferred_element_type=F32)
        s_sel = jnp.where(rowi == qq, res, s_sel)
        pieces = []
        for r in range(MB_TOPK):
            blk = sel_ref[((b * ds + qq) * H_MB + h) * MB_TOPK + r]
            for e in range(ppb):
                pieces.append(blk * MB_BLOCK + e * page + lane)
        kpos = jnp.where(rowi == qq, jnp.concatenate(pieces, axis=1), kpos)
    q_pos = past_len + lax.broadcasted_iota(jnp.int32, (ds, 1), 0)
    s_sel = s_sel + _bias_of_distance(q_pos - kpos, rb_ref, h)
    own = lax.broadcasted_iota(jnp.int32, (ds, ds), 1)
    qrow = lax.broadcasted_iota(jnp.int32, (ds, ds), 0)
    s_own = jnp.dot(q, kn_ref[0].astype(BF16), preferred_element_type=F32) \
        + _bias_of_distance(qrow - own, rb_ref, h)
    s_own = jnp.where(own <= qrow, s_own, NEG)
    m = jnp.maximum(jnp.max(s_sel, axis=1, keepdims=True), jnp.max(s_own, axis=1, keepdims=True))
    p_sel = jnp.exp(s_sel - m)
    p_own = jnp.exp(s_own - m)
    den = jnp.sum(p_sel, axis=1, keepdims=True) + jnp.sum(p_own, axis=1, keepdims=True)
    nt_dims = (((1,), (1,)), ((), ()))
    o = lax.dot_general(p_own.astype(BF16), vn_ref[0].astype(BF16), nt_dims, preferred_element_type=F32)
    rowo = lax.broadcasted_iota(jnp.int32, (ds, HEAD_D), 0)
    p_sel_b = p_sel.astype(BF16)
    for qq in range(ds):
        res = lax.dot_general(p_sel_b, vbuf[slot, qq].astype(BF16), nt_dims, preferred_element_type=F32)
        o = o + jnp.where(rowo == qq, res, 0.0)
    o_ref[0] = o / den


def _sample_attend(q, kt_new, vt_new, pool_kt, pool_vt, page_table, sel_flat, rel_bias, ds):
    db, n_pages = page_table.shape
    page = pool_kt.shape[-1]
    ppb = MB_BLOCK // page
    past_len = n_pages * page
    assert past_len % MB_BLOCK == 0
    ntile = MB_TOPK * ppb
    qh = q.reshape(db, ds, H_MB, HEAD_D).transpose(0, 2, 1, 3).reshape(db * H_MB, ds, HEAD_D)
    per_head = lambda shape: pl.BlockSpec((1,) + shape, lambda s, pt, sel: (s, 0, 0))
    grid_spec = pltpu.PrefetchScalarGridSpec(
        num_scalar_prefetch=2, grid=(db * H_MB,),
        in_specs=[per_head((ds, HEAD_D)), per_head((HEAD_D, ds)), per_head((HEAD_D, ds)),
                  pl.BlockSpec(memory_space=pltpu.SMEM), pl.BlockSpec(memory_space=pl.ANY),
                  pl.BlockSpec(memory_space=pl.ANY)],
        out_specs=per_head((ds, HEAD_D)),
        scratch_shapes=[pltpu.VMEM((2, ds, HEAD_D, ntile * page), F32), pltpu.VMEM((2, ds, HEAD_D, ntile * page), F32),
                        pltpu.SemaphoreType.DMA((2, 2))])
    kern = functools.partial(_sample_attend_kernel, ds=ds, page=page, past_len=past_len)
    return pl.pallas_call(
        kern, grid_spec=grid_spec, out_shape=jax.ShapeDtypeStruct((db * H_MB, ds, HEAD_D), F32),
        compiler_params=_cparams("arbitrary"), name="sample_attend",
    )(page_table, sel_flat, qh, kt_new, vt_new, rel_bias, pool_kt, pool_vt)


def _memkv_kernel(m_ref, g_ref, w_ref, kn_ref, ones_ref, k_ref, v_ref):
    xb = _rmsnorm_rows(m_ref[...], g_ref[...]).astype(BF16)
    wk = w_ref.shape[1] // 2
    k = jnp.dot(xb, w_ref[:, :wk], preferred_element_type=F32)
    k_ref[...] = _seg_rmsnorm(k, kn_ref[...], ones_ref[...], MEM_HD)
    v_ref[...] = jnp.dot(xb, w_ref[:, wk:], preferred_element_type=F32)


def _memkv(mem, g, w_bf16, kn, tm):
    n, d = mem.shape
    wk = w_bf16.shape[1] // 2
    out = pl.BlockSpec((tm, wk), lambda i: (i, 0))
    return pl.pallas_call(
        _memkv_kernel, grid=(n // tm,),
        in_specs=[pl.BlockSpec((tm, d), lambda i: (i, 0)), _const_spec((1, d)), _const_spec(w_bf16.shape),
                  _const_spec((1, wk)), _const_spec((wk, wk))],
        out_specs=[out, out], out_shape=[jax.ShapeDtypeStruct((n, wk), F32)] * 2,
        compiler_params=_cparams("parallel"), name="mem_kv",
    )(mem, g, w_bf16, kn, jnp.asarray(_group_ones(wk, MEM_HD), BF16))


def _mixmem_kernel(x_ref, hg_ref, omb_ref, wout_ref, gmem_ref, wq_ref, qn_ref, ones_ref, mk_ref, mv_ref, wo_ref,
                   h_ref):
    mix = (jnp.dot(hg_ref[...].astype(BF16), wout_ref[:WIDTH, :], preferred_element_type=F32)
           + jnp.dot(omb_ref[...].astype(BF16), wout_ref[WIDTH:, :], preferred_element_type=F32))
    h1 = x_ref[...] + mix
    hn = _rmsnorm_rows(h1, gmem_ref[...]).astype(BF16)
    q = _seg_rmsnorm(jnp.dot(hn, wq_ref[...], preferred_element_type=F32), qn_ref[...], ones_ref[...], MEM_HD)
    qb = q.astype(BF16)
    mk = mk_ref[0].astype(BF16)
    mv = mv_ref[0].astype(BF16)
    outs = []
    for h in range(H_MEM):
        sl = slice(h * MEM_HD, (h + 1) * MEM_HD)
        s = lax.dot_general(qb[:, sl], mk[:, sl], (((1,), (1,)), ((), ())),
                            preferred_element_type=F32) * (MEM_HD ** -0.5)
        e = jnp.exp(s - jnp.max(s, axis=-1, keepdims=True))
        p = e / jnp.sum(e, axis=-1, keepdims=True)
        outs.append(jnp.dot(p.astype(BF16), mv[:, sl], preferred_element_type=F32))
    o = jnp.concatenate(outs, axis=1).astype(BF16)
    h_ref[...] = h1 + jnp.dot(o, wo_ref[...], preferred_element_type=F32)


def _mixmem(x, hg, omb, wout, gmem, wq, qn, mk, mv, wo, batch, tm):
    n, d = x.shape
    t = n // batch
    assert t % tm == 0
    nt = t // tm
    n_mem, wm = mk.shape[1:]
    tok = lambda w: pl.BlockSpec((tm, w), lambda b, i: (b * nt + i, 0))
    mem = pl.BlockSpec((1, n_mem, wm), lambda b, i: (b, 0, 0))
    return pl.pallas_call(
        _mixmem_kernel, grid=(batch, nt),
        in_specs=[tok(d), tok(WIDTH), tok(WIDTH), _const_spec(wout.shape), _const_spec((1, d)),
                  _const_spec(wq.shape), _const_spec((1, wm)), _const_spec((wm, wm)), mem, mem,
                  _const_spec(wo.shape)],
        out_specs=tok(d), out_shape=jax.ShapeDtypeStruct((n, d), F32),
        compiler_params=_cparams("parallel", "parallel"), name="mix_mem",
    )(x, hg, omb, wout, gmem, wq, qn, jnp.asarray(_group_ones(wm, MEM_HD), BF16), mk, mv, wo)


FFN_CHUNKS = 2


def _ffn_kernel(h_ref, g_ref, wup_ref, cw_ref, cb_ref, wdn_ref, prev_ref, y_ref, cst_ref, carry_scr, *, dff):
    c = pl.program_id(1)

    @pl.when(c == 0)
    def _():
        carry_scr[...] = prev_ref[0]

    h = h_ref[...]
    hn = _rmsnorm_rows(h, g_ref[...]).astype(BF16)
    tm = h.shape[0]
    fc = dff // FFN_CHUNKS
    row = lax.broadcasted_iota(jnp.int32, (tm, fc), 0)
    acc = h
    for ci in range(FFN_CHUNKS):
        sl = slice(ci * fc, (ci + 1) * fc)
        u = jnp.dot(hn, wup_ref[:, sl], preferred_element_type=F32)
        v = jnp.dot(hn, wup_ref[:, dff + ci * fc:dff + (ci + 1) * fc], preferred_element_type=F32)
        prev = carry_scr[:, sl]
        u1 = jnp.where(row == 0, prev[1:2], pltpu.roll(u, 1, 0))
        u2 = jnp.where(row == 0, prev[0:1], jnp.where(row == 1, prev[1:2], pltpu.roll(u, 2, 0)))
        cw = cw_ref[:, sl]
        conv = cb_ref[:, sl] + u2 * cw[0:1] + u1 * cw[1:2] + u * cw[2:3]
        act = 0.5 * conv * (1.0 + lax.erf(conv * (2.0 ** -0.5))) * v
        acc = acc + jnp.dot(act.astype(BF16), wdn_ref[sl, :], preferred_element_type=F32)
        last = u[tm - (CONV_W - 1):tm]
        carry_scr[:, sl] = last
        cst_ref[0, :, sl] = last
    y_ref[...] = acc


def _ffn(h, g, wup, cw, cb, wdn, prev, batch, tm):
    n, d = h.shape
    t = n // batch
    assert t % tm == 0 and tm >= CONV_W - 1
    nt = t // tm
    dff = wdn.shape[0]
    assert dff % (FFN_CHUNKS * LANES) == 0
    tok = pl.BlockSpec((tm, d), lambda b, i: (b * nt + i, 0))
    st = pl.BlockSpec((1, CONV_W - 1, dff), lambda b, i: (b, 0, 0))
    return pl.pallas_call(
        functools.partial(_ffn_kernel, dff=dff), grid=(batch, nt),
        in_specs=[tok, _const_spec((1, d)), _const_spec(wup.shape), _const_spec(cw.shape), _const_spec((1, dff)),
                  _const_spec(wdn.shape), st],
        out_specs=[tok, st],
        out_shape=[jax.ShapeDtypeStruct((n, d), F32), jax.ShapeDtypeStruct((batch, CONV_W - 1, dff), F32)],
        scratch_shapes=[pltpu.VMEM((CONV_W - 1, dff), F32)],
        compiler_params=_cparams("parallel", "arbitrary"), name="conv_ffn",
    )(h, g, wup, cw, cb, wdn, prev)


PROMPT_TILE = 512


def _state_to_rows(s):
    b = s.shape[0]
    return s.transpose(0, 3, 1, 2).reshape(b, HEAD_D, WIDTH)


def _rows_to_state(st):
    b = st.shape[0]
    return st.reshape(b, HEAD_D, H_HG, HEAD_D).transpose(0, 2, 3, 1)


def kernel(x_prompt, x_sample, cache_k, cache_v, page_table, state_hgrn, state_conv, cache_mem_k, cache_mem_v, mem_prompt, norm_mix, w_in, hg_lb_logits, hg_out_norm, mb_q_norm, mb_k_norm, rel_bias, w_out, norm_mem, norm_mem_src, w_mem_q, w_mem_kv, mem_q_norm, mem_k_norm, w_mem_o, norm_ffn, w_up, conv_w, conv_b, w_down):
    assert norm_mix.shape[0] == 1, "one layer"
    l = 0
    b, t, dm = x_prompt.shape
    db, ds, _ = x_sample.shape
    n_mem = mem_prompt.shape[1]
    dff = w_down.shape[1]
    row = lambda a: a[None]
    w_in_b, w_out_b = w_in[l].astype(BF16), w_out[l].astype(BF16)
    w_q_b, w_kv_b, w_o_b = w_mem_q[l].astype(BF16), w_mem_kv[l].astype(BF16), w_mem_o[l].astype(BF16)
    w_up_b, w_dn_b = w_up[l].astype(BF16), w_down[l].astype(BF16)
    qn, kn = row(jnp.tile(mb_q_norm[l], H_MB)), row(jnp.tile(mb_k_norm[l], H_MB))
    gn = row(jnp.tile(hg_out_norm[l], H_HG))
    mqn, mkn = row(jnp.tile(mem_q_norm[l], H_MEM)), row(jnp.tile(mem_k_norm[l], H_MEM))

    def layer(x, batch, tile, s0t, moba, mk, mv, prev):
        n = x.shape[0]
        tm = min(PROMPT_TILE, n)
        slabs = batch if (n // batch) % tm == 0 else 1
        qhg, khg, ihg, logf, g, qmb, kmb, ksum, kt, vt, vtb = _inproj(
            x, row(norm_mix[l]), w_in_b, hg_lb_logits, qn, kn, slabs, tm)
        hg, st = _hgrn(qhg, khg, ihg, logf, g, s0t, gn, batch, min(tile, HG_TILE))
        omb = moba(qmb, kmb, ksum, kt, vt, vtb)
        h = _mixmem(x, hg, omb, w_out_b, row(norm_mem[l]), w_q_b, mqn, mk, mv, w_o_b, batch, tile)
        y, cst = _ffn(h, row(norm_ffn[l]), w_up_b, conv_w[l], row(conv_b[l]), w_dn_b, prev, batch, tile)
        return y, kt, vt, _rows_to_state(st), cst

    mk_p, mv_p = _memkv(mem_prompt.reshape(b * n_mem, dm), row(norm_mem_src[l]), w_kv_b, mkn, n_mem)
    moba_p = lambda q, k, ksum, kt, vt, vtb: _moba_prompt(q, k, vtb, ksum.reshape(b, t // MB_BLOCK, WIDTH), rel_bias, b)
    y_p, kt_p, vt_p, s_p, c_p = layer(
        x_prompt.reshape(b * t, dm), b, PROMPT_TILE, jnp.zeros((b, HEAD_D, WIDTH), F32), moba_p,
        mk_p.reshape(b, n_mem, -1), mv_p.reshape(b, n_mem, -1), jnp.zeros((b, CONV_W - 1, dff), F32))

    pool_kt = cache_k[l].transpose(0, 2, 3, 1)
    pool_vt = cache_v[l].transpose(0, 2, 3, 1)
    per_seq = lambda a: a.reshape(WIDTH, db, ds).transpose(1, 0, 2)

    def moba_s(q, k, ksum, kt, vt, vtb):
        sel = _sample_select(q, _page_ksum(pool_kt, page_table), ds)
        o = _sample_attend(q, per_seq(kt).reshape(db * H_MB, HEAD_D, ds), per_seq(vt).reshape(db * H_MB, HEAD_D, ds),
                           pool_kt, pool_vt, page_table, sel.reshape(-1), rel_bias, ds)
        return o.reshape(db, H_MB, ds, HEAD_D).transpose(0, 2, 1, 3).reshape(db * ds, WIDTH)

    y_s, kt_s, vt_s, s_s, c_s = layer(
        x_sample.reshape(db * ds, dm), db, ds, _state_to_rows(state_hgrn[l]), moba_s,
        cache_mem_k[l].reshape(db, n_mem, -1), cache_mem_v[l].reshape(db, n_mem, -1), state_conv[l])

    hd = lambda a, bb, tt: a.reshape(bb, H_MB, HEAD_D, tt).transpose(0, 3, 1, 2)[None]
    return (y_p.reshape(b, t, dm), y_s.reshape(db, ds, dm),
            hd(kt_p, b, t), hd(vt_p, b, t), hd(per_seq(kt_s), db, ds), hd(per_seq(vt_s), db, ds),
            s_p[None], s_s[None], c_p[None], c_s[None],
            mk_p.reshape(1, b, n_mem, H_MEM, MEM_HD), mv_p.reshape(1, b, n_mem, H_MEM, MEM_HD))
```

```python
import functools
import math

import numpy as np
import jax
import jax.numpy as jnp
from jax import lax
from jax.experimental import pallas as pl
from jax.experimental.pallas import tpu as pltpu

F32 = jnp.float32
BF16 = jnp.bfloat16
EPS = 1e-6

H_HG = 8
H_MB = 8
HEAD_D = 64
WIDTH = 512
MB_BLOCK = 256
MB_TOPK = 3
NUM_BUCKETS = 32
MAX_DISTANCE = 8192
H_MEM = 4
MEM_HD = 128
CONV_W = 3

LANES = 128
SUBLANES = 8
VMEM_LIMIT = 56 * 1024 * 1024

HG_SUB = 16
HG_GROUP = 128
HG_SEQS = 4
HG_TILE = 128


def _cparams(*sem):
    return pltpu.CompilerParams(dimension_semantics=sem, vmem_limit_bytes=VMEM_LIMIT)


def _const_spec(shape):
    nd = len(shape)
    return pl.BlockSpec(shape, lambda *_: (0,) * nd, pipeline_mode=pl.Buffered(1))


def _group_ones(width, group):
    i = np.arange(width) // group
    return (i[:, None] == i[None, :]).astype(np.float32)


def _split_dot(x, ones_bf16):
    hi = x.astype(BF16)
    lo = (x - hi.astype(F32)).astype(BF16)
    return (jnp.dot(hi, ones_bf16, preferred_element_type=F32)
            + jnp.dot(lo, ones_bf16, preferred_element_type=F32))


def _seg_rmsnorm(x, gain, ones_bf16, seg):
    ms = _split_dot(x * x, ones_bf16) * (1.0 / seg)
    return x * lax.rsqrt(ms + EPS) * gain


def _rmsnorm_rows(x, gain):
    ms = jnp.mean(x * x, axis=-1, keepdims=True)
    return x * lax.rsqrt(ms + EPS) * gain


def _inproj_kernel(x_ref, gmix_ref, w_ref, wvt_ref, lbl_ref, qn_ref, kn_ref, ones_ref,
                   qhg_ref, khg_ref, ihg_ref, logf_ref, g_ref, qmb_ref, kmb_ref, ksum_ref, kt_ref, vt_ref, vtb_ref):
    xb = _rmsnorm_rows(x_ref[...], gmix_ref[...]).astype(BF16)

    def proj(i):
        return jnp.dot(xb, w_ref[:, i * WIDTH:(i + 1) * WIDTH], preferred_element_type=F32)

    qhg_ref[...] = proj(0)
    f_logit = proj(1)
    logits = lbl_ref[...]
    e = jnp.exp(logits - jnp.max(logits, axis=0, keepdims=True))
    lb = e[0:1] / jnp.sum(e, axis=0, keepdims=True)
    logf_ref[...] = jnp.log(lb + (1.0 - lb) * jax.nn.sigmoid(f_logit))
    khg_ref[...] = (1.0 - lb) * jax.nn.sigmoid(-f_logit)
    ihg_ref[...] = proj(2)
    g_ref[...] = proj(3)
    ones = ones_ref[...]
    qmb_ref[...] = _seg_rmsnorm(proj(4), qn_ref[...], ones, HEAD_D)
    k = _seg_rmsnorm(proj(5), kn_ref[...], ones, HEAD_D)
    kmb_ref[...] = k.astype(BF16)
    for j in range(ksum_ref.shape[0]):
        rows = k.shape[0] // ksum_ref.shape[0]
        ksum_ref[j] = jnp.sum(k[j * rows:(j + 1) * rows], axis=0, keepdims=True)
    kt_ref[0] = k.T
    vt = lax.dot_general(wvt_ref[...], xb, (((1,), (1,)), ((), ())), preferred_element_type=F32)
    vt_ref[0] = vt
    vtb_ref[0] = vt.astype(BF16)


def _inproj(x, gmix, w_bf16, lb_logits, qn, kn, batch, tm):
    n, d = x.shape
    t = n // batch
    assert t % tm == 0
    tpb = t // tm
    grp = min(tm, MB_BLOCK)
    assert tm % grp == 0
    tok = pl.BlockSpec((tm, WIDTH), lambda i: (i, 0))
    tr = pl.BlockSpec((1, WIDTH, tm), lambda i: (i // tpb, 0, i % tpb))
    outs = ([jax.ShapeDtypeStruct((n, WIDTH), F32)] * 6 + [jax.ShapeDtypeStruct((n, WIDTH), BF16)]
            + [jax.ShapeDtypeStruct((n // grp, 1, WIDTH), F32)]
            + [jax.ShapeDtypeStruct((batch, WIDTH, t), F32)] * 2 + [jax.ShapeDtypeStruct((batch, WIDTH, t), BF16)])
    ones = jnp.asarray(_group_ones(WIDTH, HEAD_D), BF16)
    w_main, wvt = w_bf16[:, :6 * WIDTH], w_bf16[:, 6 * WIDTH:].T
    return pl.pallas_call(
        _inproj_kernel,
        grid=(n // tm,),
        in_specs=[pl.BlockSpec((tm, d), lambda i: (i, 0)), _const_spec((1, d)), _const_spec(w_main.shape),
                  _const_spec(wvt.shape), _const_spec(lb_logits.shape), _const_spec((1, WIDTH)),
                  _const_spec((1, WIDTH)), _const_spec((WIDTH, WIDTH))],
        out_specs=[tok] * 7 + [pl.BlockSpec((tm // grp, 1, WIDTH), lambda i: (i, 0, 0)), tr, tr, tr],
        out_shape=outs,
        compiler_params=_cparams("parallel"),
        name="inproj",
    )(x, gmix, w_main, wvt, lb_logits, qn, kn, ones)


def _hgrn_kernel(q_ref, k_ref, v_ref, lf_ref, g_ref, s0_ref, gn_ref, ones_ref, gmask_ref,
                 o_ref, sfin_ref, st_scr, a_scr, o_scr, *, tc, nseq, sub):
    c = pl.program_id(1)
    ng = WIDTH // HG_GROUP
    hpg = HG_GROUP // HEAD_D
    gmask = gmask_ref[...]

    @pl.when(c == 0)
    def _():
        for bi in range(nseq):
            s0 = s0_ref[bi]
            for gi in range(ng):
                blk = s0[:, gi * HG_GROUP:(gi + 1) * HG_GROUP]
                st_scr[bi * ng + gi] = jnp.concatenate([blk] * hpg, axis=0) * gmask

    row = lax.broadcasted_iota(jnp.int32, (tc, WIDTH), 0) % sub
    for bi in range(nseq):
        a = lf_ref[bi]
        sh = 1
        while sh < sub:
            a = a + jnp.where(row >= sh, pltpu.roll(a, sh, 0), 0.0)
            sh *= 2
        a_scr[bi] = a

    ones = ones_ref[...]
    trow = lax.broadcasted_iota(jnp.int32, (sub, WIDTH), 0)

    def step(n, carry):
        off = pl.multiple_of(n * sub, sub)
        for bi in range(nseq):
            q = q_ref[bi, pl.ds(off, sub), :]
            k = k_ref[bi, pl.ds(off, sub), :]
            v = v_ref[bi, pl.ds(off, sub), :]
            al = a_scr[bi, pl.ds(off, sub), :]
            a_end = al[sub - 1:sub, :]
            qe = (q * jnp.exp(al)).astype(BF16)
            kd = (k * jnp.exp(a_end - al)).astype(BF16)
            dec = jnp.exp(a_end)
            vb = v.astype(BF16)
            parts = []
            for s in range(sub):
                a_s = a_scr[bi, pl.ds(off + s, 1), :]
                k_s = k_ref[bi, pl.ds(off + s, 1), :]
                e_s = q * k_s * jnp.exp(al - a_s)
                parts.append(jnp.where(trow >= s, e_s, 0.0))
            ecat = jnp.concatenate(parts, axis=0).astype(BF16)
            outs = []
            for gi in range(ng):
                sl = slice(gi * HG_GROUP, (gi + 1) * HG_GROUP)
                st = st_scr[bi * ng + gi]
                o_g = lax.dot_general(qe[:, sl], st.astype(BF16), (((1,), (1,)), ((), ())),
                                      preferred_element_type=F32)
                ag = jnp.dot(ecat[:, sl], ones, preferred_element_type=F32)
                for s in range(sub):
                    o_g = o_g + ag[s * sub:(s + 1) * sub, :] * v[s:s + 1, sl]
                outs.append(o_g)
                upd = lax.dot_general(vb[:, sl], kd[:, sl], (((0,), (0,)), ((), ())),
                                      preferred_element_type=F32)
                st_scr[bi * ng + gi] = st * dec[:, sl] + upd * gmask
            o_scr[bi, pl.ds(off, sub), :] = jnp.concatenate(outs, axis=1)
        return carry

    lax.fori_loop(0, tc // sub, step, 0)

    for bi in range(nseq):
        o = o_scr[bi]
        ms = jnp.concatenate(
            [_split_dot(o[:, gi * HG_GROUP:(gi + 1) * HG_GROUP] ** 2, ones) for gi in range(ng)], axis=1)
        g = g_ref[bi]
        o_ref[bi] = o * lax.rsqrt(ms * (1.0 / HEAD_D) + EPS) * gn_ref[...] * (g * jax.nn.sigmoid(g))

    @pl.when(c == pl.num_programs(1) - 1)
    def _():
        for bi in range(nseq):
            cols = []
            for gi in range(ng):
                st = st_scr[bi * ng + gi]
                acc = st[0:HEAD_D, :]
                for hh in range(1, hpg):
                    acc = acc + st[hh * HEAD_D:(hh + 1) * HEAD_D, :]
                cols.append(acc)
            sfin_ref[bi] = jnp.concatenate(cols, axis=1)


def _hgrn(q, k, v, logf, g, s0t, gn, batch, tc):
    n = q.shape[0]
    t = n // batch
    nseq = min(HG_SEQS, batch)
    sub = min(HG_SUB, tc)
    assert t % tc == 0 and tc % sub == 0 and batch % nseq == 0
    ng = WIDTH // HG_GROUP
    seq = lambda a: a.reshape(batch, t, WIDTH)
    tok = pl.BlockSpec((nseq, tc, WIDTH), lambda b, c: (b, c, 0))
    st_spec = pl.BlockSpec((nseq, HEAD_D, WIDTH), lambda b, c: (b, 0, 0))
    gm = _group_ones(HG_GROUP, HEAD_D)
    kern = functools.partial(_hgrn_kernel, tc=tc, nseq=nseq, sub=sub)
    o, st = pl.pallas_call(
        kern,
        grid=(batch // nseq, t // tc),
        in_specs=[tok] * 5 + [st_spec, _const_spec((1, WIDTH)), _const_spec((HG_GROUP, HG_GROUP)),
                              _const_spec((HG_GROUP, HG_GROUP))],
        out_specs=[tok, st_spec],
        out_shape=[jax.ShapeDtypeStruct((batch, t, WIDTH), F32), jax.ShapeDtypeStruct((batch, HEAD_D, WIDTH), F32)],
        scratch_shapes=[pltpu.VMEM((nseq * ng, HG_GROUP, HG_GROUP), F32),
                        pltpu.VMEM((nseq, tc, WIDTH), F32), pltpu.VMEM((nseq, tc, WIDTH), F32)],
        compiler_params=_cparams("parallel", "arbitrary"),
        name="hgrn",
    )(seq(q), seq(k), seq(v), seq(logf), seq(g), s0t, gn, jnp.asarray(gm, BF16), jnp.asarray(gm, F32))
    return o.reshape(n, WIDTH), st


NEG = -1e30


def _bucket_table(max_dist):
    max_exact = NUM_BUCKETS // 2
    d = np.arange(max_dist)
    nf = np.maximum(d, 1).astype(np.float32)
    large = max_exact + (np.log(nf / max_exact) / math.log(MAX_DISTANCE / max_exact)
                         * (NUM_BUCKETS - max_exact)).astype(np.int32)
    bucket = np.where(d < max_exact, d, np.minimum(large, NUM_BUCKETS - 1))
    assert np.all(np.diff(bucket) >= 0) and bucket[-1] == NUM_BUCKETS - 1
    first = [int(np.argmax(bucket >= kk)) for kk in range(NUM_BUCKETS)]
    return bucket, first


_BUCKET, _BUCKET_FIRST = _bucket_table(2 * MAX_DISTANCE)


def _bias_of_distance(d, rb_ref, h):
    acc = jnp.full(d.shape, rb_ref[0, h], F32)
    for kk in range(1, NUM_BUCKETS):
        acc = jnp.where(d >= _BUCKET_FIRST[kk], rb_ref[kk, h], acc)
    return acc


def _far_block_table(nb):
    big = 4 * MB_BLOCK
    tab = np.zeros((nb, 5), np.int32)
    for delta in range(nb):
        lo, hi = max(delta * MB_BLOCK - (MB_BLOCK - 1), 0), delta * MB_BLOCK + (MB_BLOCK - 1)
        b0 = int(_BUCKET[lo])
        ks = [kk for kk in range(b0 + 1, NUM_BUCKETS) if _BUCKET_FIRST[kk] <= hi]
        assert delta < 2 or len(ks) <= 2
        cs = [_BUCKET_FIRST[kk] - delta * MB_BLOCK for kk in ks[:2]] + [big, big]
        tab[delta] = [b0, min(b0 + 1, NUM_BUCKETS - 1), min(b0 + 2, NUM_BUCKETS - 1), cs[0], cs[1]]
    return tab


def _top_blocks(gate, n_valid, n_top, axis=0):
    nb = gate.shape[axis]
    j = lax.broadcasted_iota(jnp.int32, gate.shape, axis)
    g = jnp.where(j < n_valid, gate, -jnp.inf)
    sel = jnp.zeros(gate.shape, F32)
    idxs, oks = [], []
    for _ in range(n_top):
        mx = jnp.max(g, axis=axis, keepdims=True)
        idx = jnp.min(jnp.where(g == mx, j, nb), axis=axis, keepdims=True)
        ok = jnp.where(mx > -jnp.inf, 1.0, 0.0)
        pick = jnp.where(j == idx, ok, 0.0) > 0.0
        sel = jnp.where(pick, 1.0, sel)
        g = jnp.where(pick, -jnp.inf, g)
        idxs.append(idx)
        oks.append(ok)
    return idxs, oks, sel


def _pair_masks(shape):
    lane = lax.broadcasted_iota(jnp.int32, shape, len(shape) - 1)
    return [lane < HEAD_D, lane >= HEAD_D]


def _moba_prompt_kernel(qi_ref, kj_ref, tab_ref, q_ref, k_ref, vt_ref, vtp_ref, ksum_ref, rb_ref, o_ref,
                        qm_scr, sel_scr, m_scr, l_scr, acc_scr, near_scr, alpha0_scr, p0_scr, alpha1_scr, p1_scr):
    p = pl.program_id(0)
    qi = qi_ref[p]
    kj = kj_ref[p]
    delta = qi - kj
    blk = MB_BLOCK
    nseq = q_ref.shape[0]
    dts = (lax.broadcasted_iota(jnp.int32, (blk, blk), 1)
           - lax.broadcasted_iota(jnp.int32, (blk, blk), 0))
    slots = ((alpha0_scr, p0_scr), (alpha1_scr, p1_scr))

    @pl.when(p == 0)
    def _():
        for h in range(H_MB):
            near_scr[h] = jnp.where(dts >= 0, _bias_of_distance(dts, rb_ref, h), NEG)
            near_scr[H_MB + h] = _bias_of_distance(dts + blk, rb_ref, h)

    def sequence(b, carry):
        hb = b * H_MB

        @pl.when(kj == 0)
        def _():
            q = q_ref[b]
            kmean = ksum_ref[b] * (1.0 / blk)
            masks = _pair_masks((blk, LANES))
            for h in range(H_MB):
                m_scr[hb + h] = jnp.full((1, blk), NEG, F32)
                l_scr[hb + h] = jnp.zeros((1, blk), F32)
                acc_scr[hb + h] = jnp.zeros((HEAD_D, blk), F32)
                p1_scr[hb + h] = jnp.zeros((blk, blk), BF16)
                alpha1_scr[hb + h] = jnp.ones((1, blk), F32)
                sl = slice((h // 2) * LANES, (h // 2 + 1) * LANES)
                qh = jnp.where(masks[h % 2], q[:, sl], 0.0)
                qm_scr[hb + h] = (qh * (HEAD_D ** -0.5)).astype(BF16)
                gate = lax.dot_general(kmean[:, sl], qh, (((1,), (1,)), ((), ())),
                                       precision=lax.Precision.HIGHEST, preferred_element_type=F32)
                _, _, sel = _top_blocks(gate, qi, MB_TOPK)
                sel_scr[hb + h] = jnp.where(sel > 0.0, 0.0, NEG)

        def attend(bias_of, par, own_block=False):
            (a_cur, p_cur), (a_prv, p_prv) = slots[par], slots[1 - par]
            for h in range(H_MB):
                sl = slice((h // 2) * LANES, (h // 2 + 1) * LANES)
                s_all = lax.dot_general(k_ref[b, :, sl], qm_scr[hb + h], (((1,), (1,)), ((), ())),
                                        preferred_element_type=F32)
                for qh in range(blk // LANES):
                    ql = slice(qh * LANES, (qh + 1) * LANES)
                    s = s_all[:, ql] + bias_of(h, qh)
                    m_old = m_scr[hb + h, :, ql]
                    m_new = jnp.maximum(m_old, jnp.max(s, axis=0, keepdims=True))
                    alpha = jnp.exp(m_old - m_new)
                    pexp = jnp.exp(s - m_new)
                    l_scr[hb + h, :, ql] = alpha * l_scr[hb + h, :, ql] + jnp.sum(pexp, axis=0, keepdims=True)
                    m_scr[hb + h, :, ql] = m_new
                    a_cur[hb + h, :, ql] = alpha
                    p_cur[hb + h, :, ql] = pexp.astype(BF16)
            for a_scr, p_scr, v_ref in ((a_prv, p_prv, vtp_ref),) + (((a_cur, p_cur, vt_ref),) if own_block else ()):
                for h in range(H_MB):
                    hs = slice(h * HEAD_D, (h + 1) * HEAD_D)
                    pv = jnp.dot(v_ref[b, hs, :], p_scr[hb + h], preferred_element_type=F32)
                    acc_scr[hb + h] = a_scr[hb + h] * acc_scr[hb + h] + pv

        def for_each_parity(cond, body):
            for par in range(2):
                pl.when(cond & (kj % 2 == par))(functools.partial(body, par))

        def selrow(h, qh):
            return sel_scr[hb + h, pl.ds(kj, 1), :][:, qh * LANES:(qh + 1) * LANES]

        has_steps = tab_ref[delta, 3] < 2 * blk

        for_each_parity(delta == 0, lambda par: attend(
            lambda h, qh: near_scr[h, :, qh * LANES:(qh + 1) * LANES], par, own_block=True))

        for_each_parity(delta == 1, lambda par: attend(
            lambda h, qh: near_scr[H_MB + h, :, qh * LANES:(qh + 1) * LANES] + selrow(h, qh), par))

        for_each_parity((delta >= 2) & jnp.logical_not(has_steps), lambda par: attend(
            lambda h, qh: rb_ref[tab_ref[delta, 0], h] + selrow(h, qh), par))

        def stepped(par):
            dts0 = dts[:, :LANES]

            def bias_of(h, qh):
                sr = selrow(h, qh)
                r0 = rb_ref[tab_ref[delta, 0], h] + sr
                r1 = rb_ref[tab_ref[delta, 1], h] + sr
                r2 = rb_ref[tab_ref[delta, 2], h] + sr
                m1 = dts0 >= tab_ref[delta, 3] - qh * LANES
                m2 = dts0 >= tab_ref[delta, 4] - qh * LANES
                return jnp.where(m2, r2, jnp.where(m1, r1, r0))

            attend(bias_of, par)

        for_each_parity((delta >= 2) & has_steps, stepped)

        @pl.when(delta == 0)
        def _():
            o_ref[b] = jnp.concatenate([acc_scr[hb + h] / l_scr[hb + h] for h in range(H_MB)], axis=0).T

        return carry

    lax.fori_loop(0, nseq, sequence, 0)


def _moba_prompt(q, k, vt, ksum, rel_bias, batch):
    n = q.shape[0]
    t = n // batch
    assert t % MB_BLOCK == 0
    nb = t // MB_BLOCK
    qi = np.array([i for i in range(nb) for _ in range(i + 1)], np.int32)
    kj = np.array([j for i in range(nb) for j in range(i + 1)], np.int32)
    tab = _far_block_table(nb)
    blk = MB_BLOCK
    rows = batch * H_MB
    grid_spec = pltpu.PrefetchScalarGridSpec(
        num_scalar_prefetch=3,
        grid=(len(qi),),
        in_specs=[
            pl.BlockSpec((batch, blk, WIDTH), lambda p, qi, kj, tab: (0, qi[p], 0)),
            pl.BlockSpec((batch, blk, WIDTH), lambda p, qi, kj, tab: (0, kj[p], 0)),
            pl.BlockSpec((batch, WIDTH, blk), lambda p, qi, kj, tab: (0, 0, kj[p])),
            pl.BlockSpec((batch, WIDTH, blk), lambda p, qi, kj, tab: (0, 0, jnp.maximum(kj[p] - 1, 0))),
            pl.BlockSpec((batch, nb, WIDTH), lambda p, qi, kj, tab: (0, 0, 0)),
            pl.BlockSpec(memory_space=pltpu.SMEM),
        ],
        out_specs=pl.BlockSpec((batch, blk, WIDTH), lambda p, qi, kj, tab: (0, qi[p], 0)),
        scratch_shapes=[pltpu.VMEM((rows, blk, LANES), BF16), pltpu.VMEM((rows, nb, blk), F32),
                        pltpu.VMEM((rows, 1, blk), F32), pltpu.VMEM((rows, 1, blk), F32),
                        pltpu.VMEM((rows, HEAD_D, blk), F32), pltpu.VMEM((2 * H_MB, blk, blk), F32),
                        pltpu.VMEM((rows, 1, blk), F32), pltpu.VMEM((rows, blk, blk), BF16),
                        pltpu.VMEM((rows, 1, blk), F32), pltpu.VMEM((rows, blk, blk), BF16)],
    )
    seq = lambda a: a.reshape(batch, t, WIDTH)
    return pl.pallas_call(
        _moba_prompt_kernel,
        grid_spec=grid_spec,
        out_shape=jax.ShapeDtypeStruct((batch, t, WIDTH), F32),
        compiler_params=_cparams("arbitrary"),
        name="moba_prompt",
    )(jnp.asarray(qi), jnp.asarray(kj), jnp.asarray(tab), seq(q), seq(k), vt, vt, ksum, rel_bias).reshape(n, WIDTH)


SAMPLE_PAGES_PER_STEP = 32


def _page_ksum_kernel(pt_ref, *refs, ppb):
    pages, out_ref = refs[:-1], refs[-1]
    j = pl.program_id(1)
    nblk = out_ref.shape[2]
    lane = lax.broadcasted_iota(jnp.int32, (WIDTH, nblk), 1)

    @pl.when(j == 0)
    def _():
        out_ref[...] = jnp.zeros(out_ref.shape, F32)

    acc = out_ref[0]
    for i in range(0, len(pages), ppb):
        blk = pages[i][0].reshape(WIDTH, pages[i].shape[-1])
        for e in range(1, ppb):
            blk = blk + pages[i + e][0].reshape(blk.shape)
        col = jnp.sum(blk, axis=1, keepdims=True)
        acc = jnp.where(lane == j * (len(pages) // ppb) + i // ppb, col, acc)
    out_ref[0] = acc


def _page_ksum(pool_t, page_table):
    db, n_pages = page_table.shape
    page = pool_t.shape[-1]
    ppb = MB_BLOCK // page
    pps = min(SAMPLE_PAGES_PER_STEP, n_pages)
    assert MB_BLOCK % page == 0 and n_pages % pps == 0 and pps % ppb == 0
    nblk = n_pages // ppb
    specs = [pl.BlockSpec((1,) + pool_t.shape[1:], (lambda b, j, pt, i=i: (pt[b, j * pps + i], 0, 0, 0)))
             for i in range(pps)]
    grid_spec = pltpu.PrefetchScalarGridSpec(
        num_scalar_prefetch=1, grid=(db, n_pages // pps), in_specs=specs,
        out_specs=pl.BlockSpec((1, WIDTH, nblk), lambda b, j, pt: (b, 0, 0)))
    return pl.pallas_call(
        functools.partial(_page_ksum_kernel, ppb=ppb), grid_spec=grid_spec,
        out_shape=jax.ShapeDtypeStruct((db, WIDTH, nblk), F32),
        compiler_params=_cparams("parallel", "arbitrary"), name="page_ksum",
    )(page_table, *([pool_t] * pps))


def _sample_select_kernel(q_ref, ksum_ref, idx_ref):
    q = q_ref[...]
    kmean_t = ksum_ref[0] * (1.0 / MB_BLOCK)
    nblk = kmean_t.shape[1]
    cols = []
    for h in range(H_MB):
        hs = slice(h * HEAD_D, (h + 1) * HEAD_D)
        gate = jnp.dot(q[:, hs], kmean_t[hs, :], precision=lax.Precision.HIGHEST,
                       preferred_element_type=F32)
        idxs, _, _ = _top_blocks(gate, nblk, MB_TOPK, axis=1)
        cols.extend(idxs)
    idx_ref[0] = jnp.concatenate(cols, axis=1)


def _sample_select(q, ksum_t, ds):
    db, _, nblk = ksum_t.shape
    assert nblk >= MB_TOPK
    return pl.pallas_call(
        _sample_select_kernel, grid=(db,),
        in_specs=[pl.BlockSpec((ds, WIDTH), lambda b: (b, 0)), pl.BlockSpec((1, WIDTH, nblk), lambda b: (b, 0, 0))],
        out_specs=pl.BlockSpec((1, ds, H_MB * MB_TOPK), lambda b: (b, 0, 0)),
        out_shape=jax.ShapeDtypeStruct((db, ds, H_MB * MB_TOPK), jnp.int32),
        compiler_params=_cparams("parallel"), name="sample_select",
    )(q, ksum_t)


def _sample_attend_kernel(pt_ref, sel_ref, q_ref, kn_ref, vn_ref, rb_ref, kpool_ref, vpool_ref, o_ref,
                          kbuf, vbuf, sems, *, ds, page, past_len):
    ppb = MB_BLOCK // page
    ntile = MB_TOPK * ppb
    step = pl.program_id(0)
    nstep = pl.num_programs(0)

    def copies(st, slot):
        b, h = st // H_MB, st % H_MB
        out = []
        for q in range(ds):
            for r in range(MB_TOPK):
                blk = sel_ref[((b * ds + q) * H_MB + h) * MB_TOPK + r]
                for e in range(ppb):
                    pg = pt_ref[b, blk * ppb + e]
                    dst = pl.ds((r * ppb + e) * page, page)
                    out.append(pltpu.make_async_copy(kpool_ref.at[pg, h], kbuf.at[slot, q, :, dst], sems.at[0, slot]))
                    out.append(pltpu.make_async_copy(vpool_ref.at[pg, h], vbuf.at[slot, q, :, dst], sems.at[1, slot]))
        return out

    @pl.when(step == 0)
    def _():
        for cp in copies(step, 0):
            cp.start()

    @pl.when(step + 1 < nstep)
    def _():
        for cp in copies(step + 1, (step + 1) % 2):
            cp.start()

    slot = step % 2
    for qq in range(ds):
        for tile in range(ntile):
            dst = pl.ds(tile * page, page)
            pltpu.make_async_copy(kpool_ref.at[0, 0], kbuf.at[slot, qq, :, dst], sems.at[0, slot]).wait()
            pltpu.make_async_copy(vpool_ref.at[0, 0], vbuf.at[slot, qq, :, dst], sems.at[1, slot]).wait()

    b, h = step // H_MB, step % H_MB
    q = (q_ref[0] * (HEAD_D ** -0.5)).astype(BF16)
    rowi = lax.broadcasted_iota(jnp.int32, (ds, ntile * page), 0)
    s_sel = jnp.zeros((ds, ntile * page), F32)
    kpos = jnp.zeros((ds, ntile * page), jnp.int32)
    lane = lax.broadcasted_iota(jnp.int32, (ds, page), 1)
    for qq in range(ds):
        res = jnp.dot(q, kbuf[slot, qq].astype(BF16), preferred_element_type=F32)
        s_sel = jnp.where(rowi == qq, res, s_sel)
        pieces = []
        for r in range(MB_TOPK):
            blk = sel_ref[((b * ds + qq) * H_MB + h) * MB_TOPK + r]
            for e in range(ppb):
                pieces.append(blk * MB_BLOCK + e * page + lane)
        kpos = jnp.where(rowi == qq, jnp.concatenate(pieces, axis=1), kpos)
    q_pos = past_len + lax.broadcasted_iota(jnp.int32, (ds, 1), 0)
    s_sel = s_sel + _bias_of_distance(q_pos - kpos, rb_ref, h)
    own = lax.broadcasted_iota(jnp.int32, (ds, ds), 1)
    qrow = lax.broadcasted_iota(jnp.int32, (ds, ds), 0)
    s_own = jnp.dot(q, kn_ref[0].astype(BF16), preferred_element_type=F32) \
        + _bias_of_distance(qrow - own, rb_ref, h)
    s_own = jnp.where(own <= qrow, s_own, NEG)
    m = jnp.maximum(jnp.max(s_sel, axis=1, keepdims=True), jnp.max(s_own, axis=1, keepdims=True))
    p_sel = jnp.exp(s_sel - m)
    p_own = jnp.exp(s_own - m)
    den = jnp.sum(p_sel, axis=1, keepdims=True) + jnp.sum(p_own, axis=1, keepdims=True)
    nt_dims = (((1,), (1,)), ((), ()))
    o = lax.dot_general(p_own.astype(BF16), vn_ref[0].astype(BF16), nt_dims, preferred_element_type=F32)
    rowo = lax.broadcasted_iota(jnp.int32, (ds, HEAD_D), 0)
    p_sel_b = p_sel.astype(BF16)
    for qq in range(ds):
        res = lax.dot_general(p_sel_b, vbuf[slot, qq].astype(BF16), nt_dims, preferred_element_type=F32)
        o = o + jnp.where(rowo == qq, res, 0.0)
    o_ref[0] = o / den


def _sample_attend(q, kt_new, vt_new, pool_kt, pool_vt, page_table, sel_flat, rel_bias, ds):
    db, n_pages = page_table.shape
    page = pool_kt.shape[-1]
    ppb = MB_BLOCK // page
    past_len = n_pages * page
    assert past_len % MB_BLOCK == 0
    ntile = MB_TOPK * ppb
    qh = q.reshape(db, ds, H_MB, HEAD_D).transpose(0, 2, 1, 3).reshape(db * H_MB, ds, HEAD_D)
    per_head = lambda shape: pl.BlockSpec((1,) + shape, lambda s, pt, sel: (s, 0, 0))
    grid_spec = pltpu.PrefetchScalarGridSpec(
        num_scalar_prefetch=2, grid=(db * H_MB,),
        in_specs=[per_head((ds, HEAD_D)), per_head((HEAD_D, ds)), per_head((HEAD_D, ds)),
                  pl.BlockSpec(memory_space=pltpu.SMEM), pl.BlockSpec(memory_space=pl.ANY),
                  pl.BlockSpec(memory_space=pl.ANY)],
        out_specs=per_head((ds, HEAD_D)),
        scratch_shapes=[pltpu.VMEM((2, ds, HEAD_D, ntile * page), F32), pltpu.VMEM((2, ds, HEAD_D, ntile * page), F32),
                        pltpu.SemaphoreType.DMA((2, 2))])
    kern = functools.partial(_sample_attend_kernel, ds=ds, page=page, past_len=past_len)
    return pl.pallas_call(
        kern, grid_spec=grid_spec, out_shape=jax.ShapeDtypeStruct((db * H_MB, ds, HEAD_D), F32),
        compiler_params=_cparams("arbitrary"), name="sample_attend",
    )(page_table, sel_flat, qh, kt_new, vt_new, rel_bias, pool_kt, pool_vt)


def _memkv_kernel(m_ref, g_ref, w_ref, kn_ref, ones_ref, k_ref, v_ref):
    xb = _rmsnorm_rows(m_ref[...], g_ref[...]).astype(BF16)
    wk = w_ref.shape[1] // 2
    k = jnp.dot(xb, w_ref[:, :wk], preferred_element_type=F32)
    k_ref[...] = _seg_rmsnorm(k, kn_ref[...], ones_ref[...], MEM_HD)
    v_ref[...] = jnp.dot(xb, w_ref[:, wk:], preferred_element_type=F32)


def _memkv(mem, g, w_bf16, kn, tm):
    n, d = mem.shape
    wk = w_bf16.shape[1] // 2
    out = pl.BlockSpec((tm, wk), lambda i: (i, 0))
    return pl.pallas_call(
        _memkv_kernel, grid=(n // tm,),
        in_specs=[pl.BlockSpec((tm, d), lambda i: (i, 0)), _const_spec((1, d)), _const_spec(w_bf16.shape),
                  _const_spec((1, wk)), _const_spec((wk, wk))],
        out_specs=[out, out], out_shape=[jax.ShapeDtypeStruct((n, wk), F32)] * 2,
        compiler_params=_cparams("parallel"), name="mem_kv",
    )(mem, g, w_bf16, kn, jnp.asarray(_group_ones(wk, MEM_HD), BF16))


def _mixmem_kernel(x_ref, hg_ref, omb_ref, wout_ref, gmem_ref, wq_ref, qn_ref, ones_ref, mk_ref, mv_ref, wo_ref,
                   h_ref):
    mix = (jnp.dot(hg_ref[...].astype(BF16), wout_ref[:WIDTH, :], preferred_element_type=F32)
           + jnp.dot(omb_ref[...].astype(BF16), wout_ref[WIDTH:, :], preferred_element_type=F32))
    h1 = x_ref[...] + mix
    hn = _rmsnorm_rows(h1, gmem_ref[...]).astype(BF16)
    q = _seg_rmsnorm(jnp.dot(hn, wq_ref[...], preferred_element_type=F32), qn_ref[...], ones_ref[...], MEM_HD)
    qb = q.astype(BF16)
    mk = mk_ref[0].astype(BF16)
    mv = mv_ref[0].astype(BF16)
    outs = []
    for h in range(H_MEM):
        sl = slice(h * MEM_HD, (h + 1) * MEM_HD)
        s = lax.dot_general(qb[:, sl], mk[:, sl], (((1,), (1,)), ((), ())),
                            preferred_element_type=F32) * (MEM_HD ** -0.5)
        e = jnp.exp(s - jnp.max(s, axis=-1, keepdims=True))
        p = e / jnp.sum(e, axis=-1, keepdims=True)
        outs.append(jnp.dot(p.astype(BF16), mv[:, sl], preferred_element_type=F32))
    o = jnp.concatenate(outs, axis=1).astype(BF16)
    h_ref[...] = h1 + jnp.dot(o, wo_ref[...], preferred_element_type=F32)


def _mixmem(x, hg, omb, wout, gmem, wq, qn, mk, mv, wo, batch, tm):
    n, d = x.shape
    t = n // batch
    assert t % tm == 0
    nt = t // tm
    n_mem, wm = mk.shape[1:]
    tok = lambda w: pl.BlockSpec((tm, w), lambda b, i: (b * nt + i, 0))
    mem = pl.BlockSpec((1, n_mem, wm), lambda b, i: (b, 0, 0))
    return pl.pallas_call(
        _mixmem_kernel, grid=(batch, nt),
        in_specs=[tok(d), tok(WIDTH), tok(WIDTH), _const_spec(wout.shape), _const_spec((1, d)),
                  _const_spec(wq.shape), _const_spec((1, wm)), _const_spec((wm, wm)), mem, mem,
                  _const_spec(wo.shape)],
        out_specs=tok(d), out_shape=jax.ShapeDtypeStruct((n, d), F32),
        compiler_params=_cparams("parallel", "parallel"), name="mix_mem",
    )(x, hg, omb, wout, gmem, wq, qn, jnp.asarray(_group_ones(wm, MEM_HD), BF16), mk, mv, wo)


FFN_CHUNKS = 2


def _ffn_kernel(h_ref, g_ref, wup_ref, cw_ref, cb_ref, wdn_ref, prev_ref, y_ref, cst_ref, carry_scr, *, dff):
    c = pl.program_id(1)

    @pl.when(c == 0)
    def _():
        carry_scr[...] = prev_ref[0]

    h = h_ref[...]
    hn = _rmsnorm_rows(h, g_ref[...]).astype(BF16)
    tm = h.shape[0]
    fc = dff // FFN_CHUNKS
    row = lax.broadcasted_iota(jnp.int32, (tm, fc), 0)
    acc = h
    for ci in range(FFN_CHUNKS):
        sl = slice(ci * fc, (ci + 1) * fc)
        u = jnp.dot(hn, wup_ref[:, sl], preferred_element_type=F32)
        v = jnp.dot(hn, wup_ref[:, dff + ci * fc:dff + (ci + 1) * fc], preferred_element_type=F32)
        prev = carry_scr[:, sl]
        u1 = jnp.where(row == 0, prev[1:2], pltpu.roll(u, 1, 0))
        u2 = jnp.where(row == 0, prev[0:1], jnp.where(row == 1, prev[1:2], pltpu.roll(u, 2, 0)))
        cw = cw_ref[:, sl]
        conv = cb_ref[:, sl] + u2 * cw[0:1] + u1 * cw[1:2] + u * cw[2:3]
        act = 0.5 * conv * (1.0 + lax.erf(conv * (2.0 ** -0.5))) * v
        acc = acc + jnp.dot(act.astype(BF16), wdn_ref[sl, :], preferred_element_type=F32)
        last = u[tm - (CONV_W - 1):tm]
        carry_scr[:, sl] = last
        cst_ref[0, :, sl] = last
    y_ref[...] = acc


def _ffn(h, g, wup, cw, cb, wdn, prev, batch, tm):
    n, d = h.shape
    t = n // batch
    assert t % tm == 0 and tm >= CONV_W - 1
    nt = t // tm
    dff = wdn.shape[0]
    assert dff % (FFN_CHUNKS * LANES) == 0
    tok = pl.BlockSpec((tm, d), lambda b, i: (b * nt + i, 0))
    st = pl.BlockSpec((1, CONV_W - 1, dff), lambda b, i: (b, 0, 0))
    return pl.pallas_call(
        functools.partial(_ffn_kernel, dff=dff), grid=(batch, nt),
        in_specs=[tok, _const_spec((1, d)), _const_spec(wup.shape), _const_spec(cw.shape), _const_spec((1, dff)),
                  _const_spec(wdn.shape), st],
        out_specs=[tok, st],
        out_shape=[jax.ShapeDtypeStruct((n, d), F32), jax.ShapeDtypeStruct((batch, CONV_W - 1, dff), F32)],
        scratch_shapes=[pltpu.VMEM((CONV_W - 1, dff), F32)],
        compiler_params=_cparams("parallel", "arbitrary"), name="conv_ffn",
    )(h, g, wup, cw, cb, wdn, prev)


PROMPT_TILE = 512


def _state_to_rows(s):
    b = s.shape[0]
    return s.transpose(0, 3, 1, 2).reshape(b, HEAD_D, WIDTH)


def _rows_to_state(st):
    b = st.shape[0]
    return st.reshape(b, HEAD_D, H_HG, HEAD_D).transpose(0, 2, 3, 1)


def kernel(x_prompt, x_sample, cache_k, cache_v, page_table, state_hgrn, state_conv, cache_mem_k, cache_mem_v, mem_prompt, norm_mix, w_in, hg_lb_logits, hg_out_norm, mb_q_norm, mb_k_norm, rel_bias, w_out, norm_mem, norm_mem_src, w_mem_q, w_mem_kv, mem_q_norm, mem_k_norm, w_mem_o, norm_ffn, w_up, conv_w, conv_b, w_down):
    assert norm_mix.shape[0] == 1, "one layer"
    l = 0
    b, t, dm = x_prompt.shape
    db, ds, _ = x_sample.shape
    n_mem = mem_prompt.shape[1]
    dff = w_down.shape[1]
    row = lambda a: a[None]
    w_in_b, w_out_b = w_in[l].astype(BF16), w_out[l].astype(BF16)
    w_q_b, w_kv_b, w_o_b = w_mem_q[l].astype(BF16), w_mem_kv[l].astype(BF16), w_mem_o[l].astype(BF16)
    w_up_b, w_dn_b = w_up[l].astype(BF16), w_down[l].astype(BF16)
    qn, kn = row(jnp.tile(mb_q_norm[l], H_MB)), row(jnp.tile(mb_k_norm[l], H_MB))
    gn = row(jnp.tile(hg_out_norm[l], H_HG))
    mqn, mkn = row(jnp.tile(mem_q_norm[l], H_MEM)), row(jnp.tile(mem_k_norm[l], H_MEM))

    def layer(x, batch, tile, s0t, moba, mk, mv, prev):
        n = x.shape[0]
        tm = min(PROMPT_TILE, n)
        slabs = batch if (n // batch) % tm == 0 else 1
        qhg, khg, ihg, logf, g, qmb, kmb, ksum, kt, vt, vtb = _inproj(
            x, row(norm_mix[l]), w_in_b, hg_lb_logits, qn, kn, slabs, tm)
        hg, st = _hgrn(qhg, khg, ihg, logf, g, s0t, gn, batch, min(tile, HG_TILE))
        omb = moba(qmb, kmb, ksum, kt, vt, vtb)
        h = _mixmem(x, hg, omb, w_out_b, row(norm_mem[l]), w_q_b, mqn, mk, mv, w_o_b, batch, tile)
        y, cst = _ffn(h, row(norm_ffn[l]), w_up_b, conv_w[l], row(conv_b[l]), w_dn_b, prev, batch, tile)
        return y, kt, vt, _rows_to_state(st), cst

    mk_p, mv_p = _memkv(mem_prompt.reshape(b * n_mem, dm), row(norm_mem_src[l]), w_kv_b, mkn, n_mem)
    moba_p = lambda q, k, ksum, kt, vt, vtb: _moba_prompt(q, k, vtb, ksum.reshape(b, t // MB_BLOCK, WIDTH), rel_bias, b)
    y_p, kt_p, vt_p, s_p, c_p = layer(
        x_prompt.reshape(b * t, dm), b, PROMPT_TILE, jnp.zeros((b, HEAD_D, WIDTH), F32), moba_p,
        mk_p.reshape(b, n_mem, -1), mv_p.reshape(b, n_mem, -1), jnp.zeros((b, CONV_W - 1, dff), F32))

    pool_kt = cache_k[l].transpose(0, 2, 3, 1)
    pool_vt = cache_v[l].transpose(0, 2, 3, 1)
    per_seq = lambda a: a.reshape(WIDTH, db, ds).transpose(1, 0, 2)

    def moba_s(q, k, ksum, kt, vt, vtb):
        sel = _sample_select(q, _page_ksum(pool_kt, page_table), ds)
        o = _sample_attend(q, per_seq(kt).reshape(db * H_MB, HEAD_D, ds), per_seq(vt).reshape(db * H_MB, HEAD_D, ds),
                           pool_kt, pool_vt, page_table, sel.reshape(-1), rel_bias, ds)
        return o.reshape(db, H_MB, ds, HEAD_D).transpose(0, 2, 1, 3).reshape(db * ds, WIDTH)

    y_s, kt_s, vt_s, s_s, c_s = layer(
        x_sample.reshape(db * ds, dm), db, ds, _state_to_rows(state_hgrn[l]), moba_s,
        cache_mem_k[l].reshape(db, n_mem, -1), cache_mem_v[l].reshape(db, n_mem, -1), state_conv[l])

    hd = lambda a, bb, tt: a.reshape(bb, H_MB, HEAD_D, tt).transpose(0, 3, 1, 2)[None]
    return (y_p.reshape(b, t, dm), y_s.reshape(db, ds, dm),
            hd(kt_p, b, t), hd(vt_p, b, t), hd(per_seq(kt_s), db, ds), hd(per_seq(vt_s), db, ds),
            s_p[None], s_s[None], c_p[None], c_s[None],
            mk_p.reshape(1, b, n_mem, H_MEM, MEM_HD), mv_p.reshape(1, b, n_mem, H_MEM, MEM_HD))
```

```python
import functools
import math

import numpy as np
import jax
import jax.numpy as jnp
from jax import lax
from jax.experimental import pallas as pl
from jax.experimental.pallas import tpu as pltpu

F32 = jnp.float32
BF16 = jnp.bfloat16
EPS = 1e-6

H_HG = 8
H_MB = 8
HEAD_D = 64
WIDTH = 512
MB_BLOCK = 256
MB_TOPK = 3
NUM_BUCKETS = 32
MAX_DISTANCE = 8192
H_MEM = 4
MEM_HD = 128
CONV_W = 3

LANES = 128
SUBLANES = 8
VMEM_LIMIT = 56 * 1024 * 1024

HG_SUB = 16
HG_GROUP = 128
HG_SEQS = 4
HG_TILE = 128


def _cparams(*sem):
    return pltpu.CompilerParams(dimension_semantics=sem, vmem_limit_bytes=VMEM_LIMIT)


def _const_spec(shape):
    nd = len(shape)
    return pl.BlockSpec(shape, lambda *_: (0,) * nd, pipeline_mode=pl.Buffered(1))


def _group_ones(width, group):
    i = np.arange(width) // group
    return (i[:, None] == i[None, :]).astype(np.float32)


def _split_dot(x, ones_bf16):
    hi = x.astype(BF16)
    lo = (x - hi.astype(F32)).astype(BF16)
    return (jnp.dot(hi, ones_bf16, preferred_element_type=F32)
            + jnp.dot(lo, ones_bf16, preferred_element_type=F32))


def _seg_rmsnorm(x, gain, ones_bf16, seg):
    ms = _split_dot(x * x, ones_bf16) * (1.0 / seg)
    return x * lax.rsqrt(ms + EPS) * gain


def _rmsnorm_rows(x, gain):
    ms = jnp.mean(x * x, axis=-1, keepdims=True)
    return x * lax.rsqrt(ms + EPS) * gain


def _inproj_kernel(x_ref, gmix_ref, w_ref, wvt_ref, lbl_ref, qn_ref, kn_ref, ones_ref,
                   qhg_ref, khg_ref, ihg_ref, logf_ref, g_ref, qmb_ref, kmb_ref, ksum_ref, kt_ref, vt_ref, vtb_ref):
    xb = _rmsnorm_rows(x_ref[...], gmix_ref[...]).astype(BF16)

    def proj(i):
        return jnp.dot(xb, w_ref[:, i * WIDTH:(i + 1) * WIDTH], preferred_element_type=F32)

    qhg_ref[...] = proj(0)
    f_logit = proj(1)
    logits = lbl_ref[...]
    e = jnp.exp(logits - jnp.max(logits, axis=0, keepdims=True))
    lb = e[0:1] / jnp.sum(e, axis=0, keepdims=True)
    logf_ref[...] = jnp.log(lb + (1.0 - lb) * jax.nn.sigmoid(f_logit))
    khg_ref[...] = (1.0 - lb) * jax.nn.sigmoid(-f_logit)
    ihg_ref[...] = proj(2)
    g_ref[...] = proj(3)
    ones = ones_ref[...]
    qmb_ref[...] = _seg_rmsnorm(proj(4), qn_ref[...], ones, HEAD_D)
    k = _seg_rmsnorm(proj(5), kn_ref[...], ones, HEAD_D)
    kmb_ref[...] = k.astype(BF16)
    for j in range(ksum_ref.shape[0]):
        rows = k.shape[0] // ksum_ref.shape[0]
        ksum_ref[j] = jnp.sum(k[j * rows:(j + 1) * rows], axis=0, keepdims=True)
    kt_ref[0] = k.T
    vt = lax.dot_general(wvt_ref[...], xb, (((1,), (1,)), ((), ())), preferred_element_type=F32)
    vt_ref[0] = vt
    vtb_ref[0] = vt.astype(BF16)


def _inproj(x, gmix, w_bf16, lb_logits, qn, kn, batch, tm):
    n, d = x.shape
    t = n // batch
    assert t % tm == 0
    tpb = t // tm
    grp = min(tm, MB_BLOCK)
    assert tm % grp == 0
    tok = pl.BlockSpec((tm, WIDTH), lambda i: (i, 0))
    tr = pl.BlockSpec((1, WIDTH, tm), lambda i: (i // tpb, 0, i % tpb))
    outs = ([jax.ShapeDtypeStruct((n, WIDTH), F32)] * 6 + [jax.ShapeDtypeStruct((n, WIDTH), BF16)]
            + [jax.ShapeDtypeStruct((n // grp, 1, WIDTH), F32)]
            + [jax.ShapeDtypeStruct((batch, WIDTH, t), F32)] * 2 + [jax.ShapeDtypeStruct((batch, WIDTH, t), BF16)])
    ones = jnp.asarray(_group_ones(WIDTH, HEAD_D), BF16)
    w_main, wvt = w_bf16[:, :6 * WIDTH], w_bf16[:, 6 * WIDTH:].T
    return pl.pallas_call(
        _inproj_kernel,
        grid=(n // tm,),
        in_specs=[pl.BlockSpec((tm, d), lambda i: (i, 0)), _const_spec((1, d)), _const_spec(w_main.shape),
                  _const_spec(wvt.shape), _const_spec(lb_logits.shape), _const_spec((1, WIDTH)),
                  _const_spec((1, WIDTH)), _const_spec((WIDTH, WIDTH))],
        out_specs=[tok] * 7 + [pl.BlockSpec((tm // grp, 1, WIDTH), lambda i: (i, 0, 0)), tr, tr, tr],
        out_shape=outs,
        compiler_params=_cparams("parallel"),
        name="inproj",
    )(x, gmix, w_main, wvt, lb_logits, qn, kn, ones)


def _hgrn_kernel(q_ref, k_ref, v_ref, lf_ref, g_ref, s0_ref, gn_ref, ones_ref, gmask_ref,
                 o_ref, sfin_ref, st_scr, a_scr, o_scr, *, tc, nseq, sub):
    c = pl.program_id(1)
    ng = WIDTH // HG_GROUP
    hpg = HG_GROUP // HEAD_D
    gmask = gmask_ref[...]

    @pl.when(c == 0)
    def _():
        for bi in range(nseq):
            s0 = s0_ref[bi]
            for gi in range(ng):
                blk = s0[:, gi * HG_GROUP:(gi + 1) * HG_GROUP]
                st_scr[bi * ng + gi] = jnp.concatenate([blk] * hpg, axis=0) * gmask

    row = lax.broadcasted_iota(jnp.int32, (tc, WIDTH), 0) % sub
    for bi in range(nseq):
        a = lf_ref[bi]
        sh = 1
        while sh < sub:
            a = a + jnp.where(row >= sh, pltpu.roll(a, sh, 0), 0.0)
            sh *= 2
        a_scr[bi] = a

    ones = ones_ref[...]
    trow = lax.broadcasted_iota(jnp.int32, (sub, WIDTH), 0)

    def step(n, carry):
        off = pl.multiple_of(n * sub, sub)
        for bi in range(nseq):
            q = q_ref[bi, pl.ds(off, sub), :]
            k = k_ref[bi, pl.ds(off, sub), :]
            v = v_ref[bi, pl.ds(off, sub), :]
            al = a_scr[bi, pl.ds(off, sub), :]
            a_end = al[sub - 1:sub, :]
            qe = (q * jnp.exp(al)).astype(BF16)
            kd = (k * jnp.exp(a_end - al)).astype(BF16)
            dec = jnp.exp(a_end)
            vb = v.astype(BF16)
            parts = []
            for s in range(sub):
                a_s = a_scr[bi, pl.ds(off + s, 1), :]
                k_s = k_ref[bi, pl.ds(off + s, 1), :]
                e_s = q * k_s * jnp.exp(al - a_s)
                parts.append(jnp.where(trow >= s, e_s, 0.0))
            ecat = jnp.concatenate(parts, axis=0).astype(BF16)
            outs = []
            for gi in range(ng):
                sl = slice(gi * HG_GROUP, (gi + 1) * HG_GROUP)
                st = st_scr[bi * ng + gi]
                o_g = lax.dot_general(qe[:, sl], st.astype(BF16), (((1,), (1,)), ((), ())),
                                      preferred_element_type=F32)
                ag = jnp.dot(ecat[:, sl], ones, preferred_element_type=F32)
                for s in range(sub):
                    o_g = o_g + ag[s * sub:(s + 1) * sub, :] * v[s:s + 1, sl]
                outs.append(o_g)
                upd = lax.dot_general(vb[:, sl], kd[:, sl], (((0,), (0,)), ((), ())),
                                      preferred_element_type=F32)
                st_scr[bi * ng + gi] = st * dec[:, sl] + upd * gmask
            o_scr[bi, pl.ds(off, sub), :] = jnp.concatenate(outs, axis=1)
        return carry

    lax.fori_loop(0, tc // sub, step, 0)

    for bi in range(nseq):
        o = o_scr[bi]
        ms = jnp.concatenate(
            [_split_dot(o[:, gi * HG_GROUP:(gi + 1) * HG_GROUP] ** 2, ones) for gi in range(ng)], axis=1)
        g = g_ref[bi]
        o_ref[bi] = o * lax.rsqrt(ms * (1.0 / HEAD_D) + EPS) * gn_ref[...] * (g * jax.nn.sigmoid(g))

    @pl.when(c == pl.num_programs(1) - 1)
    def _():
        for bi in range(nseq):
            cols = []
            for gi in range(ng):
                st = st_scr[bi * ng + gi]
                acc = st[0:HEAD_D, :]
                for hh in range(1, hpg):
                    acc = acc + st[hh * HEAD_D:(hh + 1) * HEAD_D, :]
                cols.append(acc)
            sfin_ref[bi] = jnp.concatenate(cols, axis=1)


def _hgrn(q, k, v, logf, g, s0t, gn, batch, tc):
    n = q.shape[0]
    t = n // batch
    nseq = min(HG_SEQS, batch)
    sub = min(HG_SUB, tc)
    assert t % tc == 0 and tc % sub == 0 and batch % nseq == 0
    ng = WIDTH // HG_GROUP
    seq = lambda a: a.reshape(batch, t, WIDTH)
    tok = pl.BlockSpec((nseq, tc, WIDTH), lambda b, c: (b, c, 0))
    st_spec = pl.BlockSpec((nseq, HEAD_D, WIDTH), lambda b, c: (b, 0, 0))
    gm = _group_ones(HG_GROUP, HEAD_D)
    kern = functools.partial(_hgrn_kernel, tc=tc, nseq=nseq, sub=sub)
    o, st = pl.pallas_call(
        kern,
        grid=(batch // nseq, t // tc),
        in_specs=[tok] * 5 + [st_spec, _const_spec((1, WIDTH)), _const_spec((HG_GROUP, HG_GROUP)),
                              _const_spec((HG_GROUP, HG_GROUP))],
        out_specs=[tok, st_spec],
        out_shape=[jax.ShapeDtypeStruct((batch, t, WIDTH), F32), jax.ShapeDtypeStruct((batch, HEAD_D, WIDTH), F32)],
        scratch_shapes=[pltpu.VMEM((nseq * ng, HG_GROUP, HG_GROUP), F32),
                        pltpu.VMEM((nseq, tc, WIDTH), F32), pltpu.VMEM((nseq, tc, WIDTH), F32)],
        compiler_params=_cparams("parallel", "arbitrary"),
        name="hgrn",
    )(seq(q), seq(k), seq(v), seq(logf), seq(g), s0t, gn, jnp.asarray(gm, BF16), jnp.asarray(gm, F32))
    return o.reshape(n, WIDTH), st


NEG = -1e30


def _bucket_table(max_dist):
    max_exact = NUM_BUCKETS // 2
    d = np.arange(max_dist)
    nf = np.maximum(d, 1).astype(np.float32)
    large = max_exact + (np.log(nf / max_exact) / math.log(MAX_DISTANCE / max_exact)
                         * (NUM_BUCKETS - max_exact)).astype(np.int32)
    bucket = np.where(d < max_exact, d, np.minimum(large, NUM_BUCKETS - 1))
    assert np.all(np.diff(bucket) >= 0) and bucket[-1] == NUM_BUCKETS - 1
    first = [int(np.argmax(bucket >= kk)) for kk in range(NUM_BUCKETS)]
    return bucket, first


_BUCKET, _BUCKET_FIRST = _bucket_table(2 * MAX_DISTANCE)


def _bias_of_distance(d, rb_ref, h):
    acc = jnp.full(d.shape, rb_ref[0, h], F32)
    for kk in range(1, NUM_BUCKETS):
        acc = jnp.where(d >= _BUCKET_FIRST[kk], rb_ref[kk, h], acc)
    return acc


def _far_block_table(nb):
    big = 4 * MB_BLOCK
    tab = np.zeros((nb, 5), np.int32)
    for delta in range(nb):
        lo, hi = max(delta * MB_BLOCK - (MB_BLOCK - 1), 0), delta * MB_BLOCK + (MB_BLOCK - 1)
        b0 = int(_BUCKET[lo])
        ks = [kk for kk in range(b0 + 1, NUM_BUCKETS) if _BUCKET_FIRST[kk] <= hi]
        assert delta < 2 or len(ks) <= 2
        cs = [_BUCKET_FIRST[kk] - delta * MB_BLOCK for kk in ks[:2]] + [big, big]
        tab[delta] = [b0, min(b0 + 1, NUM_BUCKETS - 1), min(b0 + 2, NUM_BUCKETS - 1), cs[0], cs[1]]
    return tab


def _top_blocks(gate, n_valid, n_top, axis=0):
    nb = gate.shape[axis]
    j = lax.broadcasted_iota(jnp.int32, gate.shape, axis)
    g = jnp.where(j < n_valid, gate, -jnp.inf)
    sel = jnp.zeros(gate.shape, F32)
    idxs, oks = [], []
    for _ in range(n_top):
        mx = jnp.max(g, axis=axis, keepdims=True)
        idx = jnp.min(jnp.where(g == mx, j, nb), axis=axis, keepdims=True)
        ok = jnp.where(mx > -jnp.inf, 1.0, 0.0)
        pick = jnp.where(j == idx, ok, 0.0) > 0.0
        sel = jnp.where(pick, 1.0, sel)
        g = jnp.where(pick, -jnp.inf, g)
        idxs.append(idx)
        oks.append(ok)
    return idxs, oks, sel


def _pair_masks(shape):
    lane = lax.broadcasted_iota(jnp.int32, shape, len(shape) - 1)
    return [lane < HEAD_D, lane >= HEAD_D]


def _moba_prompt_kernel(qi_ref, kj_ref, tab_ref, q_ref, k_ref, vt_ref, vtp_ref, ksum_ref, rb_ref, o_ref,
                        qm_scr, sel_scr, m_scr, l_scr, acc_scr, near_scr, alpha0_scr, p0_scr, alpha1_scr, p1_scr):
    p = pl.program_id(0)
    qi = qi_ref[p]
    kj = kj_ref[p]
    delta = qi - kj
    blk = MB_BLOCK
    nseq = q_ref.shape[0]
    dts = (lax.broadcasted_iota(jnp.int32, (blk, blk), 1)
           - lax.broadcasted_iota(jnp.int32, (blk, blk), 0))
    slots = ((alpha0_scr, p0_scr), (alpha1_scr, p1_scr))

    @pl.when(p == 0)
    def _():
        for h in range(H_MB):
            near_scr[h] = jnp.where(dts >= 0, _bias_of_distance(dts, rb_ref, h), NEG)
            near_scr[H_MB + h] = _bias_of_distance(dts + blk, rb_ref, h)

    def sequence(b, carry):
        hb = b * H_MB

        @pl.when(kj == 0)
        def _():
            q = q_ref[b]
            kmean = ksum_ref[b] * (1.0 / blk)
            masks = _pair_masks((blk, LANES))
            for h in range(H_MB):
                m_scr[hb + h] = jnp.full((1, blk), NEG, F32)
                l_scr[hb + h] = jnp.zeros((1, blk), F32)
                acc_scr[hb + h] = jnp.zeros((HEAD_D, blk), F32)
                p1_scr[hb + h] = jnp.zeros((blk, blk), BF16)
                alpha1_scr[hb + h] = jnp.ones((1, blk), F32)
                sl = slice((h // 2) * LANES, (h // 2 + 1) * LANES)
                qh = jnp.where(masks[h % 2], q[:, sl], 0.0)
                qm_scr[hb + h] = (qh * (HEAD_D ** -0.5)).T.astype(BF16)
                gate = lax.dot_general(kmean[:, sl], qh, (((1,), (1,)), ((), ())),
                                       precision=lax.Precision.HIGHEST, preferred_element_type=F32)
                _, _, sel = _top_blocks(gate, qi, MB_TOPK)
                sel_scr[hb + h] = jnp.where(sel > 0.0, 0.0, NEG)

        def attend(bias_of, par, own_block=False):
            (a_cur, p_cur), (a_prv, p_prv) = slots[par], slots[1 - par]
            for h in range(H_MB):
                sl = slice((h // 2) * LANES, (h // 2 + 1) * LANES)
                s_all = jnp.dot(k_ref[b, :, sl], qm_scr[hb + h], preferred_element_type=F32)
                for qh in range(blk // LANES):
                    ql = slice(qh * LANES, (qh + 1) * LANES)
                    s = s_all[:, ql] + bias_of(h, qh)
                    m_old = m_scr[hb + h, :, ql]
                    m_new = jnp.maximum(m_old, jnp.max(s, axis=0, keepdims=True))
                    alpha = jnp.exp(m_old - m_new)
                    pexp = jnp.exp(s - m_new)
                    l_scr[hb + h, :, ql] = alpha * l_scr[hb + h, :, ql] + jnp.sum(pexp, axis=0, keepdims=True)
                    m_scr[hb + h, :, ql] = m_new
                    a_cur[hb + h, :, ql] = alpha
                    p_cur[hb + h, :, ql] = pexp.astype(BF16)
            for a_scr, p_scr, v_ref in ((a_prv, p_prv, vtp_ref),) + (((a_cur, p_cur, vt_ref),) if own_block else ()):
                for h in range(H_MB):
                    hs = slice(h * HEAD_D, (h + 1) * HEAD_D)
                    pv = jnp.dot(v_ref[b, hs, :], p_scr[hb + h], preferred_element_type=F32)
                    acc_scr[hb + h] = a_scr[hb + h] * acc_scr[hb + h] + pv

        def for_each_parity(cond, body):
            for par in range(2):
                pl.when(cond & (kj % 2 == par))(functools.partial(body, par))

        def selrow(h, qh):
            return sel_scr[hb + h, pl.ds(kj, 1), :][:, qh * LANES:(qh + 1) * LANES]

        has_steps = tab_ref[delta, 3] < 2 * blk

        for_each_parity(delta == 0, lambda par: attend(
            lambda h, qh: near_scr[h, :, qh * LANES:(qh + 1) * LANES], par, own_block=True))

        for_each_parity(delta == 1, lambda par: attend(
            lambda h, qh: near_scr[H_MB + h, :, qh * LANES:(qh + 1) * LANES] + selrow(h, qh), par))

        for_each_parity((delta >= 2) & jnp.logical_not(has_steps), lambda par: attend(
            lambda h, qh: rb_ref[tab_ref[delta, 0], h] + selrow(h, qh), par))

        def stepped(par):
            dts0 = dts[:, :LANES]

            def bias_of(h, qh):
                sr = selrow(h, qh)
                r0 = rb_ref[tab_ref[delta, 0], h] + sr
                r1 = rb_ref[tab_ref[delta, 1], h] + sr
                r2 = rb_ref[tab_ref[delta, 2], h] + sr
                m1 = dts0 >= tab_ref[delta, 3] - qh * LANES
                m2 = dts0 >= tab_ref[delta, 4] - qh * LANES
                return jnp.where(m2, r2, jnp.where(m1, r1, r0))

            attend(bias_of, par)

        for_each_parity((delta >= 2) & has_steps, stepped)

        @pl.when(delta == 0)
        def _():
            o_ref[b] = jnp.concatenate([acc_scr[hb + h] / l_scr[hb + h] for h in range(H_MB)], axis=0).T

        return carry

    lax.fori_loop(0, nseq, sequence, 0)


def _moba_prompt(q, k, vt, ksum, rel_bias, batch):
    n = q.shape[0]
    t = n // batch
    assert t % MB_BLOCK == 0
    nb = t // MB_BLOCK
    qi = np.array([i for i in range(nb) for _ in range(i + 1)], np.int32)
    kj = np.array([j for i in range(nb) for j in range(i + 1)], np.int32)
    tab = _far_block_table(nb)
    blk = MB_BLOCK
    rows = batch * H_MB
    grid_spec = pltpu.PrefetchScalarGridSpec(
        num_scalar_prefetch=3,
        grid=(len(qi),),
        in_specs=[
            pl.BlockSpec((batch, blk, WIDTH), lambda p, qi, kj, tab: (0, qi[p], 0)),
            pl.BlockSpec((batch, blk, WIDTH), lambda p, qi, kj, tab: (0, kj[p], 0)),
            pl.BlockSpec((batch, WIDTH, blk), lambda p, qi, kj, tab: (0, 0, kj[p])),
            pl.BlockSpec((batch, WIDTH, blk), lambda p, qi, kj, tab: (0, 0, jnp.maximum(kj[p] - 1, 0))),
            pl.BlockSpec((batch, nb, WIDTH), lambda p, qi, kj, tab: (0, 0, 0)),
            pl.BlockSpec(memory_space=pltpu.SMEM),
        ],
        out_specs=pl.BlockSpec((batch, blk, WIDTH), lambda p, qi, kj, tab: (0, qi[p], 0)),
        scratch_shapes=[pltpu.VMEM((rows, LANES, blk), BF16), pltpu.VMEM((rows, nb, blk), F32),
                        pltpu.VMEM((rows, 1, blk), F32), pltpu.VMEM((rows, 1, blk), F32),
                        pltpu.VMEM((rows, HEAD_D, blk), F32), pltpu.VMEM((2 * H_MB, blk, blk), F32),
                        pltpu.VMEM((rows, 1, blk), F32), pltpu.VMEM((rows, blk, blk), BF16),
                        pltpu.VMEM((rows, 1, blk), F32), pltpu.VMEM((rows, blk, blk), BF16)],
    )
    seq = lambda a: a.reshape(batch, t, WIDTH)
    return pl.pallas_call(
        _moba_prompt_kernel,
        grid_spec=grid_spec,
        out_shape=jax.ShapeDtypeStruct((batch, t, WIDTH), F32),
        compiler_params=_cparams("arbitrary"),
        name="moba_prompt",
    )(jnp.asarray(qi), jnp.asarray(kj), jnp.asarray(tab), seq(q), seq(k), vt, vt, ksum, rel_bias).reshape(n, WIDTH)


SAMPLE_PAGES_PER_STEP = 32


def _page_ksum_kernel(pt_ref, *refs, ppb):
    pages, out_ref = refs[:-1], refs[-1]
    j = pl.program_id(1)
    nblk = out_ref.shape[2]
    lane = lax.broadcasted_iota(jnp.int32, (WIDTH, nblk), 1)

    @pl.when(j == 0)
    def _():
        out_ref[...] = jnp.zeros(out_ref.shape, F32)

    acc = out_ref[0]
    for i in range(0, len(pages), ppb):
        blk = pages[i][0].reshape(WIDTH, pages[i].shape[-1])
        for e in range(1, ppb):
            blk = blk + pages[i + e][0].reshape(blk.shape)
        col = jnp.sum(blk, axis=1, keepdims=True)
        acc = jnp.where(lane == j * (len(pages) // ppb) + i // ppb, col, acc)
    out_ref[0] = acc


def _page_ksum(pool_t, page_table):
    db, n_pages = page_table.shape
    page = pool_t.shape[-1]
    ppb = MB_BLOCK // page
    pps = min(SAMPLE_PAGES_PER_STEP, n_pages)
    assert MB_BLOCK % page == 0 and n_pages % pps == 0 and pps % ppb == 0
    nblk = n_pages // ppb
    specs = [pl.BlockSpec((1,) + pool_t.shape[1:], (lambda b, j, pt, i=i: (pt[b, j * pps + i], 0, 0, 0)))
             for i in range(pps)]
    grid_spec = pltpu.PrefetchScalarGridSpec(
        num_scalar_prefetch=1, grid=(db, n_pages // pps), in_specs=specs,
        out_specs=pl.BlockSpec((1, WIDTH, nblk), lambda b, j, pt: (b, 0, 0)))
    return pl.pallas_call(
        functools.partial(_page_ksum_kernel, ppb=ppb), grid_spec=grid_spec,
        out_shape=jax.ShapeDtypeStruct((db, WIDTH, nblk), F32),
        compiler_params=_cparams("parallel", "arbitrary"), name="page_ksum",
    )(page_table, *([pool_t] * pps))


def _sample_select_kernel(q_ref, ksum_ref, idx_ref):
    q = q_ref[...]
    kmean_t = ksum_ref[0] * (1.0 / MB_BLOCK)
    nblk = kmean_t.shape[1]
    cols = []
    for h in range(H_MB):
        hs = slice(h * HEAD_D, (h + 1) * HEAD_D)
        gate = jnp.dot(q[:, hs], kmean_t[hs, :], precision=lax.Precision.HIGHEST,
                       preferred_element_type=F32)
        idxs, _, _ = _top_blocks(gate, nblk, MB_TOPK, axis=1)
        cols.extend(idxs)
    idx_ref[0] = jnp.concatenate(cols, axis=1)


def _sample_select(q, ksum_t, ds):
    db, _, nblk = ksum_t.shape
    assert nblk >= MB_TOPK
    return pl.pallas_call(
        _sample_select_kernel, grid=(db,),
        in_specs=[pl.BlockSpec((ds, WIDTH), lambda b: (b, 0)), pl.BlockSpec((1, WIDTH, nblk), lambda b: (b, 0, 0))],
        out_specs=pl.BlockSpec((1, ds, H_MB * MB_TOPK), lambda b: (b, 0, 0)),
        out_shape=jax.ShapeDtypeStruct((db, ds, H_MB * MB_TOPK), jnp.int32),
        compiler_params=_cparams("parallel"), name="sample_select",
    )(q, ksum_t)


def _sample_attend_kernel(pt_ref, sel_ref, q_ref, kn_ref, vn_ref, rb_ref, kpool_ref, vpool_ref, o_ref,
                          kbuf, vbuf, sems, *, ds, page, past_len):
    ppb = MB_BLOCK // page
    ntile = MB_TOPK * ppb
    step = pl.program_id(0)
    nstep = pl.num_programs(0)

    def copies(st, slot):
        b, h = st // H_MB, st % H_MB
        out = []
        for q in range(ds):
            for r in range(MB_TOPK):
                blk = sel_ref[((b * ds + q) * H_MB + h) * MB_TOPK + r]
                for e in range(ppb):
                    pg = pt_ref[b, blk * ppb + e]
                    dst = pl.ds((r * ppb + e) * page, page)
                    out.append(pltpu.make_async_copy(kpool_ref.at[pg, h], kbuf.at[slot, q, :, dst], sems.at[0, slot]))
                    out.append(pltpu.make_async_copy(vpool_ref.at[pg, h], vbuf.at[slot, q, :, dst], sems.at[1, slot]))
        return out

    @pl.when(step == 0)
    def _():
        for cp in copies(step, 0):
            cp.start()

    @pl.when(step + 1 < nstep)
    def _():
        for cp in copies(step + 1, (step + 1) % 2):
            cp.start()

    slot = step % 2
    for qq in range(ds):
        for tile in range(ntile):
            dst = pl.ds(tile * page, page)
            pltpu.make_async_copy(kpool_ref.at[0, 0], kbuf.at[slot, qq, :, dst], sems.at[0, slot]).wait()
            pltpu.make_async_copy(vpool_ref.at[0, 0], vbuf.at[slot, qq, :, dst], sems.at[1, slot]).wait()

    b, h = step // H_MB, step % H_MB
    q = (q_ref[0] * (HEAD_D ** -0.5)).astype(BF16)
    rowi = lax.broadcasted_iota(jnp.int32, (ds, ntile * page), 0)
    s_sel = jnp.zeros((ds, ntile * page), F32)
    kpos = jnp.zeros((ds, ntile * page), jnp.int32)
    lane = lax.broadcasted_iota(jnp.int32, (ds, page), 1)
    for qq in range(ds):
        res = jnp.dot(q, kbuf[slot, qq].astype(BF16), preferred_element_type=F32)
        s_sel = jnp.where(rowi == qq, res, s_sel)
        pieces = []
        for r in range(MB_TOPK):
            blk = sel_ref[((b * ds + qq) * H_MB + h) * MB_TOPK + r]
            for e in range(ppb):
                pieces.append(blk * MB_BLOCK + e * page + lane)
        kpos = jnp.where(rowi == qq, jnp.concatenate(pieces, axis=1), kpos)
    q_pos = past_len + lax.broadcasted_iota(jnp.int32, (ds, 1), 0)
    s_sel = s_sel + _bias_of_distance(q_pos - kpos, rb_ref, h)
    own = lax.broadcasted_iota(jnp.int32, (ds, ds), 1)
    qrow = lax.broadcasted_iota(jnp.int32, (ds, ds), 0)
    s_own = jnp.dot(q, kn_ref[0].astype(BF16), preferred_element_type=F32) \
        + _bias_of_distance(qrow - own, rb_ref, h)
    s_own = jnp.where(own <= qrow, s_own, NEG)
    m = jnp.maximum(jnp.max(s_sel, axis=1, keepdims=True), jnp.max(s_own, axis=1, keepdims=True))
    p_sel = jnp.exp(s_sel - m)
    p_own = jnp.exp(s_own - m)
    den = jnp.sum(p_sel, axis=1, keepdims=True) + jnp.sum(p_own, axis=1, keepdims=True)
    nt_dims = (((1,), (1,)), ((), ()))
    o = lax.dot_general(p_own.astype(BF16), vn_ref[0].astype(BF16), nt_dims, preferred_element_type=F32)
    rowo = lax.broadcasted_iota(jnp.int32, (ds, HEAD_D), 0)
    p_sel_b = p_sel.astype(BF16)
    for qq in range(ds):
        res = lax.dot_general(p_sel_b, vbuf[slot, qq].astype(BF16), nt_dims, preferred_element_type=F32)
        o = o + jnp.where(rowo == qq, res, 0.0)
    o_ref[0] = o / den


def _sample_attend(q, kt_new, vt_new, pool_kt, pool_vt, page_table, sel_flat, rel_bias, ds):
    db, n_pages = page_table.shape
    page = pool_kt.shape[-1]
    ppb = MB_BLOCK // page
    past_len = n_pages * page
    assert past_len % MB_BLOCK == 0
    ntile = MB_TOPK * ppb
    qh = q.reshape(db, ds, H_MB, HEAD_D).transpose(0, 2, 1, 3).reshape(db * H_MB, ds, HEAD_D)
    per_head = lambda shape: pl.BlockSpec((1,) + shape, lambda s, pt, sel: (s, 0, 0))
    grid_spec = pltpu.PrefetchScalarGridSpec(
        num_scalar_prefetch=2, grid=(db * H_MB,),
        in_specs=[per_head((ds, HEAD_D)), per_head((HEAD_D, ds)), per_head((HEAD_D, ds)),
                  pl.BlockSpec(memory_space=pltpu.SMEM), pl.BlockSpec(memory_space=pl.ANY),
                  pl.BlockSpec(memory_space=pl.ANY)],
        out_specs=per_head((ds, HEAD_D)),
        scratch_shapes=[pltpu.VMEM((2, ds, HEAD_D, ntile * page), F32), pltpu.VMEM((2, ds, HEAD_D, ntile * page), F32),
                        pltpu.SemaphoreType.DMA((2, 2))])
    kern = functools.partial(_sample_attend_kernel, ds=ds, page=page, past_len=past_len)
    return pl.pallas_call(
        kern, grid_spec=grid_spec, out_shape=jax.ShapeDtypeStruct((db * H_MB, ds, HEAD_D), F32),
        compiler_params=_cparams("arbitrary"), name="sample_attend",
    )(page_table, sel_flat, qh, kt_new, vt_new, rel_bias, pool_kt, pool_vt)


def _memkv_kernel(m_ref, g_ref, w_ref, kn_ref, ones_ref, k_ref, v_ref):
    xb = _rmsnorm_rows(m_ref[...], g_ref[...]).astype(BF16)
    wk = w_ref.shape[1] // 2
    k = jnp.dot(xb, w_ref[:, :wk], preferred_element_type=F32)
    k_ref[...] = _seg_rmsnorm(k, kn_ref[...], ones_ref[...], MEM_HD)
    v_ref[...] = jnp.dot(xb, w_ref[:, wk:], preferred_element_type=F32)


def _memkv(mem, g, w_bf16, kn, tm):
    n, d = mem.shape
    wk = w_bf16.shape[1] // 2
    out = pl.BlockSpec((tm, wk), lambda i: (i, 0))
    return pl.pallas_call(
        _memkv_kernel, grid=(n // tm,),
        in_specs=[pl.BlockSpec((tm, d), lambda i: (i, 0)), _const_spec((1, d)), _const_spec(w_bf16.shape),
                  _const_spec((1, wk)), _const_spec((wk, wk))],
        out_specs=[out, out], out_shape=[jax.ShapeDtypeStruct((n, wk), F32)] * 2,
        compiler_params=_cparams("parallel"), name="mem_kv",
    )(mem, g, w_bf16, kn, jnp.asarray(_group_ones(wk, MEM_HD), BF16))


def _mixmem_kernel(x_ref, hg_ref, omb_ref, wout_ref, gmem_ref, wq_ref, qn_ref, ones_ref, mk_ref, mv_ref, wo_ref,
                   h_ref, *, nseq):
    mix = (jnp.dot(hg_ref[...].astype(BF16), wout_ref[:WIDTH, :], preferred_element_type=F32)
           + jnp.dot(omb_ref[...].astype(BF16), wout_ref[WIDTH:, :], preferred_element_type=F32))
    h1 = x_ref[...] + mix
    hn = _rmsnorm_rows(h1, gmem_ref[...]).astype(BF16)
    q = _seg_rmsnorm(jnp.dot(hn, wq_ref[...], preferred_element_type=F32), qn_ref[...], ones_ref[...], MEM_HD)
    tm = q.shape[0] // nseq
    per_seq = []
    for i in range(nseq):
        qb = q[i * tm:(i + 1) * tm].astype(BF16)
        mk = mk_ref[i].astype(BF16)
        mv = mv_ref[i].astype(BF16)
        outs = []
        for h in range(H_MEM):
            sl = slice(h * MEM_HD, (h + 1) * MEM_HD)
            s = lax.dot_general(qb[:, sl], mk[:, sl], (((1,), (1,)), ((), ())),
                                preferred_element_type=F32) * (MEM_HD ** -0.5)
            e = jnp.exp(s - jnp.max(s, axis=-1, keepdims=True))
            p = e / jnp.sum(e, axis=-1, keepdims=True)
            outs.append(jnp.dot(p.astype(BF16), mv[:, sl], preferred_element_type=F32))
        per_seq.append(jnp.concatenate(outs, axis=1))
    o = jnp.concatenate(per_seq, axis=0).astype(BF16)
    h_ref[...] = h1 + jnp.dot(o, wo_ref[...], preferred_element_type=F32)


def _tail_seqs(batch, nt):
    return math.gcd(batch, SHORT_SEQS) if nt == 1 else 1


def _mixmem(x, hg, omb, wout, gmem, wq, qn, mk, mv, wo, batch, tm):
    n, d = x.shape
    t = n // batch
    assert t % tm == 0
    nt = t // tm
    nseq = _tail_seqs(batch, nt)
    n_mem, wm = mk.shape[1:]
    tok = lambda w: pl.BlockSpec((nseq * tm, w), lambda b, i: (b * nt + i, 0))
    mem = pl.BlockSpec((nseq, n_mem, wm), lambda b, i: (b, 0, 0))
    return pl.pallas_call(
        functools.partial(_mixmem_kernel, nseq=nseq), grid=(batch // nseq, nt),
        in_specs=[tok(d), tok(WIDTH), tok(WIDTH), _const_spec(wout.shape), _const_spec((1, d)),
                  _const_spec(wq.shape), _const_spec((1, wm)), _const_spec((wm, wm)), mem, mem,
                  _const_spec(wo.shape)],
        out_specs=tok(d), out_shape=jax.ShapeDtypeStruct((n, d), F32),
        compiler_params=_cparams("parallel", "parallel"), name="mix_mem",
    )(x, hg, omb, wout, gmem, wq, qn, jnp.asarray(_group_ones(wm, MEM_HD), BF16), mk, mv, wo)


FFN_CHUNKS = 2
SHORT_SEQS = 8


def _ffn_kernel(h_ref, g_ref, wup_ref, cw_ref, cb_ref, wdn_ref, prev_ref, y_ref, cst_ref, carry_scr, *, dff, nseq):
    c = pl.program_id(1)

    @pl.when(c == 0)
    def _():
        carry_scr[...] = prev_ref[...]

    h = h_ref[...]
    hn = _rmsnorm_rows(h, g_ref[...]).astype(BF16)
    tm = h.shape[0] // nseq
    fc = dff // FFN_CHUNKS
    row = lax.broadcasted_iota(jnp.int32, (nseq * tm, fc), 0) % tm
    acc = h
    for ci in range(FFN_CHUNKS):
        sl = slice(ci * fc, (ci + 1) * fc)
        u = jnp.dot(hn, wup_ref[:, sl], preferred_element_type=F32)
        v = jnp.dot(hn, wup_ref[:, dff + ci * fc:dff + (ci + 1) * fc], preferred_element_type=F32)

        def carried(r):
            if nseq == 1:
                return carry_scr[0, r:r + 1, sl]
            return jnp.concatenate([jnp.broadcast_to(carry_scr[i, r:r + 1, sl], (tm, fc)) for i in range(nseq)], axis=0)

        u1 = jnp.where(row == 0, carried(1), pltpu.roll(u, 1, 0))
        u2 = jnp.where(row == 0, carried(0), jnp.where(row == 1, carried(1), pltpu.roll(u, 2, 0)))
        cw = cw_ref[:, sl]
        conv = cb_ref[:, sl] + u2 * cw[0:1] + u1 * cw[1:2] + u * cw[2:3]
        act = 0.5 * conv * (1.0 + lax.erf(conv * (2.0 ** -0.5))) * v
        acc = acc + jnp.dot(act.astype(BF16), wdn_ref[sl, :], preferred_element_type=F32)
        for i in range(nseq):
            last = u[(i + 1) * tm - (CONV_W - 1):(i + 1) * tm]
            carry_scr[i, :, sl] = last
            cst_ref[i, :, sl] = last
    y_ref[...] = acc


def _ffn(h, g, wup, cw, cb, wdn, prev, batch, tm):
    n, d = h.shape
    t = n // batch
    assert t % tm == 0 and tm >= CONV_W - 1
    nt = t // tm
    nseq = _tail_seqs(batch, nt)
    dff = wdn.shape[0]
    assert dff % (FFN_CHUNKS * LANES) == 0
    tok = pl.BlockSpec((nseq * tm, d), lambda b, i: (b * nt + i, 0))
    st = pl.BlockSpec((nseq, CONV_W - 1, dff), lambda b, i: (b, 0, 0))
    return pl.pallas_call(
        functools.partial(_ffn_kernel, dff=dff, nseq=nseq), grid=(batch // nseq, nt),
        in_specs=[tok, _const_spec((1, d)), _const_spec(wup.shape), _const_spec(cw.shape), _const_spec((1, dff)),
                  _const_spec(wdn.shape), st],
        out_specs=[tok, st],
        out_shape=[jax.ShapeDtypeStruct((n, d), F32), jax.ShapeDtypeStruct((batch, CONV_W - 1, dff), F32)],
        scratch_shapes=[pltpu.VMEM((nseq, CONV_W - 1, dff), F32)],
        compiler_params=_cparams("parallel", "arbitrary"), name="conv_ffn",
    )(h, g, wup, cw, cb, wdn, prev)


PROMPT_TILE = 512


def _state_to_rows(s):
    b = s.shape[0]
    return s.transpose(0, 3, 1, 2).reshape(b, HEAD_D, WIDTH)


def _rows_to_state(st):
    b = st.shape[0]
    return st.reshape(b, HEAD_D, H_HG, HEAD_D).transpose(0, 2, 3, 1)


def kernel(x_prompt, x_sample, cache_k, cache_v, page_table, state_hgrn, state_conv, cache_mem_k, cache_mem_v, mem_prompt, norm_mix, w_in, hg_lb_logits, hg_out_norm, mb_q_norm, mb_k_norm, rel_bias, w_out, norm_mem, norm_mem_src, w_mem_q, w_mem_kv, mem_q_norm, mem_k_norm, w_mem_o, norm_ffn, w_up, conv_w, conv_b, w_down):
    assert norm_mix.shape[0] == 1, "one layer"
    l = 0
    b, t, dm = x_prompt.shape
    db, ds, _ = x_sample.shape
    n_mem = mem_prompt.shape[1]
    dff = w_down.shape[1]
    row = lambda a: a[None]
    w_in_b, w_out_b = w_in[l].astype(BF16), w_out[l].astype(BF16)
    w_q_b, w_kv_b, w_o_b = w_mem_q[l].astype(BF16), w_mem_kv[l].astype(BF16), w_mem_o[l].astype(BF16)
    w_up_b, w_dn_b = w_up[l].astype(BF16), w_down[l].astype(BF16)
    qn, kn = row(jnp.tile(mb_q_norm[l], H_MB)), row(jnp.tile(mb_k_norm[l], H_MB))
    gn = row(jnp.tile(hg_out_norm[l], H_HG))
    mqn, mkn = row(jnp.tile(mem_q_norm[l], H_MEM)), row(jnp.tile(mem_k_norm[l], H_MEM))

    def layer(x, batch, tile, s0t, moba, mk, mv, prev):
        n = x.shape[0]
        tm = min(PROMPT_TILE, n)
        slabs = batch if (n // batch) % tm == 0 else 1
        qhg, khg, ihg, logf, g, qmb, kmb, ksum, kt, vt, vtb = _inproj(
            x, row(norm_mix[l]), w_in_b, hg_lb_logits, qn, kn, slabs, tm)
        hg, st = _hgrn(qhg, khg, ihg, logf, g, s0t, gn, batch, min(tile, HG_TILE))
        omb = moba(qmb, kmb, ksum, kt, vt, vtb)
        h = _mixmem(x, hg, omb, w_out_b, row(norm_mem[l]), w_q_b, mqn, mk, mv, w_o_b, batch, tile)
        y, cst = _ffn(h, row(norm_ffn[l]), w_up_b, conv_w[l], row(conv_b[l]), w_dn_b, prev, batch, tile)
        return y, kt, vt, _rows_to_state(st), cst

    mk_p, mv_p = _memkv(mem_prompt.reshape(b * n_mem, dm), row(norm_mem_src[l]), w_kv_b, mkn, n_mem)
    moba_p = lambda q, k, ksum, kt, vt, vtb: _moba_prompt(q, k, vtb, ksum.reshape(b, t // MB_BLOCK, WIDTH), rel_bias, b)
    y_p, kt_p, vt_p, s_p, c_p = layer(
        x_prompt.reshape(b * t, dm), b, PROMPT_TILE, jnp.zeros((b, HEAD_D, WIDTH), F32), moba_p,
        mk_p.reshape(b, n_mem, -1), mv_p.reshape(b, n_mem, -1), jnp.zeros((b, CONV_W - 1, dff), F32))

    pool_kt = cache_k[l].transpose(0, 2, 3, 1)
    pool_vt = cache_v[l].transpose(0, 2, 3, 1)
    per_seq = lambda a: a.reshape(WIDTH, db, ds).transpose(1, 0, 2)

    def moba_s(q, k, ksum, kt, vt, vtb):
        sel = _sample_select(q, _page_ksum(pool_kt, page_table), ds)
        o = _sample_attend(q, per_seq(kt).reshape(db * H_MB, HEAD_D, ds), per_seq(vt).reshape(db * H_MB, HEAD_D, ds),
                           pool_kt, pool_vt, page_table, sel.reshape(-1), rel_bias, ds)
        return o.reshape(db, H_MB, ds, HEAD_D).transpose(0, 2, 1, 3).reshape(db * ds, WIDTH)

    y_s, kt_s, vt_s, s_s, c_s = layer(
        x_sample.reshape(db * ds, dm), db, ds, _state_to_rows(state_hgrn[l]), moba_s,
        cache_mem_k[l].reshape(db, n_mem, -1), cache_mem_v[l].reshape(db, n_mem, -1), state_conv[l])

    hd = lambda a, bb, tt: a.reshape(bb, H_MB, HEAD_D, tt).transpose(0, 3, 1, 2)[None]
    return (y_p.reshape(b, t, dm), y_s.reshape(db, ds, dm),
            hd(kt_p, b, t), hd(vt_p, b, t), hd(per_seq(kt_s), db, ds), hd(per_seq(vt_s), db, ds),
            s_p[None], s_s[None], c_p[None], c_s[None],
            mk_p.reshape(1, b, n_mem, H_MEM, MEM_HD), mv_p.reshape(1, b, n_mem, H_MEM, MEM_HD))
```

```python
import functools
import math

import numpy as np
import jax
import jax.numpy as jnp
from jax import lax
from jax.experimental import pallas as pl
from jax.experimental.pallas import tpu as pltpu

F32 = jnp.float32
BF16 = jnp.bfloat16
EPS = 1e-6

H_HG = 8
H_MB = 8
HEAD_D = 64
WIDTH = 512
MB_BLOCK = 256
MB_TOPK = 3
NUM_BUCKETS = 32
MAX_DISTANCE = 8192
H_MEM = 4
MEM_HD = 128
CONV_W = 3

LANES = 128
SUBLANES = 8
VMEM_LIMIT = 56 * 1024 * 1024

HG_SUB = 16
HG_GROUP = 128
HG_SEQS = 4
HG_TILE = 128


def _cparams(*sem):
    return pltpu.CompilerParams(dimension_semantics=sem, vmem_limit_bytes=VMEM_LIMIT)


def _const_spec(shape):
    nd = len(shape)
    return pl.BlockSpec(shape, lambda *_: (0,) * nd, pipeline_mode=pl.Buffered(1))


def _group_ones(width, group):
    i = np.arange(width) // group
    return (i[:, None] == i[None, :]).astype(np.float32)


def _split_dot(x, ones_bf16):
    hi = x.astype(BF16)
    lo = (x - hi.astype(F32)).astype(BF16)
    return (jnp.dot(hi, ones_bf16, preferred_element_type=F32)
            + jnp.dot(lo, ones_bf16, preferred_element_type=F32))


def _seg_rmsnorm(x, gain, ones_bf16, seg):
    ms = _split_dot(x * x, ones_bf16) * (1.0 / seg)
    return x * lax.rsqrt(ms + EPS) * gain


def _rmsnorm_rows(x, gain):
    ms = jnp.mean(x * x, axis=-1, keepdims=True)
    return x * lax.rsqrt(ms + EPS) * gain


def _inproj_kernel(x_ref, gmix_ref, w_ref, wvt_ref, lbl_ref, qn_ref, kn_ref, ones_ref,
                   qhg_ref, khg_ref, ihg_ref, logf_ref, g_ref, qmb_ref, kmb_ref, ksum_ref, kt_ref, vt_ref, vtb_ref):
    xb = _rmsnorm_rows(x_ref[...], gmix_ref[...]).astype(BF16)

    def proj(i):
        return jnp.dot(xb, w_ref[:, i * WIDTH:(i + 1) * WIDTH], preferred_element_type=F32)

    qhg_ref[...] = proj(0)
    f_logit = proj(1)
    logits = lbl_ref[...]
    e = jnp.exp(logits - jnp.max(logits, axis=0, keepdims=True))
    lb = e[0:1] / jnp.sum(e, axis=0, keepdims=True)
    logf_ref[...] = jnp.log(lb + (1.0 - lb) * jax.nn.sigmoid(f_logit))
    khg_ref[...] = (1.0 - lb) * jax.nn.sigmoid(-f_logit)
    ihg_ref[...] = proj(2)
    g_ref[...] = proj(3)
    ones = ones_ref[...]
    qmb_ref[...] = _seg_rmsnorm(proj(4), qn_ref[...], ones, HEAD_D)
    k = _seg_rmsnorm(proj(5), kn_ref[...], ones, HEAD_D)
    kmb_ref[...] = k.astype(BF16)
    for j in range(ksum_ref.shape[0]):
        rows = k.shape[0] // ksum_ref.shape[0]
        ksum_ref[j] = jnp.sum(k[j * rows:(j + 1) * rows], axis=0, keepdims=True)
    kt_ref[0] = k.T
    vt = lax.dot_general(wvt_ref[...], xb, (((1,), (1,)), ((), ())), preferred_element_type=F32)
    vt_ref[0] = vt
    vtb_ref[0] = vt.astype(BF16)


def _inproj(x, gmix, w_bf16, lb_logits, qn, kn, batch, tm):
    n, d = x.shape
    t = n // batch
    assert t % tm == 0
    tpb = t // tm
    grp = min(tm, MB_BLOCK)
    assert tm % grp == 0
    tok = pl.BlockSpec((tm, WIDTH), lambda i: (i, 0))
    tr = pl.BlockSpec((1, WIDTH, tm), lambda i: (i // tpb, 0, i % tpb))
    outs = ([jax.ShapeDtypeStruct((n, WIDTH), F32)] * 6 + [jax.ShapeDtypeStruct((n, WIDTH), BF16)]
            + [jax.ShapeDtypeStruct((n // grp, 1, WIDTH), F32)]
            + [jax.ShapeDtypeStruct((batch, WIDTH, t), F32)] * 2 + [jax.ShapeDtypeStruct((batch, WIDTH, t), BF16)])
    ones = jnp.asarray(_group_ones(WIDTH, HEAD_D), BF16)
    w_main, wvt = w_bf16[:, :6 * WIDTH], w_bf16[:, 6 * WIDTH:].T
    return pl.pallas_call(
        _inproj_kernel,
        grid=(n // tm,),
        in_specs=[pl.BlockSpec((tm, d), lambda i: (i, 0)), _const_spec((1, d)), _const_spec(w_main.shape),
                  _const_spec(wvt.shape), _const_spec(lb_logits.shape), _const_spec((1, WIDTH)),
                  _const_spec((1, WIDTH)), _const_spec((WIDTH, WIDTH))],
        out_specs=[tok] * 7 + [pl.BlockSpec((tm // grp, 1, WIDTH), lambda i: (i, 0, 0)), tr, tr, tr],
        out_shape=outs,
        compiler_params=_cparams("parallel"),
        name="inproj",
    )(x, gmix, w_main, wvt, lb_logits, qn, kn, ones)


def _hgrn_kernel(q_ref, k_ref, v_ref, lf_ref, g_ref, s0_ref, gn_ref, ones_ref, gmask_ref,
                 o_ref, sfin_ref, st_scr, a_scr, o_scr, *, tc, nseq, sub):
    c = pl.program_id(1)
    ng = WIDTH // HG_GROUP
    hpg = HG_GROUP // HEAD_D
    gmask = gmask_ref[...]

    @pl.when(c == 0)
    def _():
        for bi in range(nseq):
            s0 = s0_ref[bi]
            for gi in range(ng):
                blk = s0[:, gi * HG_GROUP:(gi + 1) * HG_GROUP]
                st_scr[bi * ng + gi] = jnp.concatenate([blk] * hpg, axis=0) * gmask

    row = lax.broadcasted_iota(jnp.int32, (tc, WIDTH), 0) % sub
    for bi in range(nseq):
        a = lf_ref[bi]
        sh = 1
        while sh < sub:
            a = a + jnp.where(row >= sh, pltpu.roll(a, sh, 0), 0.0)
            sh *= 2
        a_scr[bi] = a

    ones = ones_ref[...]
    trow = lax.broadcasted_iota(jnp.int32, (sub, WIDTH), 0)

    def step(n, carry):
        off = pl.multiple_of(n * sub, sub)
        for bi in range(nseq):
            q = q_ref[bi, pl.ds(off, sub), :]
            k = k_ref[bi, pl.ds(off, sub), :]
            v = v_ref[bi, pl.ds(off, sub), :]
            al = a_scr[bi, pl.ds(off, sub), :]
            a_end = al[sub - 1:sub, :]
            qe = (q * jnp.exp(al)).astype(BF16)
            kd = (k * jnp.exp(a_end - al)).astype(BF16)
            dec = jnp.exp(a_end)
            vb = v.astype(BF16)
            parts = []
            for s in range(sub):
                a_s = a_scr[bi, pl.ds(off + s, 1), :]
                k_s = k_ref[bi, pl.ds(off + s, 1), :]
                e_s = q * k_s * jnp.exp(al - a_s)
                parts.append(jnp.where(trow >= s, e_s, 0.0))
            ecat = jnp.concatenate(parts, axis=0).astype(BF16)
            outs = []
            for gi in range(ng):
                sl = slice(gi * HG_GROUP, (gi + 1) * HG_GROUP)
                st = st_scr[bi * ng + gi]
                o_g = lax.dot_general(qe[:, sl], st.astype(BF16), (((1,), (1,)), ((), ())),
                                      preferred_element_type=F32)
                ag = jnp.dot(ecat[:, sl], ones, preferred_element_type=F32)
                for s in range(sub):
                    o_g = o_g + ag[s * sub:(s + 1) * sub, :] * v[s:s + 1, sl]
                outs.append(o_g)
                upd = lax.dot_general(vb[:, sl], kd[:, sl], (((0,), (0,)), ((), ())),
                                      preferred_element_type=F32)
                st_scr[bi * ng + gi] = st * dec[:, sl] + upd * gmask
            o_scr[bi, pl.ds(off, sub), :] = jnp.concatenate(outs, axis=1)
        return carry

    lax.fori_loop(0, tc // sub, step, 0)

    for bi in range(nseq):
        o = o_scr[bi]
        ms = jnp.concatenate(
            [_split_dot(o[:, gi * HG_GROUP:(gi + 1) * HG_GROUP] ** 2, ones) for gi in range(ng)], axis=1)
        g = g_ref[bi]
        o_ref[bi] = o * lax.rsqrt(ms * (1.0 / HEAD_D) + EPS) * gn_ref[...] * (g * jax.nn.sigmoid(g))

    @pl.when(c == pl.num_programs(1) - 1)
    def _():
        for bi in range(nseq):
            cols = []
            for gi in range(ng):
                st = st_scr[bi * ng + gi]
                acc = st[0:HEAD_D, :]
                for hh in range(1, hpg):
                    acc = acc + st[hh * HEAD_D:(hh + 1) * HEAD_D, :]
                cols.append(acc)
            sfin_ref[bi] = jnp.concatenate(cols, axis=1)


def _hgrn(q, k, v, logf, g, s0t, gn, batch, tc):
    n = q.shape[0]
    t = n // batch
    nseq = min(HG_SEQS, batch)
    sub = min(HG_SUB, tc)
    assert t % tc == 0 and tc % sub == 0 and batch % nseq == 0
    ng = WIDTH // HG_GROUP
    seq = lambda a: a.reshape(batch, t, WIDTH)
    tok = pl.BlockSpec((nseq, tc, WIDTH), lambda b, c: (b, c, 0))
    st_spec = pl.BlockSpec((nseq, HEAD_D, WIDTH), lambda b, c: (b, 0, 0))
    gm = _group_ones(HG_GROUP, HEAD_D)
    kern = functools.partial(_hgrn_kernel, tc=tc, nseq=nseq, sub=sub)
    o, st = pl.pallas_call(
        kern,
        grid=(batch // nseq, t // tc),
        in_specs=[tok] * 5 + [st_spec, _const_spec((1, WIDTH)), _const_spec((HG_GROUP, HG_GROUP)),
                              _const_spec((HG_GROUP, HG_GROUP))],
        out_specs=[tok, st_spec],
        out_shape=[jax.ShapeDtypeStruct((batch, t, WIDTH), F32), jax.ShapeDtypeStruct((batch, HEAD_D, WIDTH), F32)],
        scratch_shapes=[pltpu.VMEM((nseq * ng, HG_GROUP, HG_GROUP), F32),
                        pltpu.VMEM((nseq, tc, WIDTH), F32), pltpu.VMEM((nseq, tc, WIDTH), F32)],
        compiler_params=_cparams("parallel", "arbitrary"),
        name="hgrn",
    )(seq(q), seq(k), seq(v), seq(logf), seq(g), s0t, gn, jnp.asarray(gm, BF16), jnp.asarray(gm, F32))
    return o.reshape(n, WIDTH), st


NEG = -1e30


def _bucket_table(max_dist):
    max_exact = NUM_BUCKETS // 2
    d = np.arange(max_dist)
    nf = np.maximum(d, 1).astype(np.float32)
    large = max_exact + (np.log(nf / max_exact) / math.log(MAX_DISTANCE / max_exact)
                         * (NUM_BUCKETS - max_exact)).astype(np.int32)
    bucket = np.where(d < max_exact, d, np.minimum(large, NUM_BUCKETS - 1))
    assert np.all(np.diff(bucket) >= 0) and bucket[-1] == NUM_BUCKETS - 1
    first = [int(np.argmax(bucket >= kk)) for kk in range(NUM_BUCKETS)]
    return bucket, first


_BUCKET, _BUCKET_FIRST = _bucket_table(2 * MAX_DISTANCE)


def _bias_of_distance(d, rb_ref, h):
    acc = jnp.full(d.shape, rb_ref[0, h], F32)
    for kk in range(1, NUM_BUCKETS):
        acc = jnp.where(d >= _BUCKET_FIRST[kk], rb_ref[kk, h], acc)
    return acc


def _far_block_table(nb):
    big = 4 * MB_BLOCK
    tab = np.zeros((nb, 5), np.int32)
    for delta in range(nb):
        lo, hi = max(delta * MB_BLOCK - (MB_BLOCK - 1), 0), delta * MB_BLOCK + (MB_BLOCK - 1)
        b0 = int(_BUCKET[lo])
        ks = [kk for kk in range(b0 + 1, NUM_BUCKETS) if _BUCKET_FIRST[kk] <= hi]
        assert delta < 2 or len(ks) <= 2
        cs = [_BUCKET_FIRST[kk] - delta * MB_BLOCK for kk in ks[:2]] + [big, big]
        tab[delta] = [b0, min(b0 + 1, NUM_BUCKETS - 1), min(b0 + 2, NUM_BUCKETS - 1), cs[0], cs[1]]
    return tab


def _top_blocks(gate, n_valid, n_top, axis=0):
    nb = gate.shape[axis]
    j = lax.broadcasted_iota(jnp.int32, gate.shape, axis)
    g = jnp.where(j < n_valid, gate, -jnp.inf)
    sel = jnp.zeros(gate.shape, F32)
    idxs, oks = [], []
    for _ in range(n_top):
        mx = jnp.max(g, axis=axis, keepdims=True)
        idx = jnp.min(jnp.where(g == mx, j, nb), axis=axis, keepdims=True)
        ok = jnp.where(mx > -jnp.inf, 1.0, 0.0)
        pick = jnp.where(j == idx, ok, 0.0) > 0.0
        sel = jnp.where(pick, 1.0, sel)
        g = jnp.where(pick, -jnp.inf, g)
        idxs.append(idx)
        oks.append(ok)
    return idxs, oks, sel


def _pair_masks(shape):
    lane = lax.broadcasted_iota(jnp.int32, shape, len(shape) - 1)
    return [lane < HEAD_D, lane >= HEAD_D]


def _moba_prompt_kernel(qi_ref, kj_ref, tab_ref, q_ref, k_ref, vt_ref, vtp_ref, ksum_ref, rb_ref, o_ref,
                        qm_scr, sel_scr, m_scr, l_scr, acc_scr, near_scr, alpha0_scr, p0_scr, alpha1_scr, p1_scr):
    p = pl.program_id(0)
    qi = qi_ref[p]
    kj = kj_ref[p]
    delta = qi - kj
    blk = MB_BLOCK
    nseq = q_ref.shape[0]
    dts = (lax.broadcasted_iota(jnp.int32, (blk, blk), 1)
           - lax.broadcasted_iota(jnp.int32, (blk, blk), 0))
    slots = ((alpha0_scr, p0_scr), (alpha1_scr, p1_scr))

    @pl.when(p == 0)
    def _():
        for h in range(H_MB):
            near_scr[h] = jnp.where(dts >= 0, _bias_of_distance(dts, rb_ref, h), NEG)
            near_scr[H_MB + h] = _bias_of_distance(dts + blk, rb_ref, h)

    def sequence(b, carry):
        hb = b * H_MB

        @pl.when(kj == 0)
        def _():
            q = q_ref[b]
            kmean = ksum_ref[b] * (1.0 / blk)
            masks = _pair_masks((blk, LANES))
            for h in range(H_MB):
                m_scr[hb + h] = jnp.full((1, blk), NEG, F32)
                l_scr[hb + h] = jnp.zeros((1, blk), F32)
                acc_scr[hb + h] = jnp.zeros((HEAD_D, blk), F32)
                p1_scr[hb + h] = jnp.zeros((blk, blk), BF16)
                alpha1_scr[hb + h] = jnp.ones((1, blk), F32)
                sl = slice((h // 2) * LANES, (h // 2 + 1) * LANES)
                qh = jnp.where(masks[h % 2], q[:, sl], 0.0)
                qm_scr[hb + h] = (qh * (HEAD_D ** -0.5)).T.astype(BF16)
                gate = lax.dot_general(kmean[:, sl], qh, (((1,), (1,)), ((), ())),
                                       precision=lax.Precision.HIGHEST, preferred_element_type=F32)
                _, _, sel = _top_blocks(gate, qi, MB_TOPK)
                sel_scr[hb + h] = jnp.where(sel > 0.0, 0.0, NEG)

        def attend(bias_of, par, own_block=False):
            (a_cur, p_cur), (a_prv, p_prv) = slots[par], slots[1 - par]
            for h in range(H_MB):
                sl = slice((h // 2) * LANES, (h // 2 + 1) * LANES)
                s_all = jnp.dot(k_ref[b, :, sl], qm_scr[hb + h], preferred_element_type=F32)
                for qh in range(blk // LANES):
                    ql = slice(qh * LANES, (qh + 1) * LANES)
                    s = s_all[:, ql] + bias_of(h, qh)
                    m_old = m_scr[hb + h, :, ql]
                    m_new = jnp.maximum(m_old, jnp.max(s, axis=0, keepdims=True))
                    alpha = jnp.exp(m_old - m_new)
                    pexp = jnp.exp(s - m_new)
                    l_scr[hb + h, :, ql] = alpha * l_scr[hb + h, :, ql] + jnp.sum(pexp, axis=0, keepdims=True)
                    m_scr[hb + h, :, ql] = m_new
                    a_cur[hb + h, :, ql] = alpha
                    p_cur[hb + h, :, ql] = pexp.astype(BF16)
            for a_scr, p_scr, v_ref in ((a_prv, p_prv, vtp_ref),) + (((a_cur, p_cur, vt_ref),) if own_block else ()):
                for h in range(H_MB):
                    hs = slice(h * HEAD_D, (h + 1) * HEAD_D)
                    pv = jnp.dot(v_ref[b, hs, :], p_scr[hb + h], preferred_element_type=F32)
                    acc_scr[hb + h] = a_scr[hb + h] * acc_scr[hb + h] + pv

        def for_each_parity(cond, body):
            for par in range(2):
                pl.when(cond & (kj % 2 == par))(functools.partial(body, par))

        def selrow(h, qh):
            return sel_scr[hb + h, pl.ds(kj, 1), :][:, qh * LANES:(qh + 1) * LANES]

        has_steps = tab_ref[delta, 3] < 2 * blk

        for_each_parity(delta == 0, lambda par: attend(
            lambda h, qh: near_scr[h, :, qh * LANES:(qh + 1) * LANES], par, own_block=True))

        for_each_parity(delta == 1, lambda par: attend(
            lambda h, qh: near_scr[H_MB + h, :, qh * LANES:(qh + 1) * LANES] + selrow(h, qh), par))

        for_each_parity((delta >= 2) & jnp.logical_not(has_steps), lambda par: attend(
            lambda h, qh: rb_ref[tab_ref[delta, 0], h] + selrow(h, qh), par))

        def stepped(par):
            dts0 = dts[:, :LANES]

            def bias_of(h, qh):
                sr = selrow(h, qh)
                r0 = rb_ref[tab_ref[delta, 0], h] + sr
                r1 = rb_ref[tab_ref[delta, 1], h] + sr
                r2 = rb_ref[tab_ref[delta, 2], h] + sr
                m1 = dts0 >= tab_ref[delta, 3] - qh * LANES
                m2 = dts0 >= tab_ref[delta, 4] - qh * LANES
                return jnp.where(m2, r2, jnp.where(m1, r1, r0))

            attend(bias_of, par)

        for_each_parity((delta >= 2) & has_steps, stepped)

        @pl.when(delta == 0)
        def _():
            o_ref[b] = jnp.concatenate([acc_scr[hb + h] / l_scr[hb + h] for h in range(H_MB)], axis=0).T

        return carry

    lax.fori_loop(0, nseq, sequence, 0)


def _moba_prompt(q, k, vt, ksum, rel_bias, batch):
    n = q.shape[0]
    t = n // batch
    assert t % MB_BLOCK == 0
    nb = t // MB_BLOCK
    qi = np.array([i for i in range(nb) for _ in range(i + 1)], np.int32)
    kj = np.array([j for i in range(nb) for j in range(i + 1)], np.int32)
    tab = _far_block_table(nb)
    blk = MB_BLOCK
    rows = batch * H_MB
    grid_spec = pltpu.PrefetchScalarGridSpec(
        num_scalar_prefetch=3,
        grid=(len(qi),),
        in_specs=[
            pl.BlockSpec((batch, blk, WIDTH), lambda p, qi, kj, tab: (0, qi[p], 0)),
            pl.BlockSpec((batch, blk, WIDTH), lambda p, qi, kj, tab: (0, kj[p], 0)),
            pl.BlockSpec((batch, WIDTH, blk), lambda p, qi, kj, tab: (0, 0, kj[p])),
            pl.BlockSpec((batch, WIDTH, blk), lambda p, qi, kj, tab: (0, 0, jnp.maximum(kj[p] - 1, 0))),
            pl.BlockSpec((batch, nb, WIDTH), lambda p, qi, kj, tab: (0, 0, 0)),
            pl.BlockSpec(memory_space=pltpu.SMEM),
        ],
        out_specs=pl.BlockSpec((batch, blk, WIDTH), lambda p, qi, kj, tab: (0, qi[p], 0)),
        scratch_shapes=[pltpu.VMEM((rows, LANES, blk), BF16), pltpu.VMEM((rows, nb, blk), F32),
                        pltpu.VMEM((rows, 1, blk), F32), pltpu.VMEM((rows, 1, blk), F32),
                        pltpu.VMEM((rows, HEAD_D, blk), F32), pltpu.VMEM((2 * H_MB, blk, blk), F32),
                        pltpu.VMEM((rows, 1, blk), F32), pltpu.VMEM((rows, blk, blk), BF16),
                        pltpu.VMEM((rows, 1, blk), F32), pltpu.VMEM((rows, blk, blk), BF16)],
    )
    seq = lambda a: a.reshape(batch, t, WIDTH)
    return pl.pallas_call(
        _moba_prompt_kernel,
        grid_spec=grid_spec,
        out_shape=jax.ShapeDtypeStruct((batch, t, WIDTH), F32),
        compiler_params=_cparams("arbitrary"),
        name="moba_prompt",
    )(jnp.asarray(qi), jnp.asarray(kj), jnp.asarray(tab), seq(q), seq(k), vt, vt, ksum, rel_bias).reshape(n, WIDTH)


SAMPLE_PAGES_PER_STEP = 32


def _page_ksum_kernel(pt_ref, *refs, ppb):
    pages, out_ref = refs[:-1], refs[-1]
    j = pl.program_id(1)
    nblk = out_ref.shape[2]
    lane = lax.broadcasted_iota(jnp.int32, (WIDTH, nblk), 1)

    @pl.when(j == 0)
    def _():
        out_ref[...] = jnp.zeros(out_ref.shape, F32)

    acc = out_ref[0]
    for i in range(0, len(pages), ppb):
        blk = pages[i][0].reshape(WIDTH, pages[i].shape[-1])
        for e in range(1, ppb):
            blk = blk + pages[i + e][0].reshape(blk.shape)
        col = jnp.sum(blk, axis=1, keepdims=True)
        acc = jnp.where(lane == j * (len(pages) // ppb) + i // ppb, col, acc)
    out_ref[0] = acc


def _page_ksum(pool_t, page_table):
    db, n_pages = page_table.shape
    page = pool_t.shape[-1]
    ppb = MB_BLOCK // page
    pps = min(SAMPLE_PAGES_PER_STEP, n_pages)
    assert MB_BLOCK % page == 0 and n_pages % pps == 0 and pps % ppb == 0
    nblk = n_pages // ppb
    specs = [pl.BlockSpec((1,) + pool_t.shape[1:], (lambda b, j, pt, i=i: (pt[b, j * pps + i], 0, 0, 0)))
             for i in range(pps)]
    grid_spec = pltpu.PrefetchScalarGridSpec(
        num_scalar_prefetch=1, grid=(db, n_pages // pps), in_specs=specs,
        out_specs=pl.BlockSpec((1, WIDTH, nblk), lambda b, j, pt: (b, 0, 0)))
    return pl.pallas_call(
        functools.partial(_page_ksum_kernel, ppb=ppb), grid_spec=grid_spec,
        out_shape=jax.ShapeDtypeStruct((db, WIDTH, nblk), F32),
        compiler_params=_cparams("parallel", "arbitrary"), name="page_ksum",
    )(page_table, *([pool_t] * pps))


def _sample_select_kernel(q_ref, ksum_ref, idx_ref):
    q = q_ref[...]
    kmean_t = ksum_ref[0] * (1.0 / MB_BLOCK)
    nblk = kmean_t.shape[1]
    cols = []
    for h in range(H_MB):
        hs = slice(h * HEAD_D, (h + 1) * HEAD_D)
        gate = jnp.dot(q[:, hs], kmean_t[hs, :], precision=lax.Precision.HIGHEST,
                       preferred_element_type=F32)
        idxs, _, _ = _top_blocks(gate, nblk, MB_TOPK, axis=1)
        cols.extend(idxs)
    idx_ref[0] = jnp.concatenate(cols, axis=1)


def _sample_select(q, ksum_t, ds):
    db, _, nblk = ksum_t.shape
    assert nblk >= MB_TOPK
    return pl.pallas_call(
        _sample_select_kernel, grid=(db,),
        in_specs=[pl.BlockSpec((ds, WIDTH), lambda b: (b, 0)), pl.BlockSpec((1, WIDTH, nblk), lambda b: (b, 0, 0))],
        out_specs=pl.BlockSpec((1, ds, H_MB * MB_TOPK), lambda b: (b, 0, 0)),
        out_shape=jax.ShapeDtypeStruct((db, ds, H_MB * MB_TOPK), jnp.int32),
        compiler_params=_cparams("parallel"), name="sample_select",
    )(q, ksum_t)


def _sample_attend_kernel(pt_ref, sel_ref, q_ref, kn_ref, vn_ref, rb_ref, kpool_ref, vpool_ref, o_ref,
                          kbuf, vbuf, sems, *, ds, page, past_len):
    ppb = MB_BLOCK // page
    ntile = MB_TOPK * ppb
    step = pl.program_id(0)
    nstep = pl.num_programs(0)

    def copies(st, slot):
        b, h = st // H_MB, st % H_MB
        out = []
        for q in range(ds):
            for r in range(MB_TOPK):
                blk = sel_ref[((b * ds + q) * H_MB + h) * MB_TOPK + r]
                for e in range(ppb):
                    pg = pt_ref[b, blk * ppb + e]
                    dst = pl.ds((r * ppb + e) * page, page)
                    out.append(pltpu.make_async_copy(kpool_ref.at[pg, h], kbuf.at[slot, q, :, dst], sems.at[0, slot]))
                    out.append(pltpu.make_async_copy(vpool_ref.at[pg, h], vbuf.at[slot, q, :, dst], sems.at[1, slot]))
        return out

    @pl.when(step == 0)
    def _():
        for cp in copies(step, 0):
            cp.start()

    @pl.when(step + 1 < nstep)
    def _():
        for cp in copies(step + 1, (step + 1) % 2):
            cp.start()

    slot = step % 2
    for qq in range(ds):
        for tile in range(ntile):
            dst = pl.ds(tile * page, page)
            pltpu.make_async_copy(kpool_ref.at[0, 0], kbuf.at[slot, qq, :, dst], sems.at[0, slot]).wait()
            pltpu.make_async_copy(vpool_ref.at[0, 0], vbuf.at[slot, qq, :, dst], sems.at[1, slot]).wait()

    b, h = step // H_MB, step % H_MB
    q = (q_ref[0] * (HEAD_D ** -0.5)).astype(BF16)
    rowi = lax.broadcasted_iota(jnp.int32, (ds, ntile * page), 0)
    s_sel = jnp.zeros((ds, ntile * page), F32)
    kpos = jnp.zeros((ds, ntile * page), jnp.int32)
    lane = lax.broadcasted_iota(jnp.int32, (ds, page), 1)
    for qq in range(ds):
        res = jnp.dot(q, kbuf[slot, qq].astype(BF16), preferred_element_type=F32)
        s_sel = jnp.where(rowi == qq, res, s_sel)
        pieces = []
        for r in range(MB_TOPK):
            blk = sel_ref[((b * ds + qq) * H_MB + h) * MB_TOPK + r]
            for e in range(ppb):
                pieces.append(blk * MB_BLOCK + e * page + lane)
        kpos = jnp.where(rowi == qq, jnp.concatenate(pieces, axis=1), kpos)
    q_pos = past_len + lax.broadcasted_iota(jnp.int32, (ds, 1), 0)
    s_sel = s_sel + _bias_of_distance(q_pos - kpos, rb_ref, h)
    own = lax.broadcasted_iota(jnp.int32, (ds, ds), 1)
    qrow = lax.broadcasted_iota(jnp.int32, (ds, ds), 0)
    s_own = jnp.dot(q, kn_ref[0].astype(BF16), preferred_element_type=F32) \
        + _bias_of_distance(qrow - own, rb_ref, h)
    s_own = jnp.where(own <= qrow, s_own, NEG)
    m = jnp.maximum(jnp.max(s_sel, axis=1, keepdims=True), jnp.max(s_own, axis=1, keepdims=True))
    p_sel = jnp.exp(s_sel - m)
    p_own = jnp.exp(s_own - m)
    den = jnp.sum(p_sel, axis=1, keepdims=True) + jnp.sum(p_own, axis=1, keepdims=True)
    nt_dims = (((1,), (1,)), ((), ()))
    o = lax.dot_general(p_own.astype(BF16), vn_ref[0].astype(BF16), nt_dims, preferred_element_type=F32)
    rowo = lax.broadcasted_iota(jnp.int32, (ds, HEAD_D), 0)
    p_sel_b = p_sel.astype(BF16)
    for qq in range(ds):
        res = lax.dot_general(p_sel_b, vbuf[slot, qq].astype(BF16), nt_dims, preferred_element_type=F32)
        o = o + jnp.where(rowo == qq, res, 0.0)
    o_ref[0] = o / den


def _sample_attend(q, kt_new, vt_new, pool_kt, pool_vt, page_table, sel_flat, rel_bias, ds):
    db, n_pages = page_table.shape
    page = pool_kt.shape[-1]
    ppb = MB_BLOCK // page
    past_len = n_pages * page
    assert past_len % MB_BLOCK == 0
    ntile = MB_TOPK * ppb
    qh = q.reshape(db, ds, H_MB, HEAD_D).transpose(0, 2, 1, 3).reshape(db * H_MB, ds, HEAD_D)
    per_head = lambda shape: pl.BlockSpec((1,) + shape, lambda s, pt, sel: (s, 0, 0))
    grid_spec = pltpu.PrefetchScalarGridSpec(
        num_scalar_prefetch=2, grid=(db * H_MB,),
        in_specs=[per_head((ds, HEAD_D)), per_head((HEAD_D, ds)), per_head((HEAD_D, ds)),
                  pl.BlockSpec(memory_space=pltpu.SMEM), pl.BlockSpec(memory_space=pl.ANY),
                  pl.BlockSpec(memory_space=pl.ANY)],
        out_specs=per_head((ds, HEAD_D)),
        scratch_shapes=[pltpu.VMEM((2, ds, HEAD_D, ntile * page), F32), pltpu.VMEM((2, ds, HEAD_D, ntile * page), F32),
                        pltpu.SemaphoreType.DMA((2, 2))])
    kern = functools.partial(_sample_attend_kernel, ds=ds, page=page, past_len=past_len)
    return pl.pallas_call(
        kern, grid_spec=grid_spec, out_shape=jax.ShapeDtypeStruct((db * H_MB, ds, HEAD_D), F32),
        compiler_params=_cparams("arbitrary"), name="sample_attend",
    )(page_table, sel_flat, qh, kt_new, vt_new, rel_bias, pool_kt, pool_vt)


def _memkv_kernel(m_ref, g_ref, w_ref, kn_ref, ones_ref, k_ref, v_ref):
    xb = _rmsnorm_rows(m_ref[...], g_ref[...]).astype(BF16)
    wk = w_ref.shape[1] // 2
    k = jnp.dot(xb, w_ref[:, :wk], preferred_element_type=F32)
    k_ref[...] = _seg_rmsnorm(k, kn_ref[...], ones_ref[...], MEM_HD)
    v_ref[...] = jnp.dot(xb, w_ref[:, wk:], preferred_element_type=F32)


def _memkv(mem, g, w_bf16, kn, tm):
    n, d = mem.shape
    wk = w_bf16.shape[1] // 2
    out = pl.BlockSpec((tm, wk), lambda i: (i, 0))
    return pl.pallas_call(
        _memkv_kernel, grid=(n // tm,),
        in_specs=[pl.BlockSpec((tm, d), lambda i: (i, 0)), _const_spec((1, d)), _const_spec(w_bf16.shape),
                  _const_spec((1, wk)), _const_spec((wk, wk))],
        out_specs=[out, out], out_shape=[jax.ShapeDtypeStruct((n, wk), F32)] * 2,
        compiler_params=_cparams("parallel"), name="mem_kv",
    )(mem, g, w_bf16, kn, jnp.asarray(_group_ones(wk, MEM_HD), BF16))


def _mixmem_kernel(x_ref, hg_ref, omb_ref, wout_ref, gmem_ref, wq_ref, qn_ref, ones_ref, mk_ref, mv_ref, wo_ref,
                   h_ref, *, nseq):
    mix = (jnp.dot(hg_ref[...].astype(BF16), wout_ref[:WIDTH, :], preferred_element_type=F32)
           + jnp.dot(omb_ref[...].astype(BF16), wout_ref[WIDTH:, :], preferred_element_type=F32))
    h1 = x_ref[...] + mix
    hn = _rmsnorm_rows(h1, gmem_ref[...]).astype(BF16)
    q = _seg_rmsnorm(jnp.dot(hn, wq_ref[...], preferred_element_type=F32), qn_ref[...], ones_ref[...], MEM_HD)
    tm = q.shape[0] // nseq
    per_seq = []
    for i in range(nseq):
        qb = q[i * tm:(i + 1) * tm].astype(BF16)
        mk = mk_ref[i].astype(BF16)
        mv = mv_ref[i].astype(BF16)
        outs = []
        for h in range(H_MEM):
            sl = slice(h * MEM_HD, (h + 1) * MEM_HD)
            s = lax.dot_general(qb[:, sl], mk[:, sl], (((1,), (1,)), ((), ())),
                                preferred_element_type=F32) * (MEM_HD ** -0.5)
            e = jnp.exp(s - jnp.max(s, axis=-1, keepdims=True))
            p = e / jnp.sum(e, axis=-1, keepdims=True)
            outs.append(jnp.dot(p.astype(BF16), mv[:, sl], preferred_element_type=F32))
        per_seq.append(jnp.concatenate(outs, axis=1))
    o = jnp.concatenate(per_seq, axis=0).astype(BF16)
    h_ref[...] = h1 + jnp.dot(o, wo_ref[...], preferred_element_type=F32)


def _tail_seqs(batch, nt):
    return math.gcd(batch, SHORT_SEQS) if nt == 1 else 1


def _mixmem(x, hg, omb, wout, gmem, wq, qn, mk, mv, wo, batch, tm):
    n, d = x.shape
    t = n // batch
    assert t % tm == 0
    nt = t // tm
    nseq = _tail_seqs(batch, nt)
    n_mem, wm = mk.shape[1:]
    tok = lambda w: pl.BlockSpec((nseq * tm, w), lambda b, i: (b * nt + i, 0))
    mem = pl.BlockSpec((nseq, n_mem, wm), lambda b, i: (b, 0, 0))
    return pl.pallas_call(
        functools.partial(_mixmem_kernel, nseq=nseq), grid=(batch // nseq, nt),
        in_specs=[tok(d), tok(WIDTH), tok(WIDTH), _const_spec(wout.shape), _const_spec((1, d)),
                  _const_spec(wq.shape), _const_spec((1, wm)), _const_spec((wm, wm)), mem, mem,
                  _const_spec(wo.shape)],
        out_specs=tok(d), out_shape=jax.ShapeDtypeStruct((n, d), F32),
        compiler_params=_cparams("parallel", "parallel"), name="mix_mem",
    )(x, hg, omb, wout, gmem, wq, qn, jnp.asarray(_group_ones(wm, MEM_HD), BF16), mk, mv, wo)


FFN_CHUNKS = 1
SHORT_SEQS = 8


def _ffn_kernel(h_ref, g_ref, wup_ref, cw_ref, cb_ref, wdn_ref, prev_ref, y_ref, cst_ref, carry_scr, *, dff, nseq):
    c = pl.program_id(1)

    @pl.when(c == 0)
    def _():
        carry_scr[...] = prev_ref[...]

    h = h_ref[...]
    hn = _rmsnorm_rows(h, g_ref[...]).astype(BF16)
    tm = h.shape[0] // nseq
    fc = dff // FFN_CHUNKS
    row = lax.broadcasted_iota(jnp.int32, (nseq * tm, fc), 0) % tm
    acc = h
    for ci in range(FFN_CHUNKS):
        sl = slice(ci * fc, (ci + 1) * fc)
        u = jnp.dot(hn, wup_ref[:, sl], preferred_element_type=F32)
        v = jnp.dot(hn, wup_ref[:, dff + ci * fc:dff + (ci + 1) * fc], preferred_element_type=F32)

        def carried(r):
            if nseq == 1:
                return carry_scr[0, r:r + 1, sl]
            return jnp.concatenate([jnp.broadcast_to(carry_scr[i, r:r + 1, sl], (tm, fc)) for i in range(nseq)], axis=0)

        u1 = jnp.where(row == 0, carried(1), pltpu.roll(u, 1, 0))
        u2 = jnp.where(row == 0, carried(0), jnp.where(row == 1, carried(1), pltpu.roll(u, 2, 0)))
        cw = cw_ref[:, sl]
        conv = cb_ref[:, sl] + u2 * cw[0:1] + u1 * cw[1:2] + u * cw[2:3]
        act = 0.5 * conv * (1.0 + lax.erf(conv * (2.0 ** -0.5))) * v
        acc = acc + jnp.dot(act.astype(BF16), wdn_ref[sl, :], preferred_element_type=F32)
        for i in range(nseq):
            last = u[(i + 1) * tm - (CONV_W - 1):(i + 1) * tm]
            carry_scr[i, :, sl] = last
            cst_ref[i, :, sl] = last
    y_ref[...] = acc


def _ffn(h, g, wup, cw, cb, wdn, prev, batch, tm):
    n, d = h.shape
    t = n // batch
    assert t % tm == 0 and tm >= CONV_W - 1
    nt = t // tm
    nseq = _tail_seqs(batch, nt)
    dff = wdn.shape[0]
    assert dff % (FFN_CHUNKS * LANES) == 0
    tok = pl.BlockSpec((nseq * tm, d), lambda b, i: (b * nt + i, 0))
    st = pl.BlockSpec((nseq, CONV_W - 1, dff), lambda b, i: (b, 0, 0))
    return pl.pallas_call(
        functools.partial(_ffn_kernel, dff=dff, nseq=nseq), grid=(batch // nseq, nt),
        in_specs=[tok, _const_spec((1, d)), _const_spec(wup.shape), _const_spec(cw.shape), _const_spec((1, dff)),
                  _const_spec(wdn.shape), st],
        out_specs=[tok, st],
        out_shape=[jax.ShapeDtypeStruct((n, d), F32), jax.ShapeDtypeStruct((batch, CONV_W - 1, dff), F32)],
        scratch_shapes=[pltpu.VMEM((nseq, CONV_W - 1, dff), F32)],
        compiler_params=_cparams("parallel", "arbitrary"), name="conv_ffn",
    )(h, g, wup, cw, cb, wdn, prev)


PROMPT_TILE = 512


def _state_to_rows(s):
    b = s.shape[0]
    return s.transpose(0, 3, 1, 2).reshape(b, HEAD_D, WIDTH)


def _rows_to_state(st):
    b = st.shape[0]
    return st.reshape(b, HEAD_D, H_HG, HEAD_D).transpose(0, 2, 3, 1)


def kernel(x_prompt, x_sample, cache_k, cache_v, page_table, state_hgrn, state_conv, cache_mem_k, cache_mem_v, mem_prompt, norm_mix, w_in, hg_lb_logits, hg_out_norm, mb_q_norm, mb_k_norm, rel_bias, w_out, norm_mem, norm_mem_src, w_mem_q, w_mem_kv, mem_q_norm, mem_k_norm, w_mem_o, norm_ffn, w_up, conv_w, conv_b, w_down):
    assert norm_mix.shape[0] == 1, "one layer"
    l = 0
    b, t, dm = x_prompt.shape
    db, ds, _ = x_sample.shape
    n_mem = mem_prompt.shape[1]
    dff = w_down.shape[1]
    row = lambda a: a[None]
    w_in_b, w_out_b = w_in[l].astype(BF16), w_out[l].astype(BF16)
    w_q_b, w_kv_b, w_o_b = w_mem_q[l].astype(BF16), w_mem_kv[l].astype(BF16), w_mem_o[l].astype(BF16)
    w_up_b, w_dn_b = w_up[l].astype(BF16), w_down[l].astype(BF16)
    qn, kn = row(jnp.tile(mb_q_norm[l], H_MB)), row(jnp.tile(mb_k_norm[l], H_MB))
    gn = row(jnp.tile(hg_out_norm[l], H_HG))
    mqn, mkn = row(jnp.tile(mem_q_norm[l], H_MEM)), row(jnp.tile(mem_k_norm[l], H_MEM))

    def layer(x, batch, tile, s0t, moba, mk, mv, prev):
        n = x.shape[0]
        tm = min(PROMPT_TILE, n)
        slabs = batch if (n // batch) % tm == 0 else 1
        qhg, khg, ihg, logf, g, qmb, kmb, ksum, kt, vt, vtb = _inproj(
            x, row(norm_mix[l]), w_in_b, hg_lb_logits, qn, kn, slabs, tm)
        hg, st = _hgrn(qhg, khg, ihg, logf, g, s0t, gn, batch, min(tile, HG_TILE))
        omb = moba(qmb, kmb, ksum, kt, vt, vtb)
        h = _mixmem(x, hg, omb, w_out_b, row(norm_mem[l]), w_q_b, mqn, mk, mv, w_o_b, batch, tile)
        y, cst = _ffn(h, row(norm_ffn[l]), w_up_b, conv_w[l], row(conv_b[l]), w_dn_b, prev, batch, tile)
        return y, kt, vt, _rows_to_state(st), cst

    mk_p, mv_p = _memkv(mem_prompt.reshape(b * n_mem, dm), row(norm_mem_src[l]), w_kv_b, mkn, n_mem)
    moba_p = lambda q, k, ksum, kt, vt, vtb: _moba_prompt(q, k, vtb, ksum.reshape(b, t // MB_BLOCK, WIDTH), rel_bias, b)
    y_p, kt_p, vt_p, s_p, c_p = layer(
        x_prompt.reshape(b * t, dm), b, PROMPT_TILE, jnp.zeros((b, HEAD_D, WIDTH), F32), moba_p,
        mk_p.reshape(b, n_mem, -1), mv_p.reshape(b, n_mem, -1), jnp.zeros((b, CONV_W - 1, dff), F32))

    pool_kt = cache_k[l].transpose(0, 2, 3, 1)
    pool_vt = cache_v[l].transpose(0, 2, 3, 1)
    per_seq = lambda a: a.reshape(WIDTH, db, ds).transpose(1, 0, 2)

    def moba_s(q, k, ksum, kt, vt, vtb):
        sel = _sample_select(q, _page_ksum(pool_kt, page_table), ds)
        o = _sample_attend(q, per_seq(kt).reshape(db * H_MB, HEAD_D, ds), per_seq(vt).reshape(db * H_MB, HEAD_D, ds),
                           pool_kt, pool_vt, page_table, sel.reshape(-1), rel_bias, ds)
        return o.reshape(db, H_MB, ds, HEAD_D).transpose(0, 2, 1, 3).reshape(db * ds, WIDTH)

    y_s, kt_s, vt_s, s_s, c_s = layer(
        x_sample.reshape(db * ds, dm), db, ds, _state_to_rows(state_hgrn[l]), moba_s,
        cache_mem_k[l].reshape(db, n_mem, -1), cache_mem_v[l].reshape(db, n_mem, -1), state_conv[l])

    hd = lambda a, bb, tt: a.reshape(bb, H_MB, HEAD_D, tt).transpose(0, 3, 1, 2)[None]
    return (y_p.reshape(b, t, dm), y_s.reshape(db, ds, dm),
            hd(kt_p, b, t), hd(vt_p, b, t), hd(per_seq(kt_s), db, ds), hd(per_seq(vt_s), db, ds),
            s_p[None], s_s[None], c_p[None], c_s[None],
            mk_p.reshape(1, b, n_mem, H_MEM, MEM_HD), mv_p.reshape(1, b, n_mem, H_MEM, MEM_HD))
```

```python
import functools
import math

import numpy as np
import jax
import jax.numpy as jnp
from jax import lax
from jax.experimental import pallas as pl
from jax.experimental.pallas import tpu as pltpu

F32 = jnp.float32
BF16 = jnp.bfloat16
EPS = 1e-6

H_HG = 8
H_MB = 8
HEAD_D = 64
WIDTH = 512
MB_BLOCK = 256
MB_TOPK = 3
NUM_BUCKETS = 32
MAX_DISTANCE = 8192
H_MEM = 4
MEM_HD = 128
CONV_W = 3

LANES = 128
SUBLANES = 8
VMEM_LIMIT = 56 * 1024 * 1024

HG_SUB = 16
HG_GROUP = 128
HG_SEQS = 4
HG_TILE = 128


def _cparams(*sem):
    return pltpu.CompilerParams(dimension_semantics=sem, vmem_limit_bytes=VMEM_LIMIT)


def _const_spec(shape):
    nd = len(shape)
    return pl.BlockSpec(shape, lambda *_: (0,) * nd, pipeline_mode=pl.Buffered(1))


def _group_ones(width, group):
    i = np.arange(width) // group
    return (i[:, None] == i[None, :]).astype(np.float32)


def _split_dot(x, ones_bf16):
    hi = x.astype(BF16)
    lo = (x - hi.astype(F32)).astype(BF16)
    return (jnp.dot(hi, ones_bf16, preferred_element_type=F32)
            + jnp.dot(lo, ones_bf16, preferred_element_type=F32))


def _seg_rmsnorm(x, gain, ones_bf16, seg):
    ms = _split_dot(x * x, ones_bf16) * (1.0 / seg)
    return x * lax.rsqrt(ms + EPS) * gain


def _rmsnorm_rows(x, gain):
    ms = jnp.mean(x * x, axis=-1, keepdims=True)
    return x * lax.rsqrt(ms + EPS) * gain


def _inproj_kernel(x_ref, gmix_ref, w_ref, wvt_ref, lbl_ref, qn_ref, kn_ref, ones_ref,
                   qhg_ref, khg_ref, ihg_ref, logf_ref, g_ref, qmb_ref, kmb_ref, ksum_ref, kt_ref, vt_ref, vtb_ref):
    xb = _rmsnorm_rows(x_ref[...], gmix_ref[...]).astype(BF16)

    def proj(i):
        return jnp.dot(xb, w_ref[:, i * WIDTH:(i + 1) * WIDTH], preferred_element_type=F32)

    qhg_ref[...] = proj(0)
    f_logit = proj(1)
    logits = lbl_ref[...]
    e = jnp.exp(logits - jnp.max(logits, axis=0, keepdims=True))
    lb = e[0:1] / jnp.sum(e, axis=0, keepdims=True)
    logf_ref[...] = jnp.log(lb + (1.0 - lb) * jax.nn.sigmoid(f_logit))
    khg_ref[...] = (1.0 - lb) * jax.nn.sigmoid(-f_logit)
    ihg_ref[...] = proj(2)
    g_ref[...] = proj(3)
    ones = ones_ref[...]
    qmb_ref[...] = _seg_rmsnorm(proj(4), qn_ref[...], ones, HEAD_D)
    k = _seg_rmsnorm(proj(5), kn_ref[...], ones, HEAD_D)
    kmb_ref[...] = k.astype(BF16)
    for j in range(ksum_ref.shape[0]):
        rows = k.shape[0] // ksum_ref.shape[0]
        ksum_ref[j] = jnp.sum(k[j * rows:(j + 1) * rows], axis=0, keepdims=True)
    kt_ref[0] = k.T
    vt = lax.dot_general(wvt_ref[...], xb, (((1,), (1,)), ((), ())), preferred_element_type=F32)
    vt_ref[0] = vt
    vtb_ref[0] = vt.astype(BF16)


def _inproj(x, gmix, w_bf16, lb_logits, qn, kn, batch, tm):
    n, d = x.shape
    t = n // batch
    assert t % tm == 0
    tpb = t // tm
    grp = min(tm, MB_BLOCK)
    assert tm % grp == 0
    tok = pl.BlockSpec((tm, WIDTH), lambda i: (i, 0))
    tr = pl.BlockSpec((1, WIDTH, tm), lambda i: (i // tpb, 0, i % tpb))
    outs = ([jax.ShapeDtypeStruct((n, WIDTH), F32)] * 6 + [jax.ShapeDtypeStruct((n, WIDTH), BF16)]
            + [jax.ShapeDtypeStruct((n // grp, 1, WIDTH), F32)]
            + [jax.ShapeDtypeStruct((batch, WIDTH, t), F32)] * 2 + [jax.ShapeDtypeStruct((batch, WIDTH, t), BF16)])
    ones = jnp.asarray(_group_ones(WIDTH, HEAD_D), BF16)
    w_main, wvt = w_bf16[:, :6 * WIDTH], w_bf16[:, 6 * WIDTH:].T
    return pl.pallas_call(
        _inproj_kernel,
        grid=(n // tm,),
        in_specs=[pl.BlockSpec((tm, d), lambda i: (i, 0)), _const_spec((1, d)), _const_spec(w_main.shape),
                  _const_spec(wvt.shape), _const_spec(lb_logits.shape), _const_spec((1, WIDTH)),
                  _const_spec((1, WIDTH)), _const_spec((WIDTH, WIDTH))],
        out_specs=[tok] * 7 + [pl.BlockSpec((tm // grp, 1, WIDTH), lambda i: (i, 0, 0)), tr, tr, tr],
        out_shape=outs,
        compiler_params=_cparams("parallel"),
        name="inproj",
    )(x, gmix, w_main, wvt, lb_logits, qn, kn, ones)


def _hgrn_kernel(q_ref, k_ref, v_ref, lf_ref, g_ref, s0_ref, gn_ref, ones_ref, gmask_ref,
                 o_ref, sfin_ref, st_scr, a_scr, o_scr, *, tc, nseq, sub):
    c = pl.program_id(1)
    ng = WIDTH // HG_GROUP
    hpg = HG_GROUP // HEAD_D
    gmask = gmask_ref[...]

    @pl.when(c == 0)
    def _():
        for bi in range(nseq):
            s0 = s0_ref[bi]
            for gi in range(ng):
                blk = s0[:, gi * HG_GROUP:(gi + 1) * HG_GROUP]
                st_scr[bi * ng + gi] = jnp.concatenate([blk] * hpg, axis=0) * gmask

    row = lax.broadcasted_iota(jnp.int32, (tc, WIDTH), 0) % sub
    for bi in range(nseq):
        a = lf_ref[bi]
        sh = 1
        while sh < sub:
            a = a + jnp.where(row >= sh, pltpu.roll(a, sh, 0), 0.0)
            sh *= 2
        a_scr[bi] = a

    ones = ones_ref[...]
    trow = lax.broadcasted_iota(jnp.int32, (sub, WIDTH), 0)

    def step(n, carry):
        off = pl.multiple_of(n * sub, sub)
        for bi in range(nseq):
            q = q_ref[bi, pl.ds(off, sub), :]
            k = k_ref[bi, pl.ds(off, sub), :]
            v = v_ref[bi, pl.ds(off, sub), :]
            al = a_scr[bi, pl.ds(off, sub), :]
            a_end = al[sub - 1:sub, :]
            qe = (q * jnp.exp(al)).astype(BF16)
            kd = (k * jnp.exp(a_end - al)).astype(BF16)
            dec = jnp.exp(a_end)
            vb = v.astype(BF16)
            parts = []
            for s in range(sub):
                a_s = a_scr[bi, pl.ds(off + s, 1), :]
                k_s = k_ref[bi, pl.ds(off + s, 1), :]
                e_s = q * k_s * jnp.exp(al - a_s)
                parts.append(jnp.where(trow >= s, e_s, 0.0))
            ecat = jnp.concatenate(parts, axis=0).astype(BF16)
            outs = []
            for gi in range(ng):
                sl = slice(gi * HG_GROUP, (gi + 1) * HG_GROUP)
                st = st_scr[bi * ng + gi]
                o_g = lax.dot_general(qe[:, sl], st.astype(BF16), (((1,), (1,)), ((), ())),
                                      preferred_element_type=F32)
                ag = jnp.dot(ecat[:, sl], ones, preferred_element_type=F32)
                for s in range(sub):
                    o_g = o_g + ag[s * sub:(s + 1) * sub, :] * v[s:s + 1, sl]
                outs.append(o_g)
                upd = lax.dot_general(vb[:, sl], kd[:, sl], (((0,), (0,)), ((), ())),
                                      preferred_element_type=F32)
                st_scr[bi * ng + gi] = st * dec[:, sl] + upd * gmask
            o_scr[bi, pl.ds(off, sub), :] = jnp.concatenate(outs, axis=1)
        return carry

    lax.fori_loop(0, tc // sub, step, 0)

    for bi in range(nseq):
        o = o_scr[bi]
        ms = jnp.concatenate(
            [_split_dot(o[:, gi * HG_GROUP:(gi + 1) * HG_GROUP] ** 2, ones) for gi in range(ng)], axis=1)
        g = g_ref[bi]
        o_ref[bi] = o * lax.rsqrt(ms * (1.0 / HEAD_D) + EPS) * gn_ref[...] * (g * jax.nn.sigmoid(g))

    @pl.when(c == pl.num_programs(1) - 1)
    def _():
        for bi in range(nseq):
            cols = []
            for gi in range(ng):
                st = st_scr[bi * ng + gi]
                acc = st[0:HEAD_D, :]
                for hh in range(1, hpg):
                    acc = acc + st[hh * HEAD_D:(hh + 1) * HEAD_D, :]
                cols.append(acc)
            sfin_ref[bi] = jnp.concatenate(cols, axis=1)


def _hgrn(q, k, v, logf, g, s0t, gn, batch, tc):
    n = q.shape[0]
    t = n // batch
    nseq = min(HG_SEQS, batch)
    sub = min(HG_SUB, tc)
    assert t % tc == 0 and tc % sub == 0 and batch % nseq == 0
    ng = WIDTH // HG_GROUP
    seq = lambda a: a.reshape(batch, t, WIDTH)
    tok = pl.BlockSpec((nseq, tc, WIDTH), lambda b, c: (b, c, 0))
    st_spec = pl.BlockSpec((nseq, HEAD_D, WIDTH), lambda b, c: (b, 0, 0))
    gm = _group_ones(HG_GROUP, HEAD_D)
    kern = functools.partial(_hgrn_kernel, tc=tc, nseq=nseq, sub=sub)
    o, st = pl.pallas_call(
        kern,
        grid=(batch // nseq, t // tc),
        in_specs=[tok] * 5 + [st_spec, _const_spec((1, WIDTH)), _const_spec((HG_GROUP, HG_GROUP)),
                              _const_spec((HG_GROUP, HG_GROUP))],
        out_specs=[tok, st_spec],
        out_shape=[jax.ShapeDtypeStruct((batch, t, WIDTH), F32), jax.ShapeDtypeStruct((batch, HEAD_D, WIDTH), F32)],
        scratch_shapes=[pltpu.VMEM((nseq * ng, HG_GROUP, HG_GROUP), F32),
                        pltpu.VMEM((nseq, tc, WIDTH), F32), pltpu.VMEM((nseq, tc, WIDTH), F32)],
        compiler_params=_cparams("parallel", "arbitrary"),
        name="hgrn",
    )(seq(q), seq(k), seq(v), seq(logf), seq(g), s0t, gn, jnp.asarray(gm, BF16), jnp.asarray(gm, F32))
    return o.reshape(n, WIDTH), st


NEG = -1e30


def _bucket_table(max_dist):
    max_exact = NUM_BUCKETS // 2
    d = np.arange(max_dist)
    nf = np.maximum(d, 1).astype(np.float32)
    large = max_exact + (np.log(nf / max_exact) / math.log(MAX_DISTANCE / max_exact)
                         * (NUM_BUCKETS - max_exact)).astype(np.int32)
    bucket = np.where(d < max_exact, d, np.minimum(large, NUM_BUCKETS - 1))
    assert np.all(np.diff(bucket) >= 0) and bucket[-1] == NUM_BUCKETS - 1
    first = [int(np.argmax(bucket >= kk)) for kk in range(NUM_BUCKETS)]
    return bucket, first


_BUCKET, _BUCKET_FIRST = _bucket_table(2 * MAX_DISTANCE)


def _bias_of_distance(d, rb_ref, h):
    acc = jnp.full(d.shape, rb_ref[0, h], F32)
    for kk in range(1, NUM_BUCKETS):
        acc = jnp.where(d >= _BUCKET_FIRST[kk], rb_ref[kk, h], acc)
    return acc


def _far_block_table(nb):
    big = 4 * MB_BLOCK
    tab = np.zeros((nb, 5), np.int32)
    for delta in range(nb):
        lo, hi = max(delta * MB_BLOCK - (MB_BLOCK - 1), 0), delta * MB_BLOCK + (MB_BLOCK - 1)
        b0 = int(_BUCKET[lo])
        ks = [kk for kk in range(b0 + 1, NUM_BUCKETS) if _BUCKET_FIRST[kk] <= hi]
        assert delta < 2 or len(ks) <= 2
        cs = [_BUCKET_FIRST[kk] - delta * MB_BLOCK for kk in ks[:2]] + [big, big]
        tab[delta] = [b0, min(b0 + 1, NUM_BUCKETS - 1), min(b0 + 2, NUM_BUCKETS - 1), cs[0], cs[1]]
    return tab


def _top_blocks(gate, n_valid, n_top, axis=0):
    nb = gate.shape[axis]
    j = lax.broadcasted_iota(jnp.int32, gate.shape, axis)
    g = jnp.where(j < n_valid, gate, -jnp.inf)
    sel = jnp.zeros(gate.shape, F32)
    idxs, oks = [], []
    for _ in range(n_top):
        mx = jnp.max(g, axis=axis, keepdims=True)
        idx = jnp.min(jnp.where(g == mx, j, nb), axis=axis, keepdims=True)
        ok = jnp.where(mx > -jnp.inf, 1.0, 0.0)
        pick = jnp.where(j == idx, ok, 0.0) > 0.0
        sel = jnp.where(pick, 1.0, sel)
        g = jnp.where(pick, -jnp.inf, g)
        idxs.append(idx)
        oks.append(ok)
    return idxs, oks, sel


def _pair_masks(shape):
    lane = lax.broadcasted_iota(jnp.int32, shape, len(shape) - 1)
    return [lane < HEAD_D, lane >= HEAD_D]


def _moba_prompt_kernel(qi_ref, kj_ref, tab_ref, q_ref, k_ref, vt_ref, vtp_ref, ksum_ref, rb_ref, o_ref,
                        qm_scr, sel_scr, m_scr, l_scr, acc_scr, near_scr, alpha0_scr, p0_scr, alpha1_scr, p1_scr):
    p = pl.program_id(0)
    qi = qi_ref[p]
    kj = kj_ref[p]
    delta = qi - kj
    blk = MB_BLOCK
    nseq = q_ref.shape[0]
    dts = (lax.broadcasted_iota(jnp.int32, (blk, blk), 1)
           - lax.broadcasted_iota(jnp.int32, (blk, blk), 0))
    slots = ((alpha0_scr, p0_scr), (alpha1_scr, p1_scr))

    @pl.when(p == 0)
    def _():
        for h in range(H_MB):
            near_scr[h] = jnp.where(dts >= 0, _bias_of_distance(dts, rb_ref, h), NEG)
            near_scr[H_MB + h] = _bias_of_distance(dts + blk, rb_ref, h)

    @pl.when(kj == 0)
    def _():
        masks = _pair_masks((blk, LANES))
        for b in range(nseq):
            q = q_ref[b]
            kmean = ksum_ref[b] * (1.0 / blk)
            for h in range(H_MB):
                r = b * H_MB + h
                m_scr[r] = jnp.full((1, blk), NEG, F32)
                l_scr[r] = jnp.zeros((1, blk), F32)
                acc_scr[r] = jnp.zeros((HEAD_D, blk), F32)
                p1_scr[r] = jnp.zeros((blk, blk), BF16)
                alpha1_scr[r] = jnp.ones((1, blk), F32)
                sl = slice((h // 2) * LANES, (h // 2 + 1) * LANES)
                qh = jnp.where(masks[h % 2], q[:, sl], 0.0)
                qm_scr[r] = (qh * (HEAD_D ** -0.5)).T.astype(BF16)
                gate = lax.dot_general(kmean[:, sl], qh, (((1,), (1,)), ((), ())),
                                       precision=lax.Precision.HIGHEST, preferred_element_type=F32)
                _, _, sel = _top_blocks(gate, qi, MB_TOPK)
                sel_scr[r] = jnp.where(sel > 0.0, 0.0, NEG)

    def attend(b, bias_of, par, own_block=False):
        hb = b * H_MB
        (a_cur, p_cur), (a_prv, p_prv) = slots[par], slots[1 - par]
        for h in range(H_MB):
            sl = slice((h // 2) * LANES, (h // 2 + 1) * LANES)
            s_all = jnp.dot(k_ref[b, :, sl], qm_scr[hb + h], preferred_element_type=F32)
            for qh in range(blk // LANES):
                ql = slice(qh * LANES, (qh + 1) * LANES)
                s = s_all[:, ql] + bias_of(b, h, qh)
                m_old = m_scr[hb + h, :, ql]
                m_new = jnp.maximum(m_old, jnp.max(s, axis=0, keepdims=True))
                alpha = jnp.exp(m_old - m_new)
                pexp = jnp.exp(s - m_new)
                l_scr[hb + h, :, ql] = alpha * l_scr[hb + h, :, ql] + jnp.sum(pexp, axis=0, keepdims=True)
                m_scr[hb + h, :, ql] = m_new
                a_cur[hb + h, :, ql] = alpha
                p_cur[hb + h, :, ql] = pexp.astype(BF16)
        for a_scr, p_scr, v_ref in ((a_prv, p_prv, vtp_ref),) + (((a_cur, p_cur, vt_ref),) if own_block else ()):
            for h in range(H_MB):
                hs = slice(h * HEAD_D, (h + 1) * HEAD_D)
                pv = jnp.dot(v_ref[b, hs, :], p_scr[hb + h], preferred_element_type=F32)
                acc_scr[hb + h] = a_scr[hb + h] * acc_scr[hb + h] + pv

    def for_each_parity(cond, bias_of, own_block=False):
        for par in range(2):
            @pl.when(cond & (kj % 2 == par))
            def _(par=par):
                for b in range(nseq):
                    attend(b, bias_of, par, own_block)

    def selrow(b, h, qh):
        return sel_scr[b * H_MB + h, pl.ds(kj, 1), :][:, qh * LANES:(qh + 1) * LANES]

    has_steps = tab_ref[delta, 3] < 2 * blk

    for_each_parity(delta == 0, lambda b, h, qh: near_scr[h, :, qh * LANES:(qh + 1) * LANES], own_block=True)

    for_each_parity(delta == 1,
                    lambda b, h, qh: near_scr[H_MB + h, :, qh * LANES:(qh + 1) * LANES] + selrow(b, h, qh))

    for_each_parity((delta >= 2) & jnp.logical_not(has_steps),
                    lambda b, h, qh: rb_ref[tab_ref[delta, 0], h] + selrow(b, h, qh))

    dts0 = dts[:, :LANES]

    def stepped_bias(b, h, qh):
        sr = selrow(b, h, qh)
        r0 = rb_ref[tab_ref[delta, 0], h] + sr
        r1 = rb_ref[tab_ref[delta, 1], h] + sr
        r2 = rb_ref[tab_ref[delta, 2], h] + sr
        m1 = dts0 >= tab_ref[delta, 3] - qh * LANES
        m2 = dts0 >= tab_ref[delta, 4] - qh * LANES
        return jnp.where(m2, r2, jnp.where(m1, r1, r0))

    for_each_parity((delta >= 2) & has_steps, stepped_bias)

    @pl.when(delta == 0)
    def _():
        for b in range(nseq):
            hb = b * H_MB
            o_ref[b] = jnp.concatenate([acc_scr[hb + h] / l_scr[hb + h] for h in range(H_MB)], axis=0).T


def _moba_prompt(q, k, vt, ksum, rel_bias, batch):
    n = q.shape[0]
    t = n // batch
    assert t % MB_BLOCK == 0
    nb = t // MB_BLOCK
    qi = np.array([i for i in range(nb) for _ in range(i + 1)], np.int32)
    kj = np.array([j for i in range(nb) for j in range(i + 1)], np.int32)
    tab = _far_block_table(nb)
    blk = MB_BLOCK
    rows = batch * H_MB
    grid_spec = pltpu.PrefetchScalarGridSpec(
        num_scalar_prefetch=3,
        grid=(len(qi),),
        in_specs=[
            pl.BlockSpec((batch, blk, WIDTH), lambda p, qi, kj, tab: (0, qi[p], 0)),
            pl.BlockSpec((batch, blk, WIDTH), lambda p, qi, kj, tab: (0, kj[p], 0)),
            pl.BlockSpec((batch, WIDTH, blk), lambda p, qi, kj, tab: (0, 0, kj[p])),
            pl.BlockSpec((batch, WIDTH, blk), lambda p, qi, kj, tab: (0, 0, jnp.maximum(kj[p] - 1, 0))),
            pl.BlockSpec((batch, nb, WIDTH), lambda p, qi, kj, tab: (0, 0, 0)),
            pl.BlockSpec(memory_space=pltpu.SMEM),
        ],
        out_specs=pl.BlockSpec((batch, blk, WIDTH), lambda p, qi, kj, tab: (0, qi[p], 0)),
        scratch_shapes=[pltpu.VMEM((rows, LANES, blk), BF16), pltpu.VMEM((rows, nb, blk), F32),
                        pltpu.VMEM((rows, 1, blk), F32), pltpu.VMEM((rows, 1, blk), F32),
                        pltpu.VMEM((rows, HEAD_D, blk), F32), pltpu.VMEM((2 * H_MB, blk, blk), F32),
                        pltpu.VMEM((rows, 1, blk), F32), pltpu.VMEM((rows, blk, blk), BF16),
                        pltpu.VMEM((rows, 1, blk), F32), pltpu.VMEM((rows, blk, blk), BF16)],
    )
    seq = lambda a: a.reshape(batch, t, WIDTH)
    return pl.pallas_call(
        _moba_prompt_kernel,
        grid_spec=grid_spec,
        out_shape=jax.ShapeDtypeStruct((batch, t, WIDTH), F32),
        compiler_params=_cparams("arbitrary"),
        name="moba_prompt",
    )(jnp.asarray(qi), jnp.asarray(kj), jnp.asarray(tab), seq(q), seq(k), vt, vt, ksum, rel_bias).reshape(n, WIDTH)


SAMPLE_PAGES_PER_STEP = 32


def _page_ksum_kernel(pt_ref, *refs, ppb):
    pages, out_ref = refs[:-1], refs[-1]
    j = pl.program_id(1)
    nblk = out_ref.shape[2]
    lane = lax.broadcasted_iota(jnp.int32, (WIDTH, nblk), 1)

    @pl.when(j == 0)
    def _():
        out_ref[...] = jnp.zeros(out_ref.shape, F32)

    acc = out_ref[0]
    for i in range(0, len(pages), ppb):
        blk = pages[i][0].reshape(WIDTH, pages[i].shape[-1])
        for e in range(1, ppb):
            blk = blk + pages[i + e][0].reshape(blk.shape)
        col = jnp.sum(blk, axis=1, keepdims=True)
        acc = jnp.where(lane == j * (len(pages) // ppb) + i // ppb, col, acc)
    out_ref[0] = acc


def _page_ksum(pool_t, page_table):
    db, n_pages = page_table.shape
    page = pool_t.shape[-1]
    ppb = MB_BLOCK // page
    pps = min(SAMPLE_PAGES_PER_STEP, n_pages)
    assert MB_BLOCK % page == 0 and n_pages % pps == 0 and pps % ppb == 0
    nblk = n_pages // ppb
    specs = [pl.BlockSpec((1,) + pool_t.shape[1:], (lambda b, j, pt, i=i: (pt[b, j * pps + i], 0, 0, 0)))
             for i in range(pps)]
    grid_spec = pltpu.PrefetchScalarGridSpec(
        num_scalar_prefetch=1, grid=(db, n_pages // pps), in_specs=specs,
        out_specs=pl.BlockSpec((1, WIDTH, nblk), lambda b, j, pt: (b, 0, 0)))
    return pl.pallas_call(
        functools.partial(_page_ksum_kernel, ppb=ppb), grid_spec=grid_spec,
        out_shape=jax.ShapeDtypeStruct((db, WIDTH, nblk), F32),
        compiler_params=_cparams("parallel", "arbitrary"), name="page_ksum",
    )(page_table, *([pool_t] * pps))


def _sample_select_kernel(q_ref, ksum_ref, idx_ref):
    q = q_ref[...]
    kmean_t = ksum_ref[0] * (1.0 / MB_BLOCK)
    nblk = kmean_t.shape[1]
    cols = []
    for h in range(H_MB):
        hs = slice(h * HEAD_D, (h + 1) * HEAD_D)
        gate = jnp.dot(q[:, hs], kmean_t[hs, :], precision=lax.Precision.HIGHEST,
                       preferred_element_type=F32)
        idxs, _, _ = _top_blocks(gate, nblk, MB_TOPK, axis=1)
        cols.extend(idxs)
    idx_ref[0] = jnp.concatenate(cols, axis=1)


def _sample_select(q, ksum_t, ds):
    db, _, nblk = ksum_t.shape
    assert nblk >= MB_TOPK
    return pl.pallas_call(
        _sample_select_kernel, grid=(db,),
        in_specs=[pl.BlockSpec((ds, WIDTH), lambda b: (b, 0)), pl.BlockSpec((1, WIDTH, nblk), lambda b: (b, 0, 0))],
        out_specs=pl.BlockSpec((1, ds, H_MB * MB_TOPK), lambda b: (b, 0, 0)),
        out_shape=jax.ShapeDtypeStruct((db, ds, H_MB * MB_TOPK), jnp.int32),
        compiler_params=_cparams("parallel"), name="sample_select",
    )(q, ksum_t)


def _sample_attend_kernel(pt_ref, sel_ref, q_ref, kn_ref, vn_ref, rb_ref, kpool_ref, vpool_ref, o_ref,
                          kbuf, vbuf, sems, *, ds, page, past_len):
    ppb = MB_BLOCK // page
    ntile = MB_TOPK * ppb
    step = pl.program_id(0)
    nstep = pl.num_programs(0)

    def copies(st, slot):
        b, h = st // H_MB, st % H_MB
        out = []
        for q in range(ds):
            for r in range(MB_TOPK):
                blk = sel_ref[((b * ds + q) * H_MB + h) * MB_TOPK + r]
                for e in range(ppb):
                    pg = pt_ref[b, blk * ppb + e]
                    dst = pl.ds((r * ppb + e) * page, page)
                    out.append(pltpu.make_async_copy(kpool_ref.at[pg, h], kbuf.at[slot, q, :, dst], sems.at[0, slot]))
                    out.append(pltpu.make_async_copy(vpool_ref.at[pg, h], vbuf.at[slot, q, :, dst], sems.at[1, slot]))
        return out

    @pl.when(step == 0)
    def _():
        for cp in copies(step, 0):
            cp.start()

    @pl.when(step + 1 < nstep)
    def _():
        for cp in copies(step + 1, (step + 1) % 2):
            cp.start()

    slot = step % 2
    for qq in range(ds):
        for tile in range(ntile):
            dst = pl.ds(tile * page, page)
            pltpu.make_async_copy(kpool_ref.at[0, 0], kbuf.at[slot, qq, :, dst], sems.at[0, slot]).wait()
            pltpu.make_async_copy(vpool_ref.at[0, 0], vbuf.at[slot, qq, :, dst], sems.at[1, slot]).wait()

    b, h = step // H_MB, step % H_MB
    q = (q_ref[0] * (HEAD_D ** -0.5)).astype(BF16)
    rowi = lax.broadcasted_iota(jnp.int32, (ds, ntile * page), 0)
    s_sel = jnp.zeros((ds, ntile * page), F32)
    kpos = jnp.zeros((ds, ntile * page), jnp.int32)
    lane = lax.broadcasted_iota(jnp.int32, (ds, page), 1)
    for qq in range(ds):
        res = jnp.dot(q, kbuf[slot, qq].astype(BF16), preferred_element_type=F32)
        s_sel = jnp.where(rowi == qq, res, s_sel)
        pieces = []
        for r in range(MB_TOPK):
            blk = sel_ref[((b * ds + qq) * H_MB + h) * MB_TOPK + r]
            for e in range(ppb):
                pieces.append(blk * MB_BLOCK + e * page + lane)
        kpos = jnp.where(rowi == qq, jnp.concatenate(pieces, axis=1), kpos)
    q_pos = past_len + lax.broadcasted_iota(jnp.int32, (ds, 1), 0)
    s_sel = s_sel + _bias_of_distance(q_pos - kpos, rb_ref, h)
    own = lax.broadcasted_iota(jnp.int32, (ds, ds), 1)
    qrow = lax.broadcasted_iota(jnp.int32, (ds, ds), 0)
    s_own = jnp.dot(q, kn_ref[0].astype(BF16), preferred_element_type=F32) \
        + _bias_of_distance(qrow - own, rb_ref, h)
    s_own = jnp.where(own <= qrow, s_own, NEG)
    m = jnp.maximum(jnp.max(s_sel, axis=1, keepdims=True), jnp.max(s_own, axis=1, keepdims=True))
    p_sel = jnp.exp(s_sel - m)
    p_own = jnp.exp(s_own - m)
    den = jnp.sum(p_sel, axis=1, keepdims=True) + jnp.sum(p_own, axis=1, keepdims=True)
    nt_dims = (((1,), (1,)), ((), ()))
    o = lax.dot_general(p_own.astype(BF16), vn_ref[0].astype(BF16), nt_dims, preferred_element_type=F32)
    rowo = lax.broadcasted_iota(jnp.int32, (ds, HEAD_D), 0)
    p_sel_b = p_sel.astype(BF16)
    for qq in range(ds):
        res = lax.dot_general(p_sel_b, vbuf[slot, qq].astype(BF16), nt_dims, preferred_element_type=F32)
        o = o + jnp.where(rowo == qq, res, 0.0)
    o_ref[0] = o / den


def _sample_attend(q, kt_new, vt_new, pool_kt, pool_vt, page_table, sel_flat, rel_bias, ds):
    db, n_pages = page_table.shape
    page = pool_kt.shape[-1]
    ppb = MB_BLOCK // page
    past_len = n_pages * page
    assert past_len % MB_BLOCK == 0
    ntile = MB_TOPK * ppb
    qh = q.reshape(db, ds, H_MB, HEAD_D).transpose(0, 2, 1, 3).reshape(db * H_MB, ds, HEAD_D)
    per_head = lambda shape: pl.BlockSpec((1,) + shape, lambda s, pt, sel: (s, 0, 0))
    grid_spec = pltpu.PrefetchScalarGridSpec(
        num_scalar_prefetch=2, grid=(db * H_MB,),
        in_specs=[per_head((ds, HEAD_D)), per_head((HEAD_D, ds)), per_head((HEAD_D, ds)),
                  pl.BlockSpec(memory_space=pltpu.SMEM), pl.BlockSpec(memory_space=pl.ANY),
                  pl.BlockSpec(memory_space=pl.ANY)],
        out_specs=per_head((ds, HEAD_D)),
        scratch_shapes=[pltpu.VMEM((2, ds, HEAD_D, ntile * page), F32), pltpu.VMEM((2, ds, HEAD_D, ntile * page), F32),
                        pltpu.SemaphoreType.DMA((2, 2))])
    kern = functools.partial(_sample_attend_kernel, ds=ds, page=page, past_len=past_len)
    return pl.pallas_call(
        kern, grid_spec=grid_spec, out_shape=jax.ShapeDtypeStruct((db * H_MB, ds, HEAD_D), F32),
        compiler_params=_cparams("arbitrary"), name="sample_attend",
    )(page_table, sel_flat, qh, kt_new, vt_new, rel_bias, pool_kt, pool_vt)


def _memkv_kernel(m_ref, g_ref, w_ref, kn_ref, ones_ref, k_ref, v_ref):
    xb = _rmsnorm_rows(m_ref[...], g_ref[...]).astype(BF16)
    wk = w_ref.shape[1] // 2
    k = jnp.dot(xb, w_ref[:, :wk], preferred_element_type=F32)
    k_ref[...] = _seg_rmsnorm(k, kn_ref[...], ones_ref[...], MEM_HD)
    v_ref[...] = jnp.dot(xb, w_ref[:, wk:], preferred_element_type=F32)


def _memkv(mem, g, w_bf16, kn, tm):
    n, d = mem.shape
    wk = w_bf16.shape[1] // 2
    out = pl.BlockSpec((tm, wk), lambda i: (i, 0))
    return pl.pallas_call(
        _memkv_kernel, grid=(n // tm,),
        in_specs=[pl.BlockSpec((tm, d), lambda i: (i, 0)), _const_spec((1, d)), _const_spec(w_bf16.shape),
                  _const_spec((1, wk)), _const_spec((wk, wk))],
        out_specs=[out, out], out_shape=[jax.ShapeDtypeStruct((n, wk), F32)] * 2,
        compiler_params=_cparams("parallel"), name="mem_kv",
    )(mem, g, w_bf16, kn, jnp.asarray(_group_ones(wk, MEM_HD), BF16))


def _mixmem_kernel(x_ref, hg_ref, omb_ref, wout_ref, gmem_ref, wq_ref, qn_ref, ones_ref, mk_ref, mv_ref, wo_ref,
                   h_ref, *, nseq):
    mix = (jnp.dot(hg_ref[...].astype(BF16), wout_ref[:WIDTH, :], preferred_element_type=F32)
           + jnp.dot(omb_ref[...].astype(BF16), wout_ref[WIDTH:, :], preferred_element_type=F32))
    h1 = x_ref[...] + mix
    hn = _rmsnorm_rows(h1, gmem_ref[...]).astype(BF16)
    q = _seg_rmsnorm(jnp.dot(hn, wq_ref[...], preferred_element_type=F32), qn_ref[...], ones_ref[...], MEM_HD)
    tm = q.shape[0] // nseq
    per_seq = []
    for i in range(nseq):
        qb = q[i * tm:(i + 1) * tm].astype(BF16)
        mk = mk_ref[i].astype(BF16)
        mv = mv_ref[i].astype(BF16)
        outs = []
        for h in range(H_MEM):
            sl = slice(h * MEM_HD, (h + 1) * MEM_HD)
            s = lax.dot_general(qb[:, sl], mk[:, sl], (((1,), (1,)), ((), ())),
                                preferred_element_type=F32) * (MEM_HD ** -0.5)
            e = jnp.exp(s - jnp.max(s, axis=-1, keepdims=True))
            p = e / jnp.sum(e, axis=-1, keepdims=True)
            outs.append(jnp.dot(p.astype(BF16), mv[:, sl], preferred_element_type=F32))
        per_seq.append(jnp.concatenate(outs, axis=1))
    o = jnp.concatenate(per_seq, axis=0).astype(BF16)
    h_ref[...] = h1 + jnp.dot(o, wo_ref[...], preferred_element_type=F32)


def _tail_seqs(batch, nt):
    return math.gcd(batch, SHORT_SEQS) if nt == 1 else 1


def _mixmem(x, hg, omb, wout, gmem, wq, qn, mk, mv, wo, batch, tm):
    n, d = x.shape
    t = n // batch
    assert t % tm == 0
    nt = t // tm
    nseq = _tail_seqs(batch, nt)
    n_mem, wm = mk.shape[1:]
    tok = lambda w: pl.BlockSpec((nseq * tm, w), lambda b, i: (b * nt + i, 0))
    mem = pl.BlockSpec((nseq, n_mem, wm), lambda b, i: (b, 0, 0))
    return pl.pallas_call(
        functools.partial(_mixmem_kernel, nseq=nseq), grid=(batch // nseq, nt),
        in_specs=[tok(d), tok(WIDTH), tok(WIDTH), _const_spec(wout.shape), _const_spec((1, d)),
                  _const_spec(wq.shape), _const_spec((1, wm)), _const_spec((wm, wm)), mem, mem,
                  _const_spec(wo.shape)],
        out_specs=tok(d), out_shape=jax.ShapeDtypeStruct((n, d), F32),
        compiler_params=_cparams("parallel", "parallel"), name="mix_mem",
    )(x, hg, omb, wout, gmem, wq, qn, jnp.asarray(_group_ones(wm, MEM_HD), BF16), mk, mv, wo)


FFN_CHUNKS = 1
SHORT_SEQS = 8


def _ffn_kernel(h_ref, g_ref, wup_ref, cw_ref, cb_ref, wdn_ref, prev_ref, y_ref, cst_ref, carry_scr, *, dff, nseq):
    c = pl.program_id(1)

    @pl.when(c == 0)
    def _():
        carry_scr[...] = prev_ref[...]

    h = h_ref[...]
    hn = _rmsnorm_rows(h, g_ref[...]).astype(BF16)
    tm = h.shape[0] // nseq
    fc = dff // FFN_CHUNKS
    row = lax.broadcasted_iota(jnp.int32, (nseq * tm, fc), 0) % tm
    acc = h
    for ci in range(FFN_CHUNKS):
        sl = slice(ci * fc, (ci + 1) * fc)
        u = jnp.dot(hn, wup_ref[:, sl], preferred_element_type=F32)
        v = jnp.dot(hn, wup_ref[:, dff + ci * fc:dff + (ci + 1) * fc], preferred_element_type=F32)

        def carried(r):
            if nseq == 1:
                return carry_scr[0, r:r + 1, sl]
            return jnp.concatenate([jnp.broadcast_to(carry_scr[i, r:r + 1, sl], (tm, fc)) for i in range(nseq)], axis=0)

        u1 = jnp.where(row == 0, carried(1), pltpu.roll(u, 1, 0))
        u2 = jnp.where(row == 0, carried(0), jnp.where(row == 1, carried(1), pltpu.roll(u, 2, 0)))
        cw = cw_ref[:, sl]
        conv = cb_ref[:, sl] + u2 * cw[0:1] + u1 * cw[1:2] + u * cw[2:3]
        act = 0.5 * conv * (1.0 + lax.erf(conv * (2.0 ** -0.5))) * v
        acc = acc + jnp.dot(act.astype(BF16), wdn_ref[sl, :], preferred_element_type=F32)
        for i in range(nseq):
            last = u[(i + 1) * tm - (CONV_W - 1):(i + 1) * tm]
            carry_scr[i, :, sl] = last
            cst_ref[i, :, sl] = last
    y_ref[...] = acc


def _ffn(h, g, wup, cw, cb, wdn, prev, batch, tm):
    n, d = h.shape
    t = n // batch
    assert t % tm == 0 and tm >= CONV_W - 1
    nt = t // tm
    nseq = _tail_seqs(batch, nt)
    dff = wdn.shape[0]
    assert dff % (FFN_CHUNKS * LANES) == 0
    tok = pl.BlockSpec((nseq * tm, d), lambda b, i: (b * nt + i, 0))
    st = pl.BlockSpec((nseq, CONV_W - 1, dff), lambda b, i: (b, 0, 0))
    return pl.pallas_call(
        functools.partial(_ffn_kernel, dff=dff, nseq=nseq), grid=(batch // nseq, nt),
        in_specs=[tok, _const_spec((1, d)), _const_spec(wup.shape), _const_spec(cw.shape), _const_spec((1, dff)),
                  _const_spec(wdn.shape), st],
        out_specs=[tok, st],
        out_shape=[jax.ShapeDtypeStruct((n, d), F32), jax.ShapeDtypeStruct((batch, CONV_W - 1, dff), F32)],
        scratch_shapes=[pltpu.VMEM((nseq, CONV_W - 1, dff), F32)],
        compiler_params=_cparams("parallel", "arbitrary"), name="conv_ffn",
    )(h, g, wup, cw, cb, wdn, prev)


PROMPT_TILE = 512


def _state_to_rows(s):
    b = s.shape[0]
    return s.transpose(0, 3, 1, 2).reshape(b, HEAD_D, WIDTH)


def _rows_to_state(st):
    b = st.shape[0]
    return st.reshape(b, HEAD_D, H_HG, HEAD_D).transpose(0, 2, 3, 1)


def kernel(x_prompt, x_sample, cache_k, cache_v, page_table, state_hgrn, state_conv, cache_mem_k, cache_mem_v, mem_prompt, norm_mix, w_in, hg_lb_logits, hg_out_norm, mb_q_norm, mb_k_norm, rel_bias, w_out, norm_mem, norm_mem_src, w_mem_q, w_mem_kv, mem_q_norm, mem_k_norm, w_mem_o, norm_ffn, w_up, conv_w, conv_b, w_down):
    assert norm_mix.shape[0] == 1, "one layer"
    l = 0
    b, t, dm = x_prompt.shape
    db, ds, _ = x_sample.shape
    n_mem = mem_prompt.shape[1]
    dff = w_down.shape[1]
    row = lambda a: a[None]
    w_in_b, w_out_b = w_in[l].astype(BF16), w_out[l].astype(BF16)
    w_q_b, w_kv_b, w_o_b = w_mem_q[l].astype(BF16), w_mem_kv[l].astype(BF16), w_mem_o[l].astype(BF16)
    w_up_b, w_dn_b = w_up[l].astype(BF16), w_down[l].astype(BF16)
    qn, kn = row(jnp.tile(mb_q_norm[l], H_MB)), row(jnp.tile(mb_k_norm[l], H_MB))
    gn = row(jnp.tile(hg_out_norm[l], H_HG))
    mqn, mkn = row(jnp.tile(mem_q_norm[l], H_MEM)), row(jnp.tile(mem_k_norm[l], H_MEM))

    def layer(x, batch, tile, s0t, moba, mk, mv, prev):
        n = x.shape[0]
        tm = min(PROMPT_TILE, n)
        slabs = batch if (n // batch) % tm == 0 else 1
        qhg, khg, ihg, logf, g, qmb, kmb, ksum, kt, vt, vtb = _inproj(
            x, row(norm_mix[l]), w_in_b, hg_lb_logits, qn, kn, slabs, tm)
        hg, st = _hgrn(qhg, khg, ihg, logf, g, s0t, gn, batch, min(tile, HG_TILE))
        omb = moba(qmb, kmb, ksum, kt, vt, vtb)
        h = _mixmem(x, hg, omb, w_out_b, row(norm_mem[l]), w_q_b, mqn, mk, mv, w_o_b, batch, tile)
        y, cst = _ffn(h, row(norm_ffn[l]), w_up_b, conv_w[l], row(conv_b[l]), w_dn_b, prev, batch, tile)
        return y, kt, vt, _rows_to_state(st), cst

    mk_p, mv_p = _memkv(mem_prompt.reshape(b * n_mem, dm), row(norm_mem_src[l]), w_kv_b, mkn, n_mem)
    moba_p = lambda q, k, ksum, kt, vt, vtb: _moba_prompt(q, k, vtb, ksum.reshape(b, t // MB_BLOCK, WIDTH), rel_bias, b)
    y_p, kt_p, vt_p, s_p, c_p = layer(
        x_prompt.reshape(b * t, dm), b, PROMPT_TILE, jnp.zeros((b, HEAD_D, WIDTH), F32), moba_p,
        mk_p.reshape(b, n_mem, -1), mv_p.reshape(b, n_mem, -1), jnp.zeros((b, CONV_W - 1, dff), F32))

    pool_kt = cache_k[l].transpose(0, 2, 3, 1)
    pool_vt = cache_v[l].transpose(0, 2, 3, 1)
    per_seq = lambda a: a.reshape(WIDTH, db, ds).transpose(1, 0, 2)

    def moba_s(q, k, ksum, kt, vt, vtb):
        sel = _sample_select(q, _page_ksum(pool_kt, page_table), ds)
        o = _sample_attend(q, per_seq(kt).reshape(db * H_MB, HEAD_D, ds), per_seq(vt).reshape(db * H_MB, HEAD_D, ds),
                           pool_kt, pool_vt, page_table, sel.reshape(-1), rel_bias, ds)
        return o.reshape(db, H_MB, ds, HEAD_D).transpose(0, 2, 1, 3).reshape(db * ds, WIDTH)

    y_s, kt_s, vt_s, s_s, c_s = layer(
        x_sample.reshape(db * ds, dm), db, ds, _state_to_rows(state_hgrn[l]), moba_s,
        cache_mem_k[l].reshape(db, n_mem, -1), cache_mem_v[l].reshape(db, n_mem, -1), state_conv[l])

    hd = lambda a, bb, tt: a.reshape(bb, H_MB, HEAD_D, tt).transpose(0, 3, 1, 2)[None]
    return (y_p.reshape(b, t, dm), y_s.reshape(db, ds, dm),
            hd(kt_p, b, t), hd(vt_p, b, t), hd(per_seq(kt_s), db, ds), hd(per_seq(vt_s), db, ds),
            s_p[None], s_s[None], c_p[None], c_s[None],
            mk_p.reshape(1, b, n_mem, H_MEM, MEM_HD), mv_p.reshape(1, b, n_mem, H_MEM, MEM_HD))
```

```python
import functools
import math

import numpy as np
import jax
import jax.numpy as jnp
from jax import lax
from jax.experimental import pallas as pl
from jax.experimental.pallas import tpu as pltpu

F32 = jnp.float32
BF16 = jnp.bfloat16
EPS = 1e-6

H_HG = 8
H_MB = 8
HEAD_D = 64
WIDTH = 512
MB_BLOCK = 256
MB_TOPK = 3
NUM_BUCKETS = 32
MAX_DISTANCE = 8192
H_MEM = 4
MEM_HD = 128
CONV_W = 3

LANES = 128
SUBLANES = 8
VMEM_LIMIT = 56 * 1024 * 1024

HG_SUB = 16
HG_GROUP = 128
HG_SEQS = 4
HG_TILE = 128


def _cparams(*sem):
    return pltpu.CompilerParams(dimension_semantics=sem, vmem_limit_bytes=VMEM_LIMIT)


def _const_spec(shape):
    nd = len(shape)
    return pl.BlockSpec(shape, lambda *_: (0,) * nd, pipeline_mode=pl.Buffered(1))


def _group_ones(width, group):
    i = np.arange(width) // group
    return (i[:, None] == i[None, :]).astype(np.float32)


def _split_dot(x, ones_bf16):
    hi = x.astype(BF16)
    lo = (x - hi.astype(F32)).astype(BF16)
    return (jnp.dot(hi, ones_bf16, preferred_element_type=F32)
            + jnp.dot(lo, ones_bf16, preferred_element_type=F32))


def _seg_rmsnorm(x, gain, ones_bf16, seg):
    ms = _split_dot(x * x, ones_bf16) * (1.0 / seg)
    return x * lax.rsqrt(ms + EPS) * gain


def _rmsnorm_rows(x, gain):
    ms = jnp.mean(x * x, axis=-1, keepdims=True)
    return x * lax.rsqrt(ms + EPS) * gain


def _inproj_kernel(x_ref, gmix_ref, w_ref, wvt_ref, lbl_ref, qn_ref, kn_ref, ones_ref,
                   qhg_ref, khg_ref, ihg_ref, logf_ref, g_ref, qmb_ref, kmb_ref, ksum_ref, kt_ref, vt_ref, vtb_ref):
    xb = _rmsnorm_rows(x_ref[...], gmix_ref[...]).astype(BF16)

    def proj(i):
        return jnp.dot(xb, w_ref[:, i * WIDTH:(i + 1) * WIDTH], preferred_element_type=F32)

    qhg_ref[...] = proj(0)
    f_logit = proj(1)
    logits = lbl_ref[...]
    e = jnp.exp(logits - jnp.max(logits, axis=0, keepdims=True))
    lb = e[0:1] / jnp.sum(e, axis=0, keepdims=True)
    logf_ref[...] = jnp.log(lb + (1.0 - lb) * jax.nn.sigmoid(f_logit))
    khg_ref[...] = (1.0 - lb) * jax.nn.sigmoid(-f_logit)
    ihg_ref[...] = proj(2)
    g_ref[...] = proj(3)
    ones = ones_ref[...]
    qmb_ref[...] = _seg_rmsnorm(proj(4), qn_ref[...], ones, HEAD_D)
    k = _seg_rmsnorm(proj(5), kn_ref[...], ones, HEAD_D)
    kmb_ref[...] = k.astype(BF16)
    for j in range(ksum_ref.shape[0]):
        rows = k.shape[0] // ksum_ref.shape[0]
        ksum_ref[j] = jnp.sum(k[j * rows:(j + 1) * rows], axis=0, keepdims=True)
    kt_ref[0] = k.T
    vt = lax.dot_general(wvt_ref[...], xb, (((1,), (1,)), ((), ())), preferred_element_type=F32)
    vt_ref[0] = vt
    vtb_ref[0] = vt.astype(BF16)


def _inproj(x, gmix, w_bf16, lb_logits, qn, kn, batch, tm):
    n, d = x.shape
    t = n // batch
    assert t % tm == 0
    tpb = t // tm
    grp = min(tm, MB_BLOCK)
    assert tm % grp == 0
    tok = pl.BlockSpec((tm, WIDTH), lambda i: (i, 0))
    tr = pl.BlockSpec((1, WIDTH, tm), lambda i: (i // tpb, 0, i % tpb))
    outs = ([jax.ShapeDtypeStruct((n, WIDTH), F32)] * 6 + [jax.ShapeDtypeStruct((n, WIDTH), BF16)]
            + [jax.ShapeDtypeStruct((n // grp, 1, WIDTH), F32)]
            + [jax.ShapeDtypeStruct((batch, WIDTH, t), F32)] * 2 + [jax.ShapeDtypeStruct((batch, WIDTH, t), BF16)])
    ones = jnp.asarray(_group_ones(WIDTH, HEAD_D), BF16)
    w_main, wvt = w_bf16[:, :6 * WIDTH], w_bf16[:, 6 * WIDTH:].T
    return pl.pallas_call(
        _inproj_kernel,
        grid=(n // tm,),
        in_specs=[pl.BlockSpec((tm, d), lambda i: (i, 0)), _const_spec((1, d)), _const_spec(w_main.shape),
                  _const_spec(wvt.shape), _const_spec(lb_logits.shape), _const_spec((1, WIDTH)),
                  _const_spec((1, WIDTH)), _const_spec((WIDTH, WIDTH))],
        out_specs=[tok] * 7 + [pl.BlockSpec((tm // grp, 1, WIDTH), lambda i: (i, 0, 0)), tr, tr, tr],
        out_shape=outs,
        compiler_params=_cparams("parallel"),
        name="inproj",
    )(x, gmix, w_main, wvt, lb_logits, qn, kn, ones)


def _hgrn_kernel(q_ref, k_ref, v_ref, lf_ref, g_ref, s0_ref, gn_ref, ones_ref, gmask_ref,
                 o_ref, sfin_ref, st_scr, a_scr, o_scr, *, tc, nseq, sub):
    c = pl.program_id(1)
    ng = WIDTH // HG_GROUP
    hpg = HG_GROUP // HEAD_D
    gmask = gmask_ref[...]

    @pl.when(c == 0)
    def _():
        for bi in range(nseq):
            s0 = s0_ref[bi]
            for gi in range(ng):
                blk = s0[:, gi * HG_GROUP:(gi + 1) * HG_GROUP]
                st_scr[bi * ng + gi] = jnp.concatenate([blk] * hpg, axis=0) * gmask

    row = lax.broadcasted_iota(jnp.int32, (tc, WIDTH), 0) % sub
    for bi in range(nseq):
        a = lf_ref[bi]
        sh = 1
        while sh < sub:
            a = a + jnp.where(row >= sh, pltpu.roll(a, sh, 0), 0.0)
            sh *= 2
        a_scr[bi] = a

    ones = ones_ref[...]
    trow = lax.broadcasted_iota(jnp.int32, (sub, WIDTH), 0)

    def step(n, carry):
        off = pl.multiple_of(n * sub, sub)
        for bi in range(nseq):
            q = q_ref[bi, pl.ds(off, sub), :]
            k = k_ref[bi, pl.ds(off, sub), :]
            v = v_ref[bi, pl.ds(off, sub), :]
            al = a_scr[bi, pl.ds(off, sub), :]
            a_end = al[sub - 1:sub, :]
            qe = (q * jnp.exp(al)).astype(BF16)
            kd = (k * jnp.exp(a_end - al)).astype(BF16)
            dec = jnp.exp(a_end)
            vb = v.astype(BF16)
            parts = []
            for s in range(sub):
                a_s = a_scr[bi, pl.ds(off + s, 1), :]
                k_s = k_ref[bi, pl.ds(off + s, 1), :]
                e_s = q * k_s * jnp.exp(al - a_s)
                parts.append(jnp.where(trow >= s, e_s, 0.0))
            ecat = jnp.concatenate(parts, axis=0).astype(BF16)
            outs = []
            for gi in range(ng):
                sl = slice(gi * HG_GROUP, (gi + 1) * HG_GROUP)
                st = st_scr[bi * ng + gi]
                o_g = lax.dot_general(qe[:, sl], st.astype(BF16), (((1,), (1,)), ((), ())),
                                      preferred_element_type=F32)
                ag = jnp.dot(ecat[:, sl], ones, preferred_element_type=F32)
                for s in range(sub):
                    o_g = o_g + ag[s * sub:(s + 1) * sub, :] * v[s:s + 1, sl]
                outs.append(o_g)
                upd = lax.dot_general(vb[:, sl], kd[:, sl], (((0,), (0,)), ((), ())),
                                      preferred_element_type=F32)
                st_scr[bi * ng + gi] = st * dec[:, sl] + upd * gmask
            o_scr[bi, pl.ds(off, sub), :] = jnp.concatenate(outs, axis=1)
        return carry

    lax.fori_loop(0, tc // sub, step, 0)

    for bi in range(nseq):
        o = o_scr[bi]
        ms = jnp.concatenate(
            [_split_dot(o[:, gi * HG_GROUP:(gi + 1) * HG_GROUP] ** 2, ones) for gi in range(ng)], axis=1)
        g = g_ref[bi]
        o_ref[bi] = o * lax.rsqrt(ms * (1.0 / HEAD_D) + EPS) * gn_ref[...] * (g * jax.nn.sigmoid(g))

    @pl.when(c == pl.num_programs(1) - 1)
    def _():
        for bi in range(nseq):
            cols = []
            for gi in range(ng):
                st = st_scr[bi * ng + gi]
                acc = st[0:HEAD_D, :]
                for hh in range(1, hpg):
                    acc = acc + st[hh * HEAD_D:(hh + 1) * HEAD_D, :]
                cols.append(acc)
            sfin_ref[bi] = jnp.concatenate(cols, axis=1)


def _hgrn(q, k, v, logf, g, s0t, gn, batch, tc):
    n = q.shape[0]
    t = n // batch
    nseq = min(HG_SEQS, batch)
    sub = min(HG_SUB, tc)
    assert t % tc == 0 and tc % sub == 0 and batch % nseq == 0
    ng = WIDTH // HG_GROUP
    seq = lambda a: a.reshape(batch, t, WIDTH)
    tok = pl.BlockSpec((nseq, tc, WIDTH), lambda b, c: (b, c, 0))
    st_spec = pl.BlockSpec((nseq, HEAD_D, WIDTH), lambda b, c: (b, 0, 0))
    gm = _group_ones(HG_GROUP, HEAD_D)
    kern = functools.partial(_hgrn_kernel, tc=tc, nseq=nseq, sub=sub)
    o, st = pl.pallas_call(
        kern,
        grid=(batch // nseq, t // tc),
        in_specs=[tok] * 5 + [st_spec, _const_spec((1, WIDTH)), _const_spec((HG_GROUP, HG_GROUP)),
                              _const_spec((HG_GROUP, HG_GROUP))],
        out_specs=[tok, st_spec],
        out_shape=[jax.ShapeDtypeStruct((batch, t, WIDTH), F32), jax.ShapeDtypeStruct((batch, HEAD_D, WIDTH), F32)],
        scratch_shapes=[pltpu.VMEM((nseq * ng, HG_GROUP, HG_GROUP), F32),
                        pltpu.VMEM((nseq, tc, WIDTH), F32), pltpu.VMEM((nseq, tc, WIDTH), F32)],
        compiler_params=_cparams("parallel", "arbitrary"),
        name="hgrn",
    )(seq(q), seq(k), seq(v), seq(logf), seq(g), s0t, gn, jnp.asarray(gm, BF16), jnp.asarray(gm, F32))
    return o.reshape(n, WIDTH), st


NEG = -1e30


def _bucket_table(max_dist):
    max_exact = NUM_BUCKETS // 2
    d = np.arange(max_dist)
    nf = np.maximum(d, 1).astype(np.float32)
    large = max_exact + (np.log(nf / max_exact) / math.log(MAX_DISTANCE / max_exact)
                         * (NUM_BUCKETS - max_exact)).astype(np.int32)
    bucket = np.where(d < max_exact, d, np.minimum(large, NUM_BUCKETS - 1))
    assert np.all(np.diff(bucket) >= 0) and bucket[-1] == NUM_BUCKETS - 1
    first = [int(np.argmax(bucket >= kk)) for kk in range(NUM_BUCKETS)]
    return bucket, first


_BUCKET, _BUCKET_FIRST = _bucket_table(2 * MAX_DISTANCE)


def _bias_of_distance(d, rb_ref, h):
    acc = jnp.full(d.shape, rb_ref[0, h], F32)
    for kk in range(1, NUM_BUCKETS):
        acc = jnp.where(d >= _BUCKET_FIRST[kk], rb_ref[kk, h], acc)
    return acc


def _far_block_table(nb):
    big = 4 * MB_BLOCK
    tab = np.zeros((nb, 5), np.int32)
    for delta in range(nb):
        lo, hi = max(delta * MB_BLOCK - (MB_BLOCK - 1), 0), delta * MB_BLOCK + (MB_BLOCK - 1)
        b0 = int(_BUCKET[lo])
        ks = [kk for kk in range(b0 + 1, NUM_BUCKETS) if _BUCKET_FIRST[kk] <= hi]
        assert delta < 2 or len(ks) <= 2
        cs = [_BUCKET_FIRST[kk] - delta * MB_BLOCK for kk in ks[:2]] + [big, big]
        tab[delta] = [b0, min(b0 + 1, NUM_BUCKETS - 1), min(b0 + 2, NUM_BUCKETS - 1), cs[0], cs[1]]
    return tab


def _top_blocks(gate, n_valid, n_top, axis=0):
    nb = gate.shape[axis]
    j = lax.broadcasted_iota(jnp.int32, gate.shape, axis)
    g = jnp.where(j < n_valid, gate, -jnp.inf)
    sel = jnp.zeros(gate.shape, F32)
    idxs, oks = [], []
    for _ in range(n_top):
        mx = jnp.max(g, axis=axis, keepdims=True)
        idx = jnp.min(jnp.where(g == mx, j, nb), axis=axis, keepdims=True)
        ok = jnp.where(mx > -jnp.inf, 1.0, 0.0)
        pick = jnp.where(j == idx, ok, 0.0) > 0.0
        sel = jnp.where(pick, 1.0, sel)
        g = jnp.where(pick, -jnp.inf, g)
        idxs.append(idx)
        oks.append(ok)
    return idxs, oks, sel


def _pair_masks(shape):
    lane = lax.broadcasted_iota(jnp.int32, shape, len(shape) - 1)
    return [lane < HEAD_D, lane >= HEAD_D]


def _moba_prompt_kernel(qi_ref, kj_ref, tab_ref, q_ref, k_ref, vt_ref, vtp_ref, ksum_ref, rb_ref, o_ref,
                        qm_scr, sel_scr, m_scr, l_scr, acc_scr, near_scr, alpha0_scr, p0_scr, alpha1_scr, p1_scr):
    p = pl.program_id(0)
    qi = qi_ref[p]
    kj = kj_ref[p]
    delta = qi - kj
    blk = MB_BLOCK
    nseq = q_ref.shape[0]
    dts = (lax.broadcasted_iota(jnp.int32, (blk, blk), 1)
           - lax.broadcasted_iota(jnp.int32, (blk, blk), 0))
    slots = ((alpha0_scr, p0_scr), (alpha1_scr, p1_scr))

    @pl.when(p == 0)
    def _():
        for h in range(H_MB):
            near_scr[h] = jnp.where(dts >= 0, _bias_of_distance(dts, rb_ref, h), NEG)
            near_scr[H_MB + h] = _bias_of_distance(dts + blk, rb_ref, h)

    @pl.when(kj == 0)
    def _():
        masks = _pair_masks((blk, LANES))
        for b in range(nseq):
            q = q_ref[b]
            kmean = ksum_ref[b] * (1.0 / blk)
            for h in range(H_MB):
                r = b * H_MB + h
                m_scr[r] = jnp.full((1, blk), NEG, F32)
                l_scr[r] = jnp.zeros((1, blk), F32)
                acc_scr[r] = jnp.zeros((HEAD_D, blk), F32)
                p1_scr[r] = jnp.zeros((blk, blk), BF16)
                alpha1_scr[r] = jnp.ones((1, blk), F32)
                sl = slice((h // 2) * LANES, (h // 2 + 1) * LANES)
                qh = jnp.where(masks[h % 2], q[:, sl], 0.0)
                qm_scr[r] = (qh * (HEAD_D ** -0.5)).T.astype(BF16)
                gate = lax.dot_general(kmean[:, sl], qh, (((1,), (1,)), ((), ())),
                                       precision=lax.Precision.HIGHEST, preferred_element_type=F32)
                _, _, sel = _top_blocks(gate, qi, MB_TOPK)
                sel_scr[r] = jnp.where(sel > 0.0, 0.0, NEG)

    def attend(b, bias_of, par, own_block=False):
        hb = b * H_MB
        (a_cur, p_cur), (a_prv, p_prv) = slots[par], slots[1 - par]
        for h in range(H_MB):
            sl = slice((h // 2) * LANES, (h // 2 + 1) * LANES)
            s_all = jnp.dot(k_ref[b, :, sl], qm_scr[hb + h], preferred_element_type=F32)
            for qh in range(blk // LANES):
                ql = slice(qh * LANES, (qh + 1) * LANES)
                s = s_all[:, ql] + bias_of(b, h, qh)
                m_old = m_scr[hb + h, :, ql]
                m_new = jnp.maximum(m_old, jnp.max(s, axis=0, keepdims=True))
                alpha = jnp.exp(m_old - m_new)
                pexp = jnp.exp(s - m_new)
                l_scr[hb + h, :, ql] = alpha * l_scr[hb + h, :, ql] + jnp.sum(pexp, axis=0, keepdims=True)
                m_scr[hb + h, :, ql] = m_new
                a_cur[hb + h, :, ql] = alpha
                p_cur[hb + h, :, ql] = pexp.astype(BF16)
        for a_scr, p_scr, v_ref in ((a_prv, p_prv, vtp_ref),) + (((a_cur, p_cur, vt_ref),) if own_block else ()):
            for h in range(H_MB):
                hs = slice(h * HEAD_D, (h + 1) * HEAD_D)
                pv = jnp.dot(v_ref[b, hs, :], p_scr[hb + h], preferred_element_type=F32)
                acc_scr[hb + h] = a_scr[hb + h] * acc_scr[hb + h] + pv

    def for_each_parity(cond, bias_of, own_block=False):
        for par in range(2):
            @pl.when(cond & (kj % 2 == par))
            def _(par=par):
                for b in range(nseq):
                    attend(b, bias_of, par, own_block)

    def selrow(b, h, qh):
        return sel_scr[b * H_MB + h, pl.ds(kj, 1), :][:, qh * LANES:(qh + 1) * LANES]

    has_steps = tab_ref[delta, 3] < 2 * blk

    for_each_parity(delta == 0, lambda b, h, qh: near_scr[h, :, qh * LANES:(qh + 1) * LANES], own_block=True)

    for_each_parity(delta == 1,
                    lambda b, h, qh: near_scr[H_MB + h, :, qh * LANES:(qh + 1) * LANES] + selrow(b, h, qh))

    for_each_parity((delta >= 2) & jnp.logical_not(has_steps),
                    lambda b, h, qh: rb_ref[tab_ref[delta, 0], h] + selrow(b, h, qh))

    dts0 = dts[:, :LANES]

    def stepped_bias(b, h, qh):
        sr = selrow(b, h, qh)
        r0 = rb_ref[tab_ref[delta, 0], h] + sr
        r1 = rb_ref[tab_ref[delta, 1], h] + sr
        r2 = rb_ref[tab_ref[delta, 2], h] + sr
        m1 = dts0 >= tab_ref[delta, 3] - qh * LANES
        m2 = dts0 >= tab_ref[delta, 4] - qh * LANES
        return jnp.where(m2, r2, jnp.where(m1, r1, r0))

    for_each_parity((delta >= 2) & has_steps, stepped_bias)

    @pl.when(delta == 0)
    def _():
        for b in range(nseq):
            hb = b * H_MB
            o_ref[b] = jnp.concatenate([acc_scr[hb + h] / l_scr[hb + h] for h in range(H_MB)], axis=0).T


def _moba_prompt(q, k, vt, ksum, rel_bias, batch):
    n = q.shape[0]
    t = n // batch
    assert t % MB_BLOCK == 0
    nb = t // MB_BLOCK
    qi = np.array([i for i in range(nb) for _ in range(i + 1)], np.int32)
    kj = np.array([j for i in range(nb) for j in range(i + 1)], np.int32)
    tab = _far_block_table(nb)
    blk = MB_BLOCK
    rows = batch * H_MB
    grid_spec = pltpu.PrefetchScalarGridSpec(
        num_scalar_prefetch=3,
        grid=(len(qi),),
        in_specs=[
            pl.BlockSpec((batch, blk, WIDTH), lambda p, qi, kj, tab: (0, qi[p], 0)),
            pl.BlockSpec((batch, blk, WIDTH), lambda p, qi, kj, tab: (0, kj[p], 0)),
            pl.BlockSpec((batch, WIDTH, blk), lambda p, qi, kj, tab: (0, 0, kj[p])),
            pl.BlockSpec((batch, WIDTH, blk), lambda p, qi, kj, tab: (0, 0, jnp.maximum(kj[p] - 1, 0))),
            pl.BlockSpec((batch, nb, WIDTH), lambda p, qi, kj, tab: (0, 0, 0)),
            pl.BlockSpec(memory_space=pltpu.SMEM),
        ],
        out_specs=pl.BlockSpec((batch, blk, WIDTH), lambda p, qi, kj, tab: (0, qi[p], 0)),
        scratch_shapes=[pltpu.VMEM((rows, LANES, blk), BF16), pltpu.VMEM((rows, nb, blk), F32),
                        pltpu.VMEM((rows, 1, blk), F32), pltpu.VMEM((rows, 1, blk), F32),
                        pltpu.VMEM((rows, HEAD_D, blk), F32), pltpu.VMEM((2 * H_MB, blk, blk), F32),
                        pltpu.VMEM((rows, 1, blk), F32), pltpu.VMEM((rows, blk, blk), BF16),
                        pltpu.VMEM((rows, 1, blk), F32), pltpu.VMEM((rows, blk, blk), BF16)],
    )
    seq = lambda a: a.reshape(batch, t, WIDTH)
    return pl.pallas_call(
        _moba_prompt_kernel,
        grid_spec=grid_spec,
        out_shape=jax.ShapeDtypeStruct((batch, t, WIDTH), F32),
        compiler_params=_cparams("arbitrary"),
        name="moba_prompt",
    )(jnp.asarray(qi), jnp.asarray(kj), jnp.asarray(tab), seq(q), seq(k), vt, vt, ksum, rel_bias).reshape(n, WIDTH)


SAMPLE_PAGES_PER_STEP = 32


def _page_ksum_kernel(pt_ref, *refs, ppb):
    pages, out_ref = refs[:-1], refs[-1]
    j = pl.program_id(1)
    nblk = out_ref.shape[2]
    lane = lax.broadcasted_iota(jnp.int32, (WIDTH, nblk), 1)

    @pl.when(j == 0)
    def _():
        out_ref[...] = jnp.zeros(out_ref.shape, F32)

    acc = out_ref[0]
    for i in range(0, len(pages), ppb):
        blk = pages[i][0].reshape(WIDTH, pages[i].shape[-1])
        for e in range(1, ppb):
            blk = blk + pages[i + e][0].reshape(blk.shape)
        col = jnp.sum(blk, axis=1, keepdims=True)
        acc = jnp.where(lane == j * (len(pages) // ppb) + i // ppb, col, acc)
    out_ref[0] = acc


def _page_ksum(pool_t, page_table):
    db, n_pages = page_table.shape
    page = pool_t.shape[-1]
    ppb = MB_BLOCK // page
    pps = min(SAMPLE_PAGES_PER_STEP, n_pages)
    assert MB_BLOCK % page == 0 and n_pages % pps == 0 and pps % ppb == 0
    nblk = n_pages // ppb
    specs = [pl.BlockSpec((1,) + pool_t.shape[1:], (lambda b, j, pt, i=i: (pt[b, j * pps + i], 0, 0, 0)))
             for i in range(pps)]
    grid_spec = pltpu.PrefetchScalarGridSpec(
        num_scalar_prefetch=1, grid=(db, n_pages // pps), in_specs=specs,
        out_specs=pl.BlockSpec((1, WIDTH, nblk), lambda b, j, pt: (b, 0, 0)))
    return pl.pallas_call(
        functools.partial(_page_ksum_kernel, ppb=ppb), grid_spec=grid_spec,
        out_shape=jax.ShapeDtypeStruct((db, WIDTH, nblk), F32),
        compiler_params=_cparams("parallel", "arbitrary"), name="page_ksum",
    )(page_table, *([pool_t] * pps))


def _sample_select_kernel(q_ref, ksum_ref, idx_ref):
    q = q_ref[...]
    kmean_t = ksum_ref[0] * (1.0 / MB_BLOCK)
    nblk = kmean_t.shape[1]
    cols = []
    for h in range(H_MB):
        hs = slice(h * HEAD_D, (h + 1) * HEAD_D)
        gate = jnp.dot(q[:, hs], kmean_t[hs, :], precision=lax.Precision.HIGHEST,
                       preferred_element_type=F32)
        idxs, _, _ = _top_blocks(gate, nblk, MB_TOPK, axis=1)
        cols.extend(idxs)
    idx_ref[0] = jnp.concatenate(cols, axis=1)


def _sample_select(q, ksum_t, ds):
    db, _, nblk = ksum_t.shape
    assert nblk >= MB_TOPK
    return pl.pallas_call(
        _sample_select_kernel, grid=(db,),
        in_specs=[pl.BlockSpec((ds, WIDTH), lambda b: (b, 0)), pl.BlockSpec((1, WIDTH, nblk), lambda b: (b, 0, 0))],
        out_specs=pl.BlockSpec((1, ds, H_MB * MB_TOPK), lambda b: (b, 0, 0)),
        out_shape=jax.ShapeDtypeStruct((db, ds, H_MB * MB_TOPK), jnp.int32),
        compiler_params=_cparams("parallel"), name="sample_select",
    )(q, ksum_t)


def _sample_attend_kernel(pt_ref, sel_ref, q_ref, kn_ref, vn_ref, rb_ref, kpool_ref, vpool_ref, o_ref,
                          kbuf, vbuf, sems, *, ds, page, past_len):
    ppb = MB_BLOCK // page
    ntile = MB_TOPK * ppb
    step = pl.program_id(0)
    nstep = pl.num_programs(0)

    def copies(st, slot):
        b, h = st // H_MB, st % H_MB
        out = []
        for q in range(ds):
            for r in range(MB_TOPK):
                blk = sel_ref[((b * ds + q) * H_MB + h) * MB_TOPK + r]
                for e in range(ppb):
                    pg = pt_ref[b, blk * ppb + e]
                    dst = pl.ds((r * ppb + e) * page, page)
                    out.append(pltpu.make_async_copy(kpool_ref.at[pg, h], kbuf.at[slot, q, :, dst], sems.at[0, slot]))
                    out.append(pltpu.make_async_copy(vpool_ref.at[pg, h], vbuf.at[slot, q, :, dst], sems.at[1, slot]))
        return out

    @pl.when(step == 0)
    def _():
        for i, cp in enumerate(copies(step, 0)):
            cp.start(priority=i % 2)

    @pl.when(step + 1 < nstep)
    def _():
        for i, cp in enumerate(copies(step + 1, (step + 1) % 2)):
            cp.start(priority=i % 2)

    slot = step % 2
    for qq in range(ds):
        for tile in range(ntile):
            dst = pl.ds(tile * page, page)
            pltpu.make_async_copy(kpool_ref.at[0, 0], kbuf.at[slot, qq, :, dst], sems.at[0, slot]).wait()
            pltpu.make_async_copy(vpool_ref.at[0, 0], vbuf.at[slot, qq, :, dst], sems.at[1, slot]).wait()

    b, h = step // H_MB, step % H_MB
    q = (q_ref[0] * (HEAD_D ** -0.5)).astype(BF16)
    rowi = lax.broadcasted_iota(jnp.int32, (ds, ntile * page), 0)
    s_sel = jnp.zeros((ds, ntile * page), F32)
    kpos = jnp.zeros((ds, ntile * page), jnp.int32)
    lane = lax.broadcasted_iota(jnp.int32, (ds, page), 1)
    for qq in range(ds):
        res = jnp.dot(q, kbuf[slot, qq].astype(BF16), preferred_element_type=F32)
        s_sel = jnp.where(rowi == qq, res, s_sel)
        pieces = []
        for r in range(MB_TOPK):
            blk = sel_ref[((b * ds + qq) * H_MB + h) * MB_TOPK + r]
            for e in range(ppb):
                pieces.append(blk * MB_BLOCK + e * page + lane)
        kpos = jnp.where(rowi == qq, jnp.concatenate(pieces, axis=1), kpos)
    q_pos = past_len + lax.broadcasted_iota(jnp.int32, (ds, 1), 0)
    s_sel = s_sel + _bias_of_distance(q_pos - kpos, rb_ref, h)
    own = lax.broadcasted_iota(jnp.int32, (ds, ds), 1)
    qrow = lax.broadcasted_iota(jnp.int32, (ds, ds), 0)
    s_own = jnp.dot(q, kn_ref[0].astype(BF16), preferred_element_type=F32) \
        + _bias_of_distance(qrow - own, rb_ref, h)
    s_own = jnp.where(own <= qrow, s_own, NEG)
    m = jnp.maximum(jnp.max(s_sel, axis=1, keepdims=True), jnp.max(s_own, axis=1, keepdims=True))
    p_sel = jnp.exp(s_sel - m)
    p_own = jnp.exp(s_own - m)
    den = jnp.sum(p_sel, axis=1, keepdims=True) + jnp.sum(p_own, axis=1, keepdims=True)
    nt_dims = (((1,), (1,)), ((), ()))
    o = lax.dot_general(p_own.astype(BF16), vn_ref[0].astype(BF16), nt_dims, preferred_element_type=F32)
    rowo = lax.broadcasted_iota(jnp.int32, (ds, HEAD_D), 0)
    p_sel_b = p_sel.astype(BF16)
    for qq in range(ds):
        res = lax.dot_general(p_sel_b, vbuf[slot, qq].astype(BF16), nt_dims, preferred_element_type=F32)
        o = o + jnp.where(rowo == qq, res, 0.0)
    o_ref[0] = o / den


def _sample_attend(q, kt_new, vt_new, pool_kt, pool_vt, page_table, sel_flat, rel_bias, ds):
    db, n_pages = page_table.shape
    page = pool_kt.shape[-1]
    ppb = MB_BLOCK // page
    past_len = n_pages * page
    assert past_len % MB_BLOCK == 0
    ntile = MB_TOPK * ppb
    qh = q.reshape(db, ds, H_MB, HEAD_D).transpose(0, 2, 1, 3).reshape(db * H_MB, ds, HEAD_D)
    per_head = lambda shape: pl.BlockSpec((1,) + shape, lambda s, pt, sel: (s, 0, 0))
    grid_spec = pltpu.PrefetchScalarGridSpec(
        num_scalar_prefetch=2, grid=(db * H_MB,),
        in_specs=[per_head((ds, HEAD_D)), per_head((HEAD_D, ds)), per_head((HEAD_D, ds)),
                  pl.BlockSpec(memory_space=pltpu.SMEM), pl.BlockSpec(memory_space=pl.ANY),
                  pl.BlockSpec(memory_space=pl.ANY)],
        out_specs=per_head((ds, HEAD_D)),
        scratch_shapes=[pltpu.VMEM((2, ds, HEAD_D, ntile * page), F32), pltpu.VMEM((2, ds, HEAD_D, ntile * page), F32),
                        pltpu.SemaphoreType.DMA((2, 2))])
    kern = functools.partial(_sample_attend_kernel, ds=ds, page=page, past_len=past_len)
    return pl.pallas_call(
        kern, grid_spec=grid_spec, out_shape=jax.ShapeDtypeStruct((db * H_MB, ds, HEAD_D), F32),
        compiler_params=_cparams("arbitrary"), name="sample_attend",
    )(page_table, sel_flat, qh, kt_new, vt_new, rel_bias, pool_kt, pool_vt)


def _memkv_kernel(m_ref, g_ref, w_ref, kn_ref, ones_ref, k_ref, v_ref):
    xb = _rmsnorm_rows(m_ref[...], g_ref[...]).astype(BF16)
    wk = w_ref.shape[1] // 2
    k = jnp.dot(xb, w_ref[:, :wk], preferred_element_type=F32)
    k_ref[...] = _seg_rmsnorm(k, kn_ref[...], ones_ref[...], MEM_HD)
    v_ref[...] = jnp.dot(xb, w_ref[:, wk:], preferred_element_type=F32)


def _memkv(mem, g, w_bf16, kn, tm):
    n, d = mem.shape
    wk = w_bf16.shape[1] // 2
    out = pl.BlockSpec((tm, wk), lambda i: (i, 0))
    return pl.pallas_call(
        _memkv_kernel, grid=(n // tm,),
        in_specs=[pl.BlockSpec((tm, d), lambda i: (i, 0)), _const_spec((1, d)), _const_spec(w_bf16.shape),
                  _const_spec((1, wk)), _const_spec((wk, wk))],
        out_specs=[out, out], out_shape=[jax.ShapeDtypeStruct((n, wk), F32)] * 2,
        compiler_params=_cparams("parallel"), name="mem_kv",
    )(mem, g, w_bf16, kn, jnp.asarray(_group_ones(wk, MEM_HD), BF16))


def _mixmem_kernel(x_ref, hg_ref, omb_ref, wout_ref, gmem_ref, wq_ref, qn_ref, ones_ref, mk_ref, mv_ref, wo_ref,
                   h_ref, *, nseq):
    mix = (jnp.dot(hg_ref[...].astype(BF16), wout_ref[:WIDTH, :], preferred_element_type=F32)
           + jnp.dot(omb_ref[...].astype(BF16), wout_ref[WIDTH:, :], preferred_element_type=F32))
    h1 = x_ref[...] + mix
    hn = _rmsnorm_rows(h1, gmem_ref[...]).astype(BF16)
    q = _seg_rmsnorm(jnp.dot(hn, wq_ref[...], preferred_element_type=F32), qn_ref[...], ones_ref[...], MEM_HD)
    tm = q.shape[0] // nseq
    per_seq = []
    for i in range(nseq):
        qb = q[i * tm:(i + 1) * tm].astype(BF16)
        mk = mk_ref[i].astype(BF16)
        mv = mv_ref[i].astype(BF16)
        outs = []
        for h in range(H_MEM):
            sl = slice(h * MEM_HD, (h + 1) * MEM_HD)
            s = lax.dot_general(qb[:, sl], mk[:, sl], (((1,), (1,)), ((), ())),
                                preferred_element_type=F32) * (MEM_HD ** -0.5)
            e = jnp.exp(s - jnp.max(s, axis=-1, keepdims=True))
            p = e / jnp.sum(e, axis=-1, keepdims=True)
            outs.append(jnp.dot(p.astype(BF16), mv[:, sl], preferred_element_type=F32))
        per_seq.append(jnp.concatenate(outs, axis=1))
    o = jnp.concatenate(per_seq, axis=0).astype(BF16)
    h_ref[...] = h1 + jnp.dot(o, wo_ref[...], preferred_element_type=F32)


def _tail_seqs(batch, nt):
    return math.gcd(batch, SHORT_SEQS) if nt == 1 else 1


def _mixmem(x, hg, omb, wout, gmem, wq, qn, mk, mv, wo, batch, tm):
    n, d = x.shape
    t = n // batch
    assert t % tm == 0
    nt = t // tm
    nseq = _tail_seqs(batch, nt)
    n_mem, wm = mk.shape[1:]
    tok = lambda w: pl.BlockSpec((nseq * tm, w), lambda b, i: (b * nt + i, 0))
    mem = pl.BlockSpec((nseq, n_mem, wm), lambda b, i: (b, 0, 0))
    return pl.pallas_call(
        functools.partial(_mixmem_kernel, nseq=nseq), grid=(batch // nseq, nt),
        in_specs=[tok(d), tok(WIDTH), tok(WIDTH), _const_spec(wout.shape), _const_spec((1, d)),
                  _const_spec(wq.shape), _const_spec((1, wm)), _const_spec((wm, wm)), mem, mem,
                  _const_spec(wo.shape)],
        out_specs=tok(d), out_shape=jax.ShapeDtypeStruct((n, d), F32),
        compiler_params=_cparams("parallel", "parallel"), name="mix_mem",
    )(x, hg, omb, wout, gmem, wq, qn, jnp.asarray(_group_ones(wm, MEM_HD), BF16), mk, mv, wo)


FFN_CHUNKS = 1
SHORT_SEQS = 8


def _ffn_kernel(h_ref, g_ref, wup_ref, cw_ref, cb_ref, wdn_ref, prev_ref, y_ref, cst_ref, carry_scr, *, dff, nseq):
    c = pl.program_id(1)

    @pl.when(c == 0)
    def _():
        carry_scr[...] = prev_ref[...]

    h = h_ref[...]
    hn = _rmsnorm_rows(h, g_ref[...]).astype(BF16)
    tm = h.shape[0] // nseq
    fc = dff // FFN_CHUNKS
    row = lax.broadcasted_iota(jnp.int32, (nseq * tm, fc), 0) % tm
    acc = h
    for ci in range(FFN_CHUNKS):
        sl = slice(ci * fc, (ci + 1) * fc)
        u = jnp.dot(hn, wup_ref[:, sl], preferred_element_type=F32)
        v = jnp.dot(hn, wup_ref[:, dff + ci * fc:dff + (ci + 1) * fc], preferred_element_type=F32)

        def carried(r):
            if nseq == 1:
                return carry_scr[0, r:r + 1, sl]
            return jnp.concatenate([jnp.broadcast_to(carry_scr[i, r:r + 1, sl], (tm, fc)) for i in range(nseq)], axis=0)

        u1 = jnp.where(row == 0, carried(1), pltpu.roll(u, 1, 0))
        u2 = jnp.where(row == 0, carried(0), jnp.where(row == 1, carried(1), pltpu.roll(u, 2, 0)))
        cw = cw_ref[:, sl]
        conv = cb_ref[:, sl] + u2 * cw[0:1] + u1 * cw[1:2] + u * cw[2:3]
        act = 0.5 * conv * (1.0 + lax.erf(conv * (2.0 ** -0.5))) * v
        acc = acc + jnp.dot(act.astype(BF16), wdn_ref[sl, :], preferred_element_type=F32)
        for i in range(nseq):
            last = u[(i + 1) * tm - (CONV_W - 1):(i + 1) * tm]
            carry_scr[i, :, sl] = last
            cst_ref[i, :, sl] = last
    y_ref[...] = acc


def _ffn(h, g, wup, cw, cb, wdn, prev, batch, tm):
    n, d = h.shape
    t = n // batch
    assert t % tm == 0 and tm >= CONV_W - 1
    nt = t // tm
    nseq = _tail_seqs(batch, nt)
    dff = wdn.shape[0]
    assert dff % (FFN_CHUNKS * LANES) == 0
    tok = pl.BlockSpec((nseq * tm, d), lambda b, i: (b * nt + i, 0))
    st = pl.BlockSpec((nseq, CONV_W - 1, dff), lambda b, i: (b, 0, 0))
    return pl.pallas_call(
        functools.partial(_ffn_kernel, dff=dff, nseq=nseq), grid=(batch // nseq, nt),
        in_specs=[tok, _const_spec((1, d)), _const_spec(wup.shape), _const_spec(cw.shape), _const_spec((1, dff)),
                  _const_spec(wdn.shape), st],
        out_specs=[tok, st],
        out_shape=[jax.ShapeDtypeStruct((n, d), F32), jax.ShapeDtypeStruct((batch, CONV_W - 1, dff), F32)],
        scratch_shapes=[pltpu.VMEM((nseq, CONV_W - 1, dff), F32)],
        compiler_params=_cparams("parallel", "arbitrary"), name="conv_ffn",
    )(h, g, wup, cw, cb, wdn, prev)


PROMPT_TILE = 512


def _state_to_rows(s):
    b = s.shape[0]
    return s.transpose(0, 3, 1, 2).reshape(b, HEAD_D, WIDTH)


def _rows_to_state(st):
    b = st.shape[0]
    return st.reshape(b, HEAD_D, H_HG, HEAD_D).transpose(0, 2, 3, 1)


def kernel(x_prompt, x_sample, cache_k, cache_v, page_table, state_hgrn, state_conv, cache_mem_k, cache_mem_v, mem_prompt, norm_mix, w_in, hg_lb_logits, hg_out_norm, mb_q_norm, mb_k_norm, rel_bias, w_out, norm_mem, norm_mem_src, w_mem_q, w_mem_kv, mem_q_norm, mem_k_norm, w_mem_o, norm_ffn, w_up, conv_w, conv_b, w_down):
    assert norm_mix.shape[0] == 1, "one layer"
    l = 0
    b, t, dm = x_prompt.shape
    db, ds, _ = x_sample.shape
    n_mem = mem_prompt.shape[1]
    dff = w_down.shape[1]
    row = lambda a: a[None]
    w_in_b, w_out_b = w_in[l].astype(BF16), w_out[l].astype(BF16)
    w_q_b, w_kv_b, w_o_b = w_mem_q[l].astype(BF16), w_mem_kv[l].astype(BF16), w_mem_o[l].astype(BF16)
    w_up_b, w_dn_b = w_up[l].astype(BF16), w_down[l].astype(BF16)
    qn, kn = row(jnp.tile(mb_q_norm[l], H_MB)), row(jnp.tile(mb_k_norm[l], H_MB))
    gn = row(jnp.tile(hg_out_norm[l], H_HG))
    mqn, mkn = row(jnp.tile(mem_q_norm[l], H_MEM)), row(jnp.tile(mem_k_norm[l], H_MEM))

    def layer(x, batch, tile, s0t, moba, mk, mv, prev):
        n = x.shape[0]
        tm = min(PROMPT_TILE, n)
        slabs = batch if (n // batch) % tm == 0 else 1
        qhg, khg, ihg, logf, g, qmb, kmb, ksum, kt, vt, vtb = _inproj(
            x, row(norm_mix[l]), w_in_b, hg_lb_logits, qn, kn, slabs, tm)
        hg, st = _hgrn(qhg, khg, ihg, logf, g, s0t, gn, batch, min(tile, HG_TILE))
        omb = moba(qmb, kmb, ksum, kt, vt, vtb)
        h = _mixmem(x, hg, omb, w_out_b, row(norm_mem[l]), w_q_b, mqn, mk, mv, w_o_b, batch, tile)
        y, cst = _ffn(h, row(norm_ffn[l]), w_up_b, conv_w[l], row(conv_b[l]), w_dn_b, prev, batch, tile)
        return y, kt, vt, _rows_to_state(st), cst

    mk_p, mv_p = _memkv(mem_prompt.reshape(b * n_mem, dm), row(norm_mem_src[l]), w_kv_b, mkn, n_mem)
    moba_p = lambda q, k, ksum, kt, vt, vtb: _moba_prompt(q, k, vtb, ksum.reshape(b, t // MB_BLOCK, WIDTH), rel_bias, b)
    y_p, kt_p, vt_p, s_p, c_p = layer(
        x_prompt.reshape(b * t, dm), b, PROMPT_TILE, jnp.zeros((b, HEAD_D, WIDTH), F32), moba_p,
        mk_p.reshape(b, n_mem, -1), mv_p.reshape(b, n_mem, -1), jnp.zeros((b, CONV_W - 1, dff), F32))

    pool_kt = cache_k[l].transpose(0, 2, 3, 1)
    pool_vt = cache_v[l].transpose(0, 2, 3, 1)
    per_seq = lambda a: a.reshape(WIDTH, db, ds).transpose(1, 0, 2)

    def moba_s(q, k, ksum, kt, vt, vtb):
        sel = _sample_select(q, _page_ksum(pool_kt, page_table), ds)
        o = _sample_attend(q, per_seq(kt).reshape(db * H_MB, HEAD_D, ds), per_seq(vt).reshape(db * H_MB, HEAD_D, ds),
                           pool_kt, pool_vt, page_table, sel.reshape(-1), rel_bias, ds)
        return o.reshape(db, H_MB, ds, HEAD_D).transpose(0, 2, 1, 3).reshape(db * ds, WIDTH)

    y_s, kt_s, vt_s, s_s, c_s = layer(
        x_sample.reshape(db * ds, dm), db, ds, _state_to_rows(state_hgrn[l]), moba_s,
        cache_mem_k[l].reshape(db, n_mem, -1), cache_mem_v[l].reshape(db, n_mem, -1), state_conv[l])

    hd = lambda a, bb, tt: a.reshape(bb, H_MB, HEAD_D, tt).transpose(0, 3, 1, 2)[None]
    return (y_p.reshape(b, t, dm), y_s.reshape(db, ds, dm),
            hd(kt_p, b, t), hd(vt_p, b, t), hd(per_seq(kt_s), db, ds), hd(per_seq(vt_s), db, ds),
            s_p[None], s_s[None], c_p[None], c_s[None],
            mk_p.reshape(1, b, n_mem, H_MEM, MEM_HD), mv_p.reshape(1, b, n_mem, H_MEM, MEM_HD))
```
